```python
import math
import jax, jax.numpy as jnp
from jax import lax
import numpy as np

D_MODEL = 1024
BATCH = 32
SEQ = 256
DEPTH = 4
DEC_BATCH = 8
DEC_SEQ = 4096
PAST_LEN = 512

GRID_W = 64
D_MIX = D_MODEL
GLA_WIDTH = D_MIX // 2
DIFF_WIDTH = D_MIX - GLA_WIDTH
GLA_HEADS = 4
GLA_DK = GLA_WIDTH // GLA_HEADS
GLA_DV = GLA_WIDTH // GLA_HEADS
GLA_GATE_RANK = 16
GLA_GATE_NORMALIZER = 16.0
GLA_CHUNK = 64
DIFF_HEADS = 4
DIFF_HEAD_DIM = DIFF_WIDTH // (2 * DIFF_HEADS)
DIFF_V_DIM = 2 * DIFF_HEAD_DIM
ROPE_HALF = DIFF_HEAD_DIM // 2
ROPE_BASE = 10000.0
Q_BLOCK = 128
N_EXPERTS = 16
N_GROUPS = 4
EXPERTS_PER_GROUP = N_EXPERTS // N_GROUPS
TOP_K = 2
D_EXPERT = D_MODEL // 2
NORM_EPS = 1e-6
IN_SIZES = (GLA_HEADS * GLA_DK, GLA_HEADS * GLA_DK, GLA_HEADS * GLA_DV, GLA_HEADS * GLA_DV,
            GLA_GATE_RANK, GLA_GATE_RANK,
            DIFF_HEADS * 2 * DIFF_HEAD_DIM, DIFF_HEADS * 2 * DIFF_HEAD_DIM, DIFF_HEADS * DIFF_V_DIM)
IN_COLS = sum(IN_SIZES)

kernel_name = 'hybrid_gla_diffattn_moe_diffusion_step'


def rms_norm(x, g):
    xf = x.astype(jnp.float32)
    y = xf * lax.rsqrt(jnp.mean(xf * xf, axis=-1, keepdims=True) + NORM_EPS)
    return (y * g.astype(jnp.float32)).astype(x.dtype)


def modulated_rms_norm(x, g, shift, scale):
    xf = x.astype(jnp.float32)
    y = xf * lax.rsqrt(jnp.mean(xf * xf, axis=-1, keepdims=True) + NORM_EPS)
    y = y * g.astype(jnp.float32) * (1.0 + scale.astype(jnp.float32)) + shift.astype(jnp.float32)
    return y.astype(x.dtype)


def split_columns(p):
    outs = []
    start = 0
    for size in IN_SIZES:
        outs.append(p[..., start:start + size])
        start += size
    return outs


def gla_chunked(q, k, v, log_a, s0):
    B, L, H, _ = q.shape
    DV = v.shape[-1]
    N = L // GLA_CHUNK
    f32 = jnp.float32

    def chunks(t):
        return t.astype(f32).reshape(B, N, GLA_CHUNK, H, t.shape[-1])

    q, k, v, log_a = chunks(q), chunks(k), chunks(v), chunks(log_a)
    b = jnp.cumsum(log_a, axis=2)
    mid = b[:, :, GLA_CHUNK // 2:GLA_CHUNK // 2 + 1]
    last = b[:, :, GLA_CHUNK - 1:]
    att = jnp.einsum('bnthd,bnshd->bnhts', q * jnp.exp(b - mid), k * jnp.exp(mid - b))
    mask = jnp.tril(jnp.ones((GLA_CHUNK, GLA_CHUNK), dtype=bool))
    att = jnp.where(mask, att, 0.0)
    o_intra = jnp.einsum('bnhts,bnshv->bnthv', att, v)
    q_inter = jnp.swapaxes(q * jnp.exp(b), 0, 1)
    k_state = jnp.swapaxes(k * jnp.exp(last - b), 0, 1)
    v_n = jnp.swapaxes(v, 0, 1)
    decay = jnp.swapaxes(jnp.exp(last[:, :, 0]), 0, 1)

    def step(S, inp):
        qi, ki, vi, di = inp
        o = jnp.einsum('bthd,bhdv->bthv', qi, S)
        S = di[..., None] * S + jnp.einsum('bshd,bshv->bhdv', ki, vi)
        return S, o

    s_fin, o_inter = lax.scan(step, s0.astype(f32), (q_inter, k_state, v_n, decay))
    o = o_intra + jnp.swapaxes(o_inter, 0, 1)
    return o.reshape(B, L, H, DV), s_fin.astype(s0.dtype)


def gla_bidirectional(q, k, v, la_f, la_b, s0_f, s0_b):
    o_f, s_f = gla_chunked(q, k, v, la_f, s0_f)
    rev = lambda t: jnp.flip(t, axis=1)
    o_b, s_b = gla_chunked(rev(q), rev(k), rev(v), rev(la_b), s0_b)
    return (o_f + rev(o_b)).astype(v.dtype), s_f, s_b


def axial_rope_tables(length):
    rows = length // GRID_W
    row = jnp.repeat(jnp.arange(rows, dtype=jnp.float32), GRID_W)
    col = jnp.tile(jnp.arange(GRID_W, dtype=jnp.float32), rows)
    inv = ROPE_BASE ** (-jnp.arange(0, ROPE_HALF, 2, dtype=jnp.float32) / ROPE_HALF)
    ang_r = row[:, None] * inv[None, :]
    ang_c = col[:, None] * inv[None, :]
    shape = (1, length, 1, 1, ROPE_HALF // 2)
    return (jnp.cos(ang_r).reshape(shape), jnp.sin(ang_r).reshape(shape),
            jnp.cos(ang_c).reshape(shape), jnp.sin(ang_c).reshape(shape))


def rotate(x, cos, sin):
    x1, x2 = jnp.split(x, 2, axis=-1)
    return jnp.concatenate([x1 * cos - x2 * sin, x1 * sin + x2 * cos], axis=-1)


def axial_rope(x, tables):
    cr, sr, cc, sc = tables
    xf = x.astype(jnp.float32)
    return jnp.concatenate([rotate(xf[..., :ROPE_HALF], cr, sr),
                            rotate(xf[..., ROPE_HALF:], cc, sc)], axis=-1).astype(x.dtype)


def diff_attention(q, k, v, lam):
    B, Lq, H, _, Dh = q.shape
    nblk = Lq // Q_BLOCK
    qb = jnp.moveaxis(q.reshape(B, nblk, Q_BLOCK, H, 2, Dh), 1, 0)
    scale = Dh ** -0.5

    def block(qi):
        s = jnp.einsum('bqhcd,bkhcd->bhcqk', qi, k).astype(jnp.float32) * scale
        p = jax.nn.softmax(s, axis=-1)
        w = p[:, :, 0] - lam * p[:, :, 1]
        return jnp.einsum('bhqk,bkhv->bqhv', w.astype(v.dtype), v)

    o = lax.map(block, qb)
    return jnp.moveaxis(o, 0, 1).reshape(B, Lq, H, v.shape[-1])


def moe(h, w_router, b_router, w_eg, w_eu, w_ed):
    B, L, D = h.shape
    t = h.reshape(B * L, D)
    s = jax.nn.sigmoid((t @ w_router).astype(jnp.float32))
    sel = (s + b_router.astype(jnp.float32)).reshape(-1, N_GROUPS, EXPERTS_PER_GROUP)
    group_score = jnp.sum(lax.top_k(sel, TOP_K)[0], axis=-1)
    g_idx = jnp.argmax(group_score, axis=-1)
    in_group = jnp.take_along_axis(sel, g_idx[:, None, None], axis=1)[:, 0]
    _, local = lax.top_k(in_group, TOP_K)
    e_idx = g_idx[:, None] * EXPERTS_PER_GROUP + local
    w = jnp.take_along_axis(s, e_idx, axis=1)
    w = w / jnp.sum(w, axis=-1, keepdims=True)
    gates = jnp.sum(jax.nn.one_hot(e_idx, N_EXPERTS, dtype=jnp.float32) * w[..., None], axis=1).astype(h.dtype)
    y = jnp.zeros_like(t)
    for e in range(N_EXPERTS):
        a = jax.nn.silu(t @ w_eg[e]) * (t @ w_eu[e])
        y = y + gates[:, e:e + 1] * (a @ w_ed[e])
    return y.reshape(B, L, D)


def trunk_layer(x, mod, lam, lam_init, lw, w_router, b_router, s0_f, s0_b, ctx_k, ctx_v, rope):
    (g1, g2, w_in_l, w_af, b_af, w_ab, b_ab, g_gla, g_q, g_k, g_diff, w_out_l, w_eg, w_eu, w_ed) = lw
    B, L, _ = x.shape
    sh1, sc1, gt1, sh2, sc2, gt2 = jnp.split(mod, 6, axis=-1)
    h = modulated_rms_norm(x, g1, sh1, sc1)
    q_g, k_g, v_g, g_g, a_f, a_b, q_d, k_d, v_d = split_columns(h @ w_in_l)
    heads = lambda t, d: t.reshape(B, L, -1, d)
    la_f = jax.nn.log_sigmoid((a_f @ w_af + b_af).astype(jnp.float32)) / GLA_GATE_NORMALIZER
    la_b = jax.nn.log_sigmoid((a_b @ w_ab + b_ab).astype(jnp.float32)) / GLA_GATE_NORMALIZER
    o_g, s_f, s_b = gla_bidirectional(heads(q_g, GLA_DK) * GLA_DK ** -0.5, heads(k_g, GLA_DK), heads(v_g, GLA_DV),
                                      heads(la_f, GLA_DK), heads(la_b, GLA_DK), s0_f, s0_b)
    o_g = (rms_norm(o_g, g_gla) * jax.nn.silu(heads(g_g, GLA_DV))).reshape(B, L, GLA_WIDTH)
    q_d = rms_norm(q_d.reshape(B, L, DIFF_HEADS, 2, DIFF_HEAD_DIM), g_q)
    k_d = rms_norm(k_d.reshape(B, L, DIFF_HEADS, 2, DIFF_HEAD_DIM), g_k)
    v_d = v_d.reshape(B, L, DIFF_HEADS, DIFF_V_DIM)
    if rope is None:
        keys, vals = k_d, v_d
    else:
        q_d = axial_rope(q_d, rope)
        keys = jnp.concatenate([ctx_k.astype(k_d.dtype), axial_rope(k_d, rope)], axis=1)
        vals = jnp.concatenate([ctx_v.astype(v_d.dtype), v_d], axis=1)
    o_d = diff_attention(q_d, keys, vals, lam)
    o_d = (rms_norm(o_d, g_diff) * (1.0 - lam_init)).reshape(B, L, DIFF_WIDTH)
    out = jnp.concatenate([o_g, o_d], axis=-1) @ w_out_l
    x = x + gt1 * out
    h = modulated_rms_norm(x, g2, sh2, sc2)
    x = x + gt2 * moe(h, w_router, b_router, w_eg, w_eu, w_ed)
    return x, k_d, v_d, s_f, s_b


def setup_inputs(seed: int = 0) -> dict:
    key = jax.random.key(seed)
    ks = iter(jax.random.split(key, 40))

    def nrm(shape, s=1.0):
        return s * jax.random.normal(next(ks), shape, jnp.float32)

    def gain(shape):
        return 1.0 + 0.02 * nrm(shape)

    D = D_MODEL
    return {
        'x_prompt': nrm((BATCH, SEQ, D)),
        'x_sample': nrm((DEC_BATCH, DEC_SEQ, D)),
        'cache_k': nrm((DEC_BATCH, DEPTH, PAST_LEN, DIFF_HEADS, 2, DIFF_HEAD_DIM)),
        'cache_v': nrm((DEC_BATCH, DEPTH, PAST_LEN, DIFF_HEADS, DIFF_V_DIM)),
        'state_gla_fwd': nrm((DEC_BATCH, DEPTH, GLA_HEADS, GLA_DK, GLA_DV)),
        'state_gla_bwd': nrm((DEC_BATCH, DEPTH, GLA_HEADS, GLA_DK, GLA_DV)),
        'c': nrm((DEC_BATCH, D)),
        'c_ctx': nrm((D,)),
        'w_mod': nrm((DEPTH, D, 6 * D), 0.5 * D ** -0.5),
        'b_mod': nrm((DEPTH, 6 * D), 0.01),
        'g_norm1': gain((DEPTH, D)),
        'g_norm2': gain((DEPTH, D)),
        'w_in': nrm((DEPTH, D, IN_COLS), D ** -0.5),
        'w_gate_fwd': nrm((DEPTH, GLA_GATE_RANK, GLA_WIDTH), GLA_GATE_RANK ** -0.5),
        'b_gate_fwd': nrm((DEPTH, GLA_WIDTH), 0.1),
        'w_gate_bwd': nrm((DEPTH, GLA_GATE_RANK, GLA_WIDTH), GLA_GATE_RANK ** -0.5),
        'b_gate_bwd': nrm((DEPTH, GLA_WIDTH), 0.1),
        'g_gla_out': gain((DEPTH, GLA_DV)),
        'g_q_norm': gain((DEPTH, DIFF_HEAD_DIM)),
        'g_k_norm': gain((DEPTH, DIFF_HEAD_DIM)),
        'lambda_q1': nrm((DEPTH, DIFF_HEAD_DIM), 0.1),
        'lambda_k1': nrm((DEPTH, DIFF_HEAD_DIM), 0.1),
        'lambda_q2': nrm((DEPTH, DIFF_HEAD_DIM), 0.1),
        'lambda_k2': nrm((DEPTH, DIFF_HEAD_DIM), 0.1),
        'g_diff_out': gain((DEPTH, DIFF_V_DIM)),
        'w_out': nrm((DEPTH, D_MIX, D), D_MIX ** -0.5),
        'w_router': nrm((D, N_EXPERTS), D ** -0.5),
        'b_router': nrm((N_EXPERTS,), 0.01),
        'w_exp_gate': nrm((DEPTH, N_EXPERTS, D, D_EXPERT), D ** -0.5),
        'w_exp_up': nrm((DEPTH, N_EXPERTS, D, D_EXPERT), D ** -0.5),
        'w_exp_down': nrm((DEPTH, N_EXPERTS, D_EXPERT, D), D_EXPERT ** -0.5),
    }


def reference(x_prompt, x_sample, cache_k, cache_v, state_gla_fwd, state_gla_bwd, c, c_ctx,
              w_mod, b_mod, g_norm1, g_norm2, w_in, w_gate_fwd, b_gate_fwd, w_gate_bwd, b_gate_bwd,
              g_gla_out, g_q_norm, g_k_norm, lambda_q1, lambda_k1, lambda_q2, lambda_k2, g_diff_out,
              w_out, w_router, b_router, w_exp_gate, w_exp_up, w_exp_down):
    f32 = jnp.float32
    silu_ctx = jax.nn.silu(c_ctx.astype(f32))
    silu_c = jax.nn.silu(c.astype(f32))
    rope = axial_rope_tables(x_sample.shape[1])
    xp, xs = x_prompt, x_sample
    nb = xp.shape[0]
    new_k, new_v, new_sf, new_sb = [], [], [], []
    for l in range(DEPTH):
        mod_p = (silu_ctx @ w_mod[l] + b_mod[l]).astype(xp.dtype)[None, None, :]
        mod_s = (silu_c @ w_mod[l] + b_mod[l]).astype(xs.dtype)[:, None, :]
        lam_init = 0.8 - 0.6 * math.exp(-0.3 * l)
        lam = (jnp.exp(jnp.sum(lambda_q1[l].astype(f32) * lambda_k1[l].astype(f32)))
               - jnp.exp(jnp.sum(lambda_q2[l].astype(f32) * lambda_k2[l].astype(f32))) + lam_init)
        lw = (g_norm1[l], g_norm2[l], w_in[l], w_gate_fwd[l], b_gate_fwd[l], w_gate_bwd[l], b_gate_bwd[l],
              g_gla_out[l], g_q_norm[l], g_k_norm[l], g_diff_out[l], w_out[l],
              w_exp_gate[l], w_exp_up[l], w_exp_down[l])
        zero_state = jnp.zeros((nb, GLA_HEADS, GLA_DK, GLA_DV), xp.dtype)
        xp, k_l, v_l, sf_l, sb_l = trunk_layer(xp, mod_p, lam, lam_init, lw, w_router, b_router,
                                               zero_state, zero_state, None, None, None)
        new_k.append(k_l)
        new_v.append(v_l)
        new_sf.append(sf_l)
        new_sb.append(sb_l)
        xs, _, _, _, _ = trunk_layer(xs, mod_s, lam, lam_init, lw, w_router, b_router,
                                     state_gla_fwd[:, l], state_gla_bwd[:, l], cache_k[:, l], cache_v[:, l], rope)
    new_cache_k = jnp.stack(new_k, axis=1)
    new_cache_v = jnp.stack(new_v, axis=1)
    new_state_gla_fwd = jnp.stack(new_sf, axis=1)
    new_state_gla_bwd = jnp.stack(new_sb, axis=1)
    return (xp, xs, new_cache_k, new_cache_v, new_state_gla_fwd, new_state_gla_bwd)
```

```python
import functools
import math

import jax
import jax.numpy as jnp
from jax import lax
from jax.experimental import pallas as pl
from jax.experimental.pallas import tpu as pltpu

F32 = jnp.float32
BF16 = jnp.bfloat16

D_MODEL = 1024
GLA_HEADS = 4
GLA_DK = 128
GLA_DV = 128
GLA_WIDTH = GLA_HEADS * GLA_DK
GLA_GATE_RANK = 16
GLA_GATE_NORMALIZER = 16.0
GLA_CHUNK = 64
DIFF_HEADS = 4
DIFF_HEAD_DIM = 64
DIFF_V_DIM = 128
DIFF_WIDTH = DIFF_HEADS * 2 * DIFF_HEAD_DIM
ROPE_HALF = DIFF_HEAD_DIM // 2
ROPE_BASE = 10000.0
GRID_W = 64
N_EXPERTS = 16
N_GROUPS = 4
EXPERTS_PER_GROUP = 4
D_EXPERT = 512
NORM_EPS = 1e-6

PAIR_A = (0, 0, 0, 1, 1, 3)
PAIR_B = (1, 2, 3, 3, 2, 2)
N_PAIRS = len(PAIR_A)
N_COMBOS = N_GROUPS * N_PAIRS

GATE_COLS = 128
MAIN_COLS = 7 * 512
TOKEN_TILE = 256
MOE_TILE = 256
ATTN_TQ = 256
ATTN_TK = 512
VMEM_LIMIT = 56 * 1024 * 1024


def _cparams(sem):
    return pltpu.CompilerParams(dimension_semantics=sem, vmem_limit_bytes=VMEM_LIMIT)


def _dot(a, b):
    return jnp.dot(a, b, preferred_element_type=F32)


def _dot_nt(a, b):
    return lax.dot_general(a, b, (((1,), (1,)), ((), ())), preferred_element_type=F32)


def _dot_tn(a, b):
    return lax.dot_general(a, b, (((0,), (0,)), ((), ())), preferred_element_type=F32)


def _sigmoid(x):
    return 1.0 / (1.0 + jnp.exp(-x))


def _split_bf16(x):
    hi = x.astype(BF16)
    lo = (x - hi.astype(F32)).astype(BF16)
    return hi, lo


def _mod_kernel(c_ref, w_ref, b_ref, o_ref):
    c = c_ref[...]
    s = c * _sigmoid(c)
    o_ref[0] = jnp.dot(s, w_ref[0], preferred_element_type=F32,
                       precision=lax.Precision.HIGHEST) + b_ref[0]


def _modulation(cs, w_mod, b_mod):
    depth, d, n = w_mod.shape
    rows = cs.shape[0]
    tn = 1536
    return pl.pallas_call(
        _mod_kernel,
        grid=(depth, n // tn),
        in_specs=[pl.BlockSpec((rows, d), lambda l, j: (0, 0)),
                  pl.BlockSpec((1, d, tn), lambda l, j: (l, 0, j)),
                  pl.BlockSpec((1, 1, tn), lambda l, j: (l, 0, j))],
        out_specs=pl.BlockSpec((1, rows, tn), lambda l, j: (l, 0, j)),
        out_shape=jax.ShapeDtypeStruct((depth, rows, n), F32),
        compiler_params=_cparams(("arbitrary", "arbitrary")),
        name="modulation",
    )(cs, w_mod, b_mod.reshape(depth, 1, n))


def _group_rms(p, gmat_ref, g):
    ms = _dot((p * p).astype(BF16), gmat_ref[...])
    return p * lax.rsqrt(ms + NORM_EPS) * g


def _rope(y, cos, sin):
    w = y.shape[1]
    lane = lax.broadcasted_iota(jnp.int32, y.shape, 1)
    first = (lane % (2 * (ROPE_HALF // 2))) < (ROPE_HALF // 2)
    partner = jnp.where(first, pltpu.roll(y, w - ROPE_HALF // 2, axis=1), pltpu.roll(y, ROPE_HALF // 2, axis=1))
    return y * cos + partner * sin


def _pre_kernel(has_moe, is_sample, *refs):
    refs = list(refs)
    x_ref = refs.pop(0)
    y_ref = refs.pop(0) if has_moe else None
    mod_ref, g1_ref, win_ref, wgate_ref, bgate_ref, gmat_ref, gq_ref, gk_ref = refs[:8]
    refs = refs[8:]
    if is_sample:
        cos_ref, sin_ref = refs[:2]
        refs = refs[2:]
    xo_ref = refs.pop(0) if has_moe else None
    gla_ref, la_ref, diff_ref = refs[:3]
    refs = refs[3:]
    if not is_sample:
        kc_ref, vc_ref = refs

    d = D_MODEL
    m = mod_ref[0]
    x = x_ref[...]
    if has_moe:
        x = x + m[:, 5 * d:6 * d] * y_ref[...].astype(F32)
        xo_ref[...] = x
    ms = jnp.mean(x * x, axis=-1, keepdims=True)
    gs = g1_ref[0] * (1.0 + m[:, d:2 * d])
    hb = (x * lax.rsqrt(ms + NORM_EPS) * gs + m[:, 0:d]).astype(BF16)

    for j in range(4):
        p = _dot(hb, win_ref[0, :, j * 512:(j + 1) * 512])
        if j == 0:
            p = p * (GLA_DK ** -0.5)
        gla_ref[:, j * 512:(j + 1) * 512] = p.astype(BF16)

    a = _dot(hb, win_ref[0, :, MAIN_COLS:MAIN_COLS + GATE_COLS])
    z = _dot(a.astype(BF16), wgate_ref[0]) + bgate_ref[0]
    la_ref[...] = (jnp.minimum(z, 0.0) - jnp.log1p(jnp.exp(-jnp.abs(z)))) * (1.0 / GLA_GATE_NORMALIZER)

    scale = DIFF_HEAD_DIM ** -0.5
    q = _group_rms(_dot(hb, win_ref[0, :, 2048:2560]), gmat_ref, gq_ref[0])
    k = _group_rms(_dot(hb, win_ref[0, :, 2560:3072]), gmat_ref, gk_ref[0])
    v = _dot(hb, win_ref[0, :, 3072:3584])
    if is_sample:
        cos = cos_ref[...]
        sin = sin_ref[...]
        q = _rope(q, cos, sin)
        k = _rope(k, cos, sin)
    else:
        kc_ref[0, 0] = k
        vc_ref[0, 0] = v
    diff_ref[:, 0:512] = (q * scale).astype(BF16)
    diff_ref[:, 512:1024] = k.astype(BF16)
    diff_ref[:, 1024:1536] = v.astype(BF16)


def _pre_mixer(l, x, y, mod, prep, rope, nb, seq, is_sample):
    t, d = x.shape
    tt = TOKEN_TILE
    tiles_per_batch = seq // tt
    has_moe = y is not None
    if is_sample:
        mod_map = lambda i: (1 + i // tiles_per_batch, 0, 0)
    else:
        mod_map = lambda i: (0, 0, 0)
    row = lambda i: (i, 0)
    const2 = lambda i: (0, 0)
    lay3 = lambda i: (l, 0, 0)

    ins = [x]
    in_specs = [pl.BlockSpec((tt, d), row)]
    if has_moe:
        ins.append(y)
        in_specs.append(pl.BlockSpec((tt, d), row))
    ins += [mod, prep["g1"], prep["w_in"], prep["w_gate"], prep["b_gate"], prep["gmat"], prep["gq"], prep["gk"]]
    in_specs += [pl.BlockSpec((1, 1, 6 * d), mod_map),
                 pl.BlockSpec((1, 1, d), lay3),
                 pl.BlockSpec((1, d, MAIN_COLS + GATE_COLS), lay3),
                 pl.BlockSpec((1, GATE_COLS, 2 * GLA_WIDTH), lay3),
                 pl.BlockSpec((1, 1, 2 * GLA_WIDTH), lay3),
                 pl.BlockSpec((DIFF_WIDTH, DIFF_WIDTH), const2),
                 pl.BlockSpec((1, 1, DIFF_WIDTH), lay3),
                 pl.BlockSpec((1, 1, DIFF_WIDTH), lay3)]
    if is_sample:
        ins += [rope[0], rope[1]]
        in_specs += [pl.BlockSpec((tt, DIFF_WIDTH), lambda i: (i % tiles_per_batch, 0))] * 2

    out_shape = []
    out_specs = []
    if has_moe:
        out_shape.append(jax.ShapeDtypeStruct((t, d), F32))
        out_specs.append(pl.BlockSpec((tt, d), row))
    out_shape += [jax.ShapeDtypeStruct((t, 4 * GLA_WIDTH), BF16),
                  jax.ShapeDtypeStruct((t, 2 * GLA_WIDTH), F32),
                  jax.ShapeDtypeStruct((t, 3 * DIFF_WIDTH), BF16)]
    out_specs += [pl.BlockSpec((tt, 4 * GLA_WIDTH), row),
                  pl.BlockSpec((tt, 2 * GLA_WIDTH), row),
                  pl.BlockSpec((tt, 3 * DIFF_WIDTH), row)]
    if not is_sample:
        assert seq == tt
        out_shape += [jax.ShapeDtypeStruct((nb, 1, seq, DIFF_WIDTH), F32)] * 2
        out_specs += [pl.BlockSpec((1, 1, seq, DIFF_WIDTH), lambda i: (i, 0, 0, 0))] * 2

    outs = pl.pallas_call(
        functools.partial(_pre_kernel, has_moe, is_sample),
        grid=(t // tt,),
        in_specs=in_specs,
        out_specs=out_specs,
        out_shape=out_shape,
        compiler_params=_cparams(("arbitrary",)),
        name="pre_mixer",
    )(*ins)
    outs = list(outs)
    x_new = outs.pop(0) if has_moe else x
    return [x_new] + outs


def _gla_chunk(q, k, v, la, st_ref, tri, mask, mid_row, last_row):
    c = GLA_CHUNK
    la_hi, la_lo = _split_bf16(la)
    b = _dot(tri, la_hi) + _dot(tri, la_lo)
    mid = b[mid_row:mid_row + 1]
    last = b[last_row:last_row + 1]
    qf = q.astype(F32)
    kf = k.astype(F32)
    qe = (qf * jnp.exp(b - mid)).astype(BF16)
    ke = (kf * jnp.exp(mid - b)).astype(BF16)
    att = jnp.where(mask, _dot_nt(qe, ke), 0.0)
    o = _dot(att.astype(BF16), v)
    qi = (qf * jnp.exp(b)).astype(BF16)
    ks = (kf * jnp.exp(last - b)).astype(BF16)
    st = st_ref[...]
    o = o + _dot_nt(qi, st.astype(BF16))
    st_ref[...] = st * jnp.exp(last) + _dot_tn(v, ks)
    return o


def _gla_kernel(has_state, n_chunks, *refs):
    refs = list(refs)
    q_ref, k_ref, v_ref, g_ref, laf_ref, lab_ref, gg_ref = refs[:7]
    refs = refs[7:]
    if has_state:
        s0f_ref, s0b_ref = refs[:2]
        refs = refs[2:]
    o_ref, sf_ref, sb_ref, stf_ref, stb_ref, of_ref, ob_ref = refs

    c = GLA_CHUNK
    if has_state:
        stf_ref[...] = s0f_ref[0, 0, 0].T
        stb_ref[...] = s0b_ref[0, 0, 0].T
    else:
        stf_ref[...] = jnp.zeros_like(stf_ref)
        stb_ref[...] = jnp.zeros_like(stb_ref)

    r = lax.broadcasted_iota(jnp.int32, (c, c), 0)
    s = lax.broadcasted_iota(jnp.int32, (c, c), 1)
    lower = r >= s
    upper = r <= s
    tril = jnp.where(lower, 1.0, 0.0).astype(BF16)
    triu = jnp.where(upper, 1.0, 0.0).astype(BF16)

    def body(n, carry):
        rf = pl.ds(pl.multiple_of(n * c, c), c)
        rb = pl.ds(pl.multiple_of((n_chunks - 1 - n) * c, c), c)
        of_ref[rf, :] = _gla_chunk(q_ref[rf, :], k_ref[rf, :], v_ref[rf, :], laf_ref[rf, :],
                                   stf_ref, tril, lower, c // 2, c - 1)
        ob_ref[rb, :] = _gla_chunk(q_ref[rb, :], k_ref[rb, :], v_ref[rb, :], lab_ref[rb, :],
                                   stb_ref, triu, upper, c - 1 - c // 2, 0)
        return carry

    lax.fori_loop(0, n_chunks, body, 0)

    sf_ref[0, 0] = stf_ref[...].T
    sb_ref[0, 0] = stb_ref[...].T

    blk = min(512, n_chunks * c)

    def fin(i, carry):
        rr = pl.ds(pl.multiple_of(i * blk, blk), blk)
        o = of_ref[rr, :] + ob_ref[rr, :]
        ms = jnp.mean(o * o, axis=-1, keepdims=True)
        g = g_ref[rr, :].astype(F32)
        o_ref[rr, :] = (o * lax.rsqrt(ms + NORM_EPS) * gg_ref[0] * (g * _sigmoid(g))).astype(BF16)
        return carry

    lax.fori_loop(0, (n_chunks * c) // blk, fin, 0)


def _gla(l, gla_in, la, gg, s0f, s0b, nb, seq):
    t = gla_in.shape[0]
    h = GLA_HEADS
    has_state = s0f is not None
    n_chunks = seq // GLA_CHUNK
    col = lambda off: (lambda b, hh: (b, off + hh))
    ins = [gla_in, gla_in, gla_in, gla_in, la, la, gg]
    in_specs = [pl.BlockSpec((seq, GLA_DK), col(0)),
                pl.BlockSpec((seq, GLA_DK), col(h)),
                pl.BlockSpec((seq, GLA_DV), col(2 * h)),
                pl.BlockSpec((seq, GLA_DV), col(3 * h)),
                pl.BlockSpec((seq, GLA_DK), col(0)),
                pl.BlockSpec((seq, GLA_DK), col(h)),
                pl.BlockSpec((1, 1, GLA_DV), lambda b, hh: (l, 0, 0))]
    if has_state:
        ins += [s0f, s0b]
        in_specs += [pl.BlockSpec((1, 1, 1, GLA_DK, GLA_DV), lambda b, hh: (b, l, hh, 0, 0))] * 2
    return pl.pallas_call(
        functools.partial(_gla_kernel, has_state, n_chunks),
        grid=(nb, h),
        in_specs=in_specs,
        out_specs=[pl.BlockSpec((seq, GLA_DV), col(0)),
                   pl.BlockSpec((1, 1, GLA_DK, GLA_DV), lambda b, hh: (b, hh, 0, 0)),
                   pl.BlockSpec((1, 1, GLA_DK, GLA_DV), lambda b, hh: (b, hh, 0, 0))],
        out_shape=[jax.ShapeDtypeStruct((t, GLA_WIDTH), BF16),
                   jax.ShapeDtypeStruct((nb, h, GLA_DK, GLA_DV), F32),
                   jax.ShapeDtypeStruct((nb, h, GLA_DK, GLA_DV), F32)],
        scratch_shapes=[pltpu.VMEM((GLA_DV, GLA_DK), F32), pltpu.VMEM((GLA_DV, GLA_DK), F32),
                        pltpu.VMEM((seq, GLA_DV), F32), pltpu.VMEM((seq, GLA_DV), F32)],
        compiler_params=_cparams(("arbitrary", "arbitrary")),
        name="gla",
    )(*ins)


def _attn_kernel(lam_init, has_ctx, n_kt, tk, *refs):
    refs = list(refs)
    q_ref, k_ref, v_ref = refs[:3]
    refs = refs[3:]
    if has_ctx:
        ck_ref, cv_ref = refs[:2]
        refs = refs[2:]
    lq1_ref, lk1_ref, lq2_ref, lk2_ref, gd_ref, o_ref = refs

    q = q_ref[...]
    tq = q.shape[0]
    lane = lax.broadcasted_iota(jnp.int32, q.shape, 1)
    zero = jnp.zeros_like(q)
    qs = (jnp.where(lane < DIFF_HEAD_DIM, q, zero), jnp.where(lane >= DIFF_HEAD_DIM, q, zero))

    def update(state, k, v):
        new = []
        for c in range(2):
            m, lsum, acc = state[c]
            s = _dot_nt(qs[c], k)
            m_new = jnp.maximum(m, jnp.max(s, axis=1, keepdims=True))
            alpha = jnp.exp(m - m_new)
            p = jnp.exp(s - m_new)
            lsum = alpha * lsum + jnp.sum(p, axis=1, keepdims=True)
            acc = alpha * acc + _dot(p.astype(BF16), v)
            new.append((m_new, lsum, acc))
        return tuple(new)

    init = tuple((jnp.full((tq, 1), -jnp.inf, F32), jnp.zeros((tq, 1), F32), jnp.zeros((tq, DIFF_V_DIM), F32))
                 for _ in range(2))
    state = init
    if has_ctx:
        state = update(state, ck_ref[0, 0].astype(BF16), cv_ref[0, 0].astype(BF16))

    def body(j, st):
        rr = pl.ds(pl.multiple_of(j * tk, tk), tk)
        return update(st, k_ref[rr, :], v_ref[rr, :])

    state = lax.fori_loop(0, n_kt, body, state)

    lam = (jnp.exp(jnp.sum(lq1_ref[0] * lk1_ref[0], axis=-1, keepdims=True))
           - jnp.exp(jnp.sum(lq2_ref[0] * lk2_ref[0], axis=-1, keepdims=True)) + lam_init)
    (m0, l0, a0), (m1, l1, a1) = state
    o = a0 / l0 - lam * (a1 / l1)
    ms = jnp.mean(o * o, axis=-1, keepdims=True)
    o_ref[...] = (o * lax.rsqrt(ms + NORM_EPS) * gd_ref[0] * (1.0 - lam_init)).astype(BF16)


def _diff_attention(l, lam_init, diff_in, ctx_k, ctx_v, lams, gd, nb, seq):
    t = diff_in.shape[0]
    h = DIFF_HEADS
    has_ctx = ctx_k is not None
    tq = min(ATTN_TQ, seq)
    tk = min(ATTN_TK, seq)
    nq = seq // tq
    w = 2 * DIFF_HEAD_DIM
    ins = [diff_in, diff_in, diff_in]
    in_specs = [pl.BlockSpec((tq, w), lambda b, hh, i: (b * nq + i, hh)),
                pl.BlockSpec((seq, w), lambda b, hh, i: (b, h + hh)),
                pl.BlockSpec((seq, DIFF_V_DIM), lambda b, hh, i: (b, 2 * h + hh))]
    if has_ctx:
        past = ctx_k.shape[2]
        ins += [ctx_k, ctx_v]
        in_specs += [pl.BlockSpec((1, 1, past, w), lambda b, hh, i: (b, l, 0, hh)),
                     pl.BlockSpec((1, 1, past, DIFF_V_DIM), lambda b, hh, i: (b, l, 0, hh))]
    lay3 = lambda b, hh, i: (l, 0, 0)
    ins += list(lams) + [gd]
    in_specs += [pl.BlockSpec((1, 1, DIFF_HEAD_DIM), lay3)] * 4 + [pl.BlockSpec((1, 1, DIFF_V_DIM), lay3)]
    return pl.pallas_call(
        functools.partial(_attn_kernel, lam_init, has_ctx, seq // tk, tk),
        grid=(nb, h, nq),
        in_specs=in_specs,
        out_specs=pl.BlockSpec((tq, DIFF_V_DIM), lambda b, hh, i: (b * nq + i, hh)),
        out_shape=jax.ShapeDtypeStruct((t, DIFF_WIDTH), BF16),
        compiler_params=_cparams(("arbitrary", "arbitrary", "arbitrary")),
        name="diff_attention",
    )(*ins)


def _top2_of4(v):
    m1 = jnp.maximum(jnp.maximum(v[0], v[1]), jnp.maximum(v[2], v[3]))
    i1 = jnp.where(v[0] == m1, 0, jnp.where(v[1] == m1, 1, jnp.where(v[2] == m1, 2, 3)))
    neg = jnp.full_like(m1, -jnp.inf)
    w = [jnp.where(i1 == j, neg, v[j]) for j in range(4)]
    m2 = jnp.maximum(jnp.maximum(w[0], w[1]), jnp.maximum(w[2], w[3]))
    i2 = jnp.where(w[0] == m2, 0, jnp.where(w[1] == m2, 1, jnp.where(w[2] == m2, 2, 3)))
    return m1, i1, m2, i2


def _post_kernel(og_ref, od_ref, x_ref, mod_ref, g2_ref, wo_ref, wr_ref, br_ref, x1_ref, h2_ref, route_ref):
    d = D_MODEL
    m = mod_ref[0]
    out = _dot(og_ref[...], wo_ref[0, 0:GLA_WIDTH, :]) + _dot(od_ref[...], wo_ref[0, GLA_WIDTH:, :])
    x1 = x_ref[...] + m[:, 2 * d:3 * d] * out
    x1_ref[...] = x1
    ms = jnp.mean(x1 * x1, axis=-1, keepdims=True)
    h2 = x1 * lax.rsqrt(ms + NORM_EPS) * (g2_ref[0] * (1.0 + m[:, 4 * d:5 * d])) + m[:, 3 * d:4 * d]
    h_hi, h_lo = _split_bf16(h2)
    h2_ref[...] = h_hi
    w_hi, w_lo = _split_bf16(wr_ref[...])
    z = _dot_nt(w_hi, h_hi) + _dot_nt(w_lo, h_hi) + _dot_nt(w_hi, h_lo)
    s = _sigmoid(z)
    sel = s + br_ref[...]

    e = EXPERTS_PER_GROUP
    tops = []
    for g in range(N_GROUPS):
        tops.append(_top2_of4([sel[g * e + j:g * e + j + 1, :] for j in range(e)]))
    score = [t[0] + t[2] for t in tops]
    best = jnp.maximum(jnp.maximum(score[0], score[1]), jnp.maximum(score[2], score[3]))
    gi = jnp.where(score[0] == best, 0, jnp.where(score[1] == best, 1, jnp.where(score[2] == best, 2, 3)))

    def pick(rows):
        return jnp.where(gi == 0, rows[0], jnp.where(gi == 1, rows[1], jnp.where(gi == 2, rows[2], rows[3])))

    i1 = pick([t[1] for t in tops])
    i2 = pick([t[3] for t in tops])

    def gate_of(idx):
        per_group = []
        for g in range(N_GROUPS):
            rows = [s[g * e + j:g * e + j + 1, :] for j in range(e)]
            per_group.append(jnp.where(idx == 0, rows[0], jnp.where(idx == 1, rows[1],
                                                                     jnp.where(idx == 2, rows[2], rows[3]))))
        return pick(per_group)

    s1 = gate_of(i1)
    s2 = gate_of(i2)
    tot = s1 + s2
    w1 = s1 / tot
    w2 = s2 / tot
    lo = jnp.minimum(i1, i2)
    hi = jnp.maximum(i1, i2)
    w_lo = jnp.where(i1 < i2, w1, w2)
    w_hi = jnp.where(i1 < i2, w2, w1)
    pair = jnp.where(lo == 0, hi - 1, jnp.where(lo == 1, jnp.where(hi == 3, 3, 4), 5))
    swapped = pair == 5
    route_ref[...] = jnp.zeros_like(route_ref)
    route_ref[0:1, :] = (gi * N_PAIRS + pair).astype(F32)
    route_ref[1:2, :] = jnp.where(swapped, w_hi, w_lo)
    route_ref[2:3, :] = jnp.where(swapped, w_lo, w_hi)


def _post_mixer(l, og, od, x, mod, prep, seq, is_sample):
    t, d = x.shape
    tt = TOKEN_TILE
    tiles_per_batch = seq // tt
    if is_sample:
        mod_map = lambda i: (1 + i // tiles_per_batch, 0, 0)
    else:
        mod_map = lambda i: (0, 0, 0)
    row = lambda i: (i, 0)
    lay3 = lambda i: (l, 0, 0)
    return pl.pallas_call(
        _post_kernel,
        grid=(t // tt,),
        in_specs=[pl.BlockSpec((tt, GLA_WIDTH), row),
                  pl.BlockSpec((tt, DIFF_WIDTH), row),
                  pl.BlockSpec((tt, d), row),
                  pl.BlockSpec((1, 1, 6 * d), mod_map),
                  pl.BlockSpec((1, 1, d), lay3),
                  pl.BlockSpec((1, d, d), lay3),
                  pl.BlockSpec((N_EXPERTS, d), lambda i: (0, 0)),
                  pl.BlockSpec((N_EXPERTS, 1), lambda i: (0, 0))],
        out_specs=[pl.BlockSpec((tt, d), row),
                   pl.BlockSpec((tt, d), row),
                   pl.BlockSpec((8, tt), lambda i: (0, i))],
        out_shape=[jax.ShapeDtypeStruct((t, d), F32),
                   jax.ShapeDtypeStruct((t, d), BF16),
                   jax.ShapeDtypeStruct((8, t), F32)],
        compiler_params=_cparams(("arbitrary",)),
        name="post_mixer",
    )(og, od, x, mod, prep["g2"], prep["w_out"], prep["w_router_t"], prep["b_router"])


def _moe_kernel(ea_ref, eb_ref, valid_ref, x_ref, gate_ref, wga_ref, wua_ref, wda_ref, wgb_ref, wub_ref, wdb_ref,
                y_ref):
    i = pl.program_id(0)

    @pl.when(valid_ref[i] > 0)
    def _():
        x = x_ref[...]
        gate = gate_ref[...]

        def ffn(wg_ref, wu_ref, wd_ref):
            hg = _dot(x, wg_ref[0, 0])
            hu = _dot(x, wu_ref[0, 0])
            a = (hg * _sigmoid(hg)) * hu
            return _dot(a.astype(BF16), wd_ref[0, 0])

        y = gate[:, 0:1] * ffn(wga_ref, wua_ref, wda_ref) + gate[:, 1:2] * ffn(wgb_ref, wub_ref, wdb_ref)
        y_ref[...] = y.astype(BF16)

    @pl.when(valid_ref[i] == 0)
    def _():
        y_ref[...] = jnp.zeros_like(y_ref)


def _moe(l, h2, route, prep):
    t, d = h2.shape
    tm = MOE_TILE
    p_rows = t + N_COMBOS * tm
    n_tiles = p_rows // tm

    combo = route[0].astype(jnp.int32)
    order = jnp.argsort(combo).astype(jnp.int32)
    sorted_combo = combo[order]
    counts = jnp.sum(combo[:, None] == jnp.arange(N_COMBOS, dtype=jnp.int32)[None, :], axis=0).astype(jnp.int32)
    padded = ((counts + tm - 1) // tm) * tm
    pad_end = jnp.cumsum(padded)
    pad_start = pad_end - padded
    start = jnp.cumsum(counts) - counts
    dest = pad_start[sorted_combo] + (jnp.arange(t, dtype=jnp.int32) - start[sorted_combo])
    row_ids = jnp.zeros((p_rows,), jnp.int32).at[dest].set(order)
    pos = jnp.zeros((t,), jnp.int32).at[order].set(dest)
    gates = jnp.zeros((p_rows, 2), F32).at[dest].set(route[1:3].T[order])
    tile_start = jnp.arange(n_tiles, dtype=jnp.int32) * tm
    valid = (tile_start < pad_end[-1]).astype(jnp.int32)
    last_valid_start = jnp.maximum(pad_end[-1] - tm, 0)
    tile_combo = jnp.searchsorted(pad_end, jnp.minimum(tile_start, last_valid_start), side="right").astype(jnp.int32)
    tile_combo = jnp.minimum(tile_combo, N_COMBOS - 1)
    grp = tile_combo // N_PAIRS
    pr = tile_combo % N_PAIRS
    ea = grp * EXPERTS_PER_GROUP + jnp.asarray(PAIR_A, jnp.int32)[pr]
    eb = grp * EXPERTS_PER_GROUP + jnp.asarray(PAIR_B, jnp.int32)[pr]

    xs = jnp.take(h2, row_ids, axis=0)

    wa = lambda i, ea_r, eb_r, v_r: (l, ea_r[i], 0, 0)
    wb = lambda i, ea_r, eb_r, v_r: (l, eb_r[i], 0, 0)
    row = lambda i, ea_r, eb_r, v_r: (i, 0)
    up = pl.BlockSpec((1, 1, d, D_EXPERT), wa)
    dn = pl.BlockSpec((1, 1, D_EXPERT, d), wa)
    upb = pl.BlockSpec((1, 1, d, D_EXPERT), wb)
    dnb = pl.BlockSpec((1, 1, D_EXPERT, d), wb)
    ys = pl.pallas_call(
        _moe_kernel,
        grid_spec=pltpu.PrefetchScalarGridSpec(
            num_scalar_prefetch=3,
            grid=(n_tiles,),
            in_specs=[pl.BlockSpec((tm, d), row), pl.BlockSpec((tm, 2), row), up, up, dn, upb, upb, dnb],
            out_specs=pl.BlockSpec((tm, d), row)),
        out_shape=jax.ShapeDtypeStruct((p_rows, d), BF16),
        compiler_params=_cparams(("arbitrary",)),
        name="moe",
    )(ea, eb, valid, xs, gates, prep["w_eg"], prep["w_eu"], prep["w_ed"], prep["w_eg"], prep["w_eu"], prep["w_ed"])
    return jnp.take(ys, pos, axis=0)


def _resid_kernel(x_ref, y_ref, mod_ref, o_ref):
    d = D_MODEL
    o_ref[...] = x_ref[...] + mod_ref[0][:, 5 * d:6 * d] * y_ref[...].astype(F32)


def _residual(x, y, mod, seq, is_sample):
    t, d = x.shape
    tt = TOKEN_TILE
    tiles_per_batch = seq // tt
    if is_sample:
        mod_map = lambda i: (1 + i // tiles_per_batch, 0, 0)
    else:
        mod_map = lambda i: (0, 0, 0)
    row = lambda i: (i, 0)
    return pl.pallas_call(
        _resid_kernel,
        grid=(t // tt,),
        in_specs=[pl.BlockSpec((tt, d), row), pl.BlockSpec((tt, d), row), pl.BlockSpec((1, 1, 6 * d), mod_map)],
        out_specs=pl.BlockSpec((tt, d), row),
        out_shape=jax.ShapeDtypeStruct((t, d), F32),
        compiler_params=_cparams(("arbitrary",)),
        name="residual",
    )(x, y, mod)


def _rope_tables(length):
    rows = length // GRID_W
    row = jnp.repeat(jnp.arange(rows, dtype=F32), GRID_W)
    col = jnp.tile(jnp.arange(GRID_W, dtype=F32), rows)
    inv = ROPE_BASE ** (-jnp.arange(0, ROPE_HALF, 2, dtype=F32) / ROPE_HALF)
    ang_r = row[:, None] * inv[None, :]
    ang_c = col[:, None] * inv[None, :]
    cos = jnp.concatenate([jnp.cos(ang_r), jnp.cos(ang_r), jnp.cos(ang_c), jnp.cos(ang_c)], axis=-1)
    sin = jnp.concatenate([-jnp.sin(ang_r), jnp.sin(ang_r), -jnp.sin(ang_c), jnp.sin(ang_c)], axis=-1)
    reps = DIFF_WIDTH // DIFF_HEAD_DIM
    return jnp.tile(cos, (1, reps)), jnp.tile(sin, (1, reps))


def _prepare(g_norm1, g_norm2, w_in, w_gate_fwd, b_gate_fwd, w_gate_bwd, b_gate_bwd, g_q_norm, g_k_norm,
             w_out, w_router, b_router, w_exp_gate, w_exp_up, w_exp_down):
    depth, d, _ = w_in.shape
    r = GLA_GATE_RANK
    gate_cols = jnp.pad(w_in[..., 2048:2048 + 2 * r], ((0, 0), (0, 0), (0, GATE_COLS - 2 * r)))
    w_in_r = jnp.concatenate([w_in[..., :2048], w_in[..., 2048 + 2 * r:], gate_cols], axis=-1).astype(BF16)
    w_gate = jnp.zeros((depth, GATE_COLS, 2 * GLA_WIDTH), F32)
    w_gate = w_gate.at[:, 0:r, 0:GLA_WIDTH].set(w_gate_fwd).at[:, r:2 * r, GLA_WIDTH:].set(w_gate_bwd).astype(BF16)
    b_gate = jnp.concatenate([b_gate_fwd, b_gate_bwd], axis=-1).reshape(depth, 1, 2 * GLA_WIDTH)
    grp = jnp.arange(DIFF_WIDTH) // DIFF_HEAD_DIM
    gmat = jnp.where(grp[:, None] == grp[None, :], 1.0 / DIFF_HEAD_DIM, 0.0).astype(BF16)
    reps = DIFF_WIDTH // DIFF_HEAD_DIM
    return {
        "g1": g_norm1.reshape(depth, 1, d),
        "g2": g_norm2.reshape(depth, 1, d),
        "w_in": w_in_r,
        "w_gate": w_gate,
        "b_gate": b_gate,
        "gmat": gmat,
        "gq": jnp.tile(g_q_norm, (1, reps)).reshape(depth, 1, DIFF_WIDTH),
        "gk": jnp.tile(g_k_norm, (1, reps)).reshape(depth, 1, DIFF_WIDTH),
        "w_out": w_out.astype(BF16),
        "w_router_t": w_router.T,
        "b_router": b_router.reshape(N_EXPERTS, 1),
        "w_eg": w_exp_gate.astype(BF16),
        "w_eu": w_exp_up.astype(BF16),
        "w_ed": w_exp_down.astype(BF16),
    }


def kernel(x_prompt, x_sample, cache_k, cache_v, state_gla_fwd, state_gla_bwd, c, c_ctx, w_mod, b_mod, g_norm1,
           g_norm2, w_in, w_gate_fwd, b_gate_fwd, w_gate_bwd, b_gate_bwd, g_gla_out, g_q_norm, g_k_norm, lambda_q1,
           lambda_k1, lambda_q2, lambda_k2, g_diff_out, w_out, w_router, b_router, w_exp_gate, w_exp_up, w_exp_down):
    nb_p, seq_p, d = x_prompt.shape
    nb_s, seq_s, _ = x_sample.shape
    depth = w_in.shape[0]
    past = cache_k.shape[2]

    prep = _prepare(g_norm1, g_norm2, w_in, w_gate_fwd, b_gate_fwd, w_gate_bwd, b_gate_bwd, g_q_norm, g_k_norm,
                    w_out, w_router, b_router, w_exp_gate, w_exp_up, w_exp_down)
    rope = _rope_tables(seq_s)

    mod_rows = 8 * ((1 + nb_s + 7) // 8)
    cs = jnp.zeros((mod_rows, d), F32).at[0].set(c_ctx).at[1:1 + nb_s].set(c)
    mod_all = _modulation(cs, w_mod, b_mod)
    ctx_k = cache_k.reshape(nb_s, depth, past, DIFF_WIDTH)
    ctx_v = cache_v.reshape(nb_s, depth, past, DIFF_WIDTH)
    gg = g_gla_out.reshape(depth, 1, GLA_DV)
    gd = g_diff_out.reshape(depth, 1, DIFF_V_DIM)
    lams = [a.reshape(depth, 1, DIFF_HEAD_DIM) for a in (lambda_q1, lambda_k1, lambda_q2, lambda_k2)]

    xp = x_prompt.reshape(nb_p * seq_p, d)
    xs = x_sample.reshape(nb_s * seq_s, d)
    yp = ys = None
    mod_prev = None
    new_k, new_v, new_sf, new_sb = [], [], [], []
    for l in range(depth):
        lam_init = 0.8 - 0.6 * math.exp(-0.3 * l)
        mod = mod_all[l].reshape(mod_rows, 1, 6 * d)
        if l > 0:
            xp = _residual(xp, yp, mod_prev, seq_p, False)
            xs = _residual(xs, ys, mod_prev, seq_s, True)
        xp, gla_in, la, diff_in, k_l, v_l = _pre_mixer(l, xp, None, mod, prep, None, nb_p, seq_p, False)
        og, sf_l, sb_l = _gla(l, gla_in, la, gg, None, None, nb_p, seq_p)
        od = _diff_attention(l, lam_init, diff_in, None, None, lams, gd, nb_p, seq_p)
        xp, h2, route = _post_mixer(l, og, od, xp, mod, prep, seq_p, False)
        yp = _moe(l, h2, route, prep)
        new_k.append(k_l)
        new_v.append(v_l)
        new_sf.append(sf_l)
        new_sb.append(sb_l)
        xs, gla_in, la, diff_in = _pre_mixer(l, xs, None, mod, prep, rope, nb_s, seq_s, True)
        og, _, _ = _gla(l, gla_in, la, gg, state_gla_fwd, state_gla_bwd, nb_s, seq_s)
        od = _diff_attention(l, lam_init, diff_in, ctx_k, ctx_v, lams, gd, nb_s, seq_s)
        xs, h2, route = _post_mixer(l, og, od, xs, mod, prep, seq_s, True)
        ys = _moe(l, h2, route, prep)
        mod_prev = mod
    xp = _residual(xp, yp, mod_prev, seq_p, False)
    xs = _residual(xs, ys, mod_prev, seq_s, True)

    new_cache_k = jnp.concatenate(new_k, axis=1).reshape(nb_p, depth, seq_p, DIFF_HEADS, 2, DIFF_HEAD_DIM)
    new_cache_v = jnp.concatenate(new_v, axis=1).reshape(nb_p, depth, seq_p, DIFF_HEADS, DIFF_V_DIM)
    new_sf = jnp.stack(new_sf, axis=1)
    new_sb = jnp.stack(new_sb, axis=1)
    return (xp.reshape(nb_p, seq_p, d), xs.reshape(nb_s, seq_s, d), new_cache_k, new_cache_v, new_sf, new_sb)
```

```python
import functools
import math

import jax
import jax.numpy as jnp
from jax import lax
from jax.experimental import pallas as pl
from jax.experimental.pallas import tpu as pltpu

F32 = jnp.float32
BF16 = jnp.bfloat16

D_MODEL = 1024
GLA_HEADS = 4
GLA_DK = 128
GLA_DV = 128
GLA_WIDTH = GLA_HEADS * GLA_DK
GLA_GATE_RANK = 16
GLA_GATE_NORMALIZER = 16.0
GLA_CHUNK = 64
DIFF_HEADS = 4
DIFF_HEAD_DIM = 64
DIFF_V_DIM = 128
DIFF_WIDTH = DIFF_HEADS * 2 * DIFF_HEAD_DIM
ROPE_HALF = DIFF_HEAD_DIM // 2
ROPE_BASE = 10000.0
GRID_W = 64
N_EXPERTS = 16
N_GROUPS = 4
EXPERTS_PER_GROUP = 4
D_EXPERT = 512
NORM_EPS = 1e-6
LOG2E = 1.4426950408889634

PAIR_A = (0, 0, 0, 1, 1, 3)
PAIR_B = (1, 2, 3, 3, 2, 2)
N_PAIRS = len(PAIR_A)
N_COMBOS = N_GROUPS * N_PAIRS

GATE_COLS = 128
MAIN_COLS = 7 * 512
TOKEN_TILE = 256
MOE_TILE = 256
ATTN_TQ = 512
ATTN_TK = 256
ATTN_GROUP = 128
ATTN_UNROLL = 4
ONES_ROWS = 16
VMEM_LIMIT = 56 * 1024 * 1024


def _cparams(sem):
    return pltpu.CompilerParams(dimension_semantics=sem, vmem_limit_bytes=VMEM_LIMIT)


def _dot(a, b):
    return jnp.dot(a, b, preferred_element_type=F32)


def _dot_nt(a, b):
    return lax.dot_general(a, b, (((1,), (1,)), ((), ())), preferred_element_type=F32)


def _dot_tn(a, b):
    return lax.dot_general(a, b, (((0,), (0,)), ((), ())), preferred_element_type=F32)


def _sigmoid(x):
    return 1.0 / (1.0 + jnp.exp(-x))


def _split_bf16(x):
    hi = x.astype(BF16)
    lo = (x - hi.astype(F32)).astype(BF16)
    return hi, lo


def _mod_kernel(c_ref, w_ref, b_ref, o_ref):
    c = c_ref[...]
    s = c * _sigmoid(c)
    o_ref[0] = jnp.dot(s, w_ref[0], preferred_element_type=F32,
                       precision=lax.Precision.HIGHEST) + b_ref[0]


def _modulation(cs, w_mod, b_mod):
    depth, d, n = w_mod.shape
    rows = cs.shape[0]
    tn = 1536
    return pl.pallas_call(
        _mod_kernel,
        grid=(depth, n // tn),
        in_specs=[pl.BlockSpec((rows, d), lambda l, j: (0, 0)),
                  pl.BlockSpec((1, d, tn), lambda l, j: (l, 0, j)),
                  pl.BlockSpec((1, 1, tn), lambda l, j: (l, 0, j))],
        out_specs=pl.BlockSpec((1, rows, tn), lambda l, j: (l, 0, j)),
        out_shape=jax.ShapeDtypeStruct((depth, rows, n), F32),
        compiler_params=_cparams(("arbitrary", "arbitrary")),
        name="modulation",
    )(cs, w_mod, b_mod.reshape(depth, 1, n))


def _group_rms(p, gmat_ref, g):
    ms = _dot((p * p).astype(BF16), gmat_ref[...])
    return p * lax.rsqrt(ms + NORM_EPS) * g


def _rope(y, cos, sin):
    w = y.shape[1]
    lane = lax.broadcasted_iota(jnp.int32, y.shape, 1)
    first = (lane % (2 * (ROPE_HALF // 2))) < (ROPE_HALF // 2)
    partner = jnp.where(first, pltpu.roll(y, w - ROPE_HALF // 2, axis=1), pltpu.roll(y, ROPE_HALF // 2, axis=1))
    return y * cos + partner * sin


def _pre_kernel(has_moe, is_sample, *refs):
    refs = list(refs)
    x_ref = refs.pop(0)
    if has_moe:
        y_ref, modp_ref = refs[:2]
        refs = refs[2:]
    mod_ref, g1_ref, win_ref, wgate_ref, bgate_ref, gmat_ref, gq_ref, gk_ref = refs[:8]
    refs = refs[8:]
    if is_sample:
        cos_ref, sin_ref = refs[:2]
        refs = refs[2:]
    xo_ref = refs.pop(0) if has_moe else None
    gla_ref, la_ref, qt_ref, kd_ref, vt_ref = refs[:5]
    refs = refs[5:]
    if not is_sample:
        kc_ref, vc_ref = refs

    d = D_MODEL
    m = mod_ref[0]
    x = x_ref[...]
    if has_moe:
        x = x + modp_ref[0][:, 5 * d:6 * d] * y_ref[...].astype(F32)
        xo_ref[...] = x
    ms = jnp.mean(x * x, axis=-1, keepdims=True)
    gs = g1_ref[0] * (1.0 + m[:, d:2 * d])
    hb = (x * lax.rsqrt(ms + NORM_EPS) * gs + m[:, 0:d]).astype(BF16)

    for j in range(4):
        p = _dot(hb, win_ref[0, :, j * 512:(j + 1) * 512])
        if j == 0:
            p = p * (GLA_DK ** -0.5)
        gla_ref[:, j * 512:(j + 1) * 512] = p.astype(BF16)

    a = _dot(hb, win_ref[0, :, MAIN_COLS:MAIN_COLS + GATE_COLS])
    z = _dot(a.astype(BF16), wgate_ref[0]) + bgate_ref[0]
    la_ref[...] = (jnp.minimum(z, 0.0) - jnp.log1p(jnp.exp(-jnp.abs(z)))) * (1.0 / GLA_GATE_NORMALIZER)

    scale = DIFF_HEAD_DIM ** -0.5 * LOG2E
    q = _group_rms(_dot(hb, win_ref[0, :, 2048:2560]), gmat_ref, gq_ref[0])
    k = _group_rms(_dot(hb, win_ref[0, :, 2560:3072]), gmat_ref, gk_ref[0])
    v = _dot(hb, win_ref[0, :, 3072:3584])
    if is_sample:
        cos = cos_ref[...]
        sin = sin_ref[...]
        q = _rope(q, cos, sin)
        k = _rope(k, cos, sin)
    else:
        kc_ref[0, 0] = k
        vc_ref[0, 0] = v
    qt_ref[0] = (q * scale).astype(BF16).T
    kd_ref[...] = k.astype(BF16)
    vt_ref[0] = v.astype(BF16).T


def _pre_mixer(l, x, y, mod_prev, mod, prep, rope, nb, seq, is_sample):
    t, d = x.shape
    tt = TOKEN_TILE
    tiles_per_batch = seq // tt
    has_moe = y is not None
    if is_sample:
        mod_map = lambda i: (1 + i // tiles_per_batch, 0, 0)
    else:
        mod_map = lambda i: (0, 0, 0)
    row = lambda i: (i, 0)
    const2 = lambda i: (0, 0)
    lay3 = lambda i: (l, 0, 0)

    ins = [x]
    in_specs = [pl.BlockSpec((tt, d), row)]
    if has_moe:
        ins += [y, mod_prev]
        in_specs += [pl.BlockSpec((tt, d), row), pl.BlockSpec((1, 1, 6 * d), mod_map)]
    ins += [mod, prep["g1"], prep["w_in"], prep["w_gate"], prep["b_gate"], prep["gmat"], prep["gq"], prep["gk"]]
    in_specs += [pl.BlockSpec((1, 1, 6 * d), mod_map),
                 pl.BlockSpec((1, 1, d), lay3),
                 pl.BlockSpec((1, d, MAIN_COLS + GATE_COLS), lay3),
                 pl.BlockSpec((1, GATE_COLS, 2 * GLA_WIDTH), lay3),
                 pl.BlockSpec((1, 1, 2 * GLA_WIDTH), lay3),
                 pl.BlockSpec((DIFF_WIDTH, DIFF_WIDTH), const2),
                 pl.BlockSpec((1, 1, DIFF_WIDTH), lay3),
                 pl.BlockSpec((1, 1, DIFF_WIDTH), lay3)]
    if is_sample:
        ins += [rope[0], rope[1]]
        in_specs += [pl.BlockSpec((tt, DIFF_WIDTH), lambda i: (i % tiles_per_batch, 0))] * 2

    out_shape = []
    out_specs = []
    if has_moe:
        out_shape.append(jax.ShapeDtypeStruct((t, d), F32))
        out_specs.append(pl.BlockSpec((tt, d), row))
    tr = lambda i: (i // tiles_per_batch, 0, i % tiles_per_batch)
    out_shape += [jax.ShapeDtypeStruct((t, 4 * GLA_WIDTH), BF16),
                  jax.ShapeDtypeStruct((t, 2 * GLA_WIDTH), F32),
                  jax.ShapeDtypeStruct((nb, DIFF_WIDTH, seq), BF16),
                  jax.ShapeDtypeStruct((t, DIFF_WIDTH), BF16),
                  jax.ShapeDtypeStruct((nb, DIFF_WIDTH, seq), BF16)]
    out_specs += [pl.BlockSpec((tt, 4 * GLA_WIDTH), row),
                  pl.BlockSpec((tt, 2 * GLA_WIDTH), row),
                  pl.BlockSpec((1, DIFF_WIDTH, tt), tr),
                  pl.BlockSpec((tt, DIFF_WIDTH), row),
                  pl.BlockSpec((1, DIFF_WIDTH, tt), tr)]
    if not is_sample:
        assert seq == tt
        out_shape += [jax.ShapeDtypeStruct((nb, 1, seq, DIFF_WIDTH), F32)] * 2
        out_specs += [pl.BlockSpec((1, 1, seq, DIFF_WIDTH), lambda i: (i, 0, 0, 0))] * 2

    outs = pl.pallas_call(
        functools.partial(_pre_kernel, has_moe, is_sample),
        grid=(t // tt,),
        in_specs=in_specs,
        out_specs=out_specs,
        out_shape=out_shape,
        compiler_params=_cparams(("arbitrary",)),
        name="pre_mixer",
    )(*ins)
    outs = list(outs)
    x_new = outs.pop(0) if has_moe else x
    return [x_new] + outs


def _gla_chunk(q, k, v, la, st_ref, tri, mask, mid_row, last_row):
    c = GLA_CHUNK
    la_hi, la_lo = _split_bf16(la)
    b = _dot(tri, la_hi) + _dot(tri, la_lo)
    mid = b[mid_row:mid_row + 1]
    last = b[last_row:last_row + 1]
    qf = q.astype(F32)
    kf = k.astype(F32)
    qe = (qf * jnp.exp(b - mid)).astype(BF16)
    ke = (kf * jnp.exp(mid - b)).astype(BF16)
    att = jnp.where(mask, _dot_nt(qe, ke), 0.0)
    o = _dot(att.astype(BF16), v)
    qi = (qf * jnp.exp(b)).astype(BF16)
    ks = (kf * jnp.exp(last - b)).astype(BF16)
    st = st_ref[...]
    o = o + _dot_nt(qi, st.astype(BF16))
    st_ref[...] = st * jnp.exp(last) + _dot_tn(v, ks)
    return o


def _gla_kernel(has_state, n_chunks, *refs):
    refs = list(refs)
    q_ref, k_ref, v_ref, g_ref, laf_ref, lab_ref, gg_ref = refs[:7]
    refs = refs[7:]
    if has_state:
        s0f_ref, s0b_ref = refs[:2]
        refs = refs[2:]
    o_ref, sf_ref, sb_ref, stf_ref, stb_ref, of_ref, ob_ref = refs

    c = GLA_CHUNK
    if has_state:
        stf_ref[...] = s0f_ref[0, 0, 0].T
        stb_ref[...] = s0b_ref[0, 0, 0].T
    else:
        stf_ref[...] = jnp.zeros_like(stf_ref)
        stb_ref[...] = jnp.zeros_like(stb_ref)

    r = lax.broadcasted_iota(jnp.int32, (c, c), 0)
    s = lax.broadcasted_iota(jnp.int32, (c, c), 1)
    lower = r >= s
    upper = r <= s
    tril = jnp.where(lower, 1.0, 0.0).astype(BF16)
    triu = jnp.where(upper, 1.0, 0.0).astype(BF16)

    def body(n, carry):
        rf = pl.ds(pl.multiple_of(n * c, c), c)
        rb = pl.ds(pl.multiple_of((n_chunks - 1 - n) * c, c), c)
        of_ref[rf, :] = _gla_chunk(q_ref[rf, :], k_ref[rf, :], v_ref[rf, :], laf_ref[rf, :],
                                   stf_ref, tril, lower, c // 2, c - 1)
        ob_ref[rb, :] = _gla_chunk(q_ref[rb, :], k_ref[rb, :], v_ref[rb, :], lab_ref[rb, :],
                                   stb_ref, triu, upper, c - 1 - c // 2, 0)
        return carry

    lax.fori_loop(0, n_chunks, body, 0)

    sf_ref[0, 0] = stf_ref[...].T
    sb_ref[0, 0] = stb_ref[...].T

    blk = min(512, n_chunks * c)

    def fin(i, carry):
        rr = pl.ds(pl.multiple_of(i * blk, blk), blk)
        o = of_ref[rr, :] + ob_ref[rr, :]
        ms = jnp.mean(o * o, axis=-1, keepdims=True)
        g = g_ref[rr, :].astype(F32)
        o_ref[rr, :] = (o * lax.rsqrt(ms + NORM_EPS) * gg_ref[0] * (g * _sigmoid(g))).astype(BF16)
        return carry

    lax.fori_loop(0, (n_chunks * c) // blk, fin, 0)


def _gla(l, gla_in, la, gg, s0f, s0b, nb, seq):
    t = gla_in.shape[0]
    h = GLA_HEADS
    has_state = s0f is not None
    n_chunks = seq // GLA_CHUNK
    col = lambda off: (lambda b, hh: (b, off + hh))
    ins = [gla_in, gla_in, gla_in, gla_in, la, la, gg]
    in_specs = [pl.BlockSpec((seq, GLA_DK), col(0)),
                pl.BlockSpec((seq, GLA_DK), col(h)),
                pl.BlockSpec((seq, GLA_DV), col(2 * h)),
                pl.BlockSpec((seq, GLA_DV), col(3 * h)),
                pl.BlockSpec((seq, GLA_DK), col(0)),
                pl.BlockSpec((seq, GLA_DK), col(h)),
                pl.BlockSpec((1, 1, GLA_DV), lambda b, hh: (l, 0, 0))]
    if has_state:
        ins += [s0f, s0b]
        in_specs += [pl.BlockSpec((1, 1, 1, GLA_DK, GLA_DV), lambda b, hh: (b, l, hh, 0, 0))] * 2
    return pl.pallas_call(
        functools.partial(_gla_kernel, has_state, n_chunks),
        grid=(nb, h),
        in_specs=in_specs,
        out_specs=[pl.BlockSpec((seq, GLA_DV), col(0)),
                   pl.BlockSpec((1, 1, GLA_DK, GLA_DV), lambda b, hh: (b, hh, 0, 0)),
                   pl.BlockSpec((1, 1, GLA_DK, GLA_DV), lambda b, hh: (b, hh, 0, 0))],
        out_shape=[jax.ShapeDtypeStruct((t, GLA_WIDTH), BF16),
                   jax.ShapeDtypeStruct((nb, h, GLA_DK, GLA_DV), F32),
                   jax.ShapeDtypeStruct((nb, h, GLA_DK, GLA_DV), F32)],
        scratch_shapes=[pltpu.VMEM((GLA_DV, GLA_DK), F32), pltpu.VMEM((GLA_DV, GLA_DK), F32),
                        pltpu.VMEM((seq, GLA_DV), F32), pltpu.VMEM((seq, GLA_DV), F32)],
        compiler_params=_cparams(("arbitrary", "arbitrary")),
        name="gla",
    )(*ins)


def _attn_kernel(lam_init, has_ctx, n_kt, tk, *refs):
    refs = list(refs)
    qt_ref, k_ref, vt_ref = refs[:3]
    refs = refs[3:]
    if has_ctx:
        ck_ref, cv_ref = refs[:2]
        refs = refs[2:]
    lq1_ref, lk1_ref, lq2_ref, lk2_ref, gd_ref, o_ref = refs

    qt = qt_ref[0]
    tq = qt.shape[1]
    gq = ATTN_GROUP
    n_groups = tq // gq
    dim = lax.broadcasted_iota(jnp.int32, (2 * DIFF_HEAD_DIM, gq), 0)
    zero = jnp.zeros((2 * DIFF_HEAD_DIM, gq), BF16)
    qws = []
    for g in range(n_groups):
        qg = qt[:, g * gq:(g + 1) * gq]
        qws.append(jnp.concatenate([jnp.where(dim < DIFF_HEAD_DIM, qg, zero),
                                    jnp.where(dim >= DIFF_HEAD_DIM, qg, zero)], axis=1))

    def scores(k):
        return [_dot(k, qws[g]) for g in range(n_groups)]

    def absorb(state, sts, vt):
        new = []
        for g in range(n_groups):
            m, acc = state[g]
            st = sts[g]
            m_new = jnp.maximum(m, jnp.max(st, axis=0, keepdims=True))
            alpha = jnp.exp2(m - m_new)
            p = jnp.exp2(st - m_new)
            acc = alpha * acc + _dot(vt, p.astype(BF16))
            new.append((m_new, acc))
        return tuple(new)

    def process(state, tiles):
        sts = scores(tiles[0][0])
        for u, (_, vt) in enumerate(tiles):
            nxt = scores(tiles[u + 1][0]) if u + 1 < len(tiles) else None
            state = absorb(state, sts, vt)
            sts = nxt
        return state

    def with_ones(vt):
        return jnp.concatenate([vt, jnp.ones((ONES_ROWS, vt.shape[1]), BF16)], axis=0)

    state = tuple((jnp.full((1, 2 * gq), -jnp.inf, F32), jnp.zeros((DIFF_V_DIM + ONES_ROWS, 2 * gq), F32))
                  for _ in range(n_groups))
    if has_ctx:
        past = ck_ref.shape[2]
        ctk = tk if past % tk == 0 else past
        cvt = cv_ref[0, 0].T.astype(BF16)
        tiles = [(ck_ref[0, 0, j * ctk:(j + 1) * ctk, :].astype(BF16), with_ones(cvt[:, j * ctk:(j + 1) * ctk]))
                 for j in range(past // ctk)]
        state = process(state, tiles)

    unroll = ATTN_UNROLL if n_kt % ATTN_UNROLL == 0 else 1

    def body(j, st):
        tiles = []
        for u in range(unroll):
            rr = pl.ds(pl.multiple_of((j * unroll + u) * tk, tk), tk)
            tiles.append((k_ref[rr, :], with_ones(vt_ref[0, :, rr])))
        return process(st, tiles)

    state = lax.fori_loop(0, n_kt // unroll, body, state)

    lam = (jnp.exp(jnp.sum(lq1_ref[0] * lk1_ref[0], axis=-1, keepdims=True))
           - jnp.exp(jnp.sum(lq2_ref[0] * lk2_ref[0], axis=-1, keepdims=True)) + lam_init)
    for g in range(n_groups):
        m, acc = state[g]
        on = acc[:DIFF_V_DIM] / acc[DIFF_V_DIM:DIFF_V_DIM + 1]
        o = on[:, :gq] - lam * on[:, gq:]
        ms = jnp.mean(o * o, axis=0, keepdims=True)
        o = o * lax.rsqrt(ms + NORM_EPS) * gd_ref[0] * (1.0 - lam_init)
        o_ref[g * gq:(g + 1) * gq, :] = o.T.astype(BF16)


def _diff_attention(l, lam_init, qt, kd, vt, ctx_k, ctx_v, lams, gd, nb, seq):
    t = kd.shape[0]
    h = DIFF_HEADS
    has_ctx = ctx_k is not None
    tq = min(ATTN_TQ, seq)
    tk = min(ATTN_TK, seq)
    nq = seq // tq
    w = 2 * DIFF_HEAD_DIM
    ins = [qt, kd, vt]
    in_specs = [pl.BlockSpec((1, w, tq), lambda b, hh, i: (b, hh, i)),
                pl.BlockSpec((seq, w), lambda b, hh, i: (b, hh)),
                pl.BlockSpec((1, DIFF_V_DIM, seq), lambda b, hh, i: (b, hh, 0))]
    if has_ctx:
        past = ctx_k.shape[2]
        ins += [ctx_k, ctx_v]
        in_specs += [pl.BlockSpec((1, 1, past, w), lambda b, hh, i: (b, l, 0, hh)),
                     pl.BlockSpec((1, 1, past, DIFF_V_DIM), lambda b, hh, i: (b, l, 0, hh))]
    lay3 = lambda b, hh, i: (l, 0, 0)
    ins += list(lams) + [gd]
    in_specs += [pl.BlockSpec((1, 1, DIFF_HEAD_DIM), lay3)] * 4 + [pl.BlockSpec((1, DIFF_V_DIM, 1), lay3)]
    return pl.pallas_call(
        functools.partial(_attn_kernel, lam_init, has_ctx, seq // tk, tk),
        grid=(nb, h, nq),
        in_specs=in_specs,
        out_specs=pl.BlockSpec((tq, DIFF_V_DIM), lambda b, hh, i: (b * nq + i, hh)),
        out_shape=jax.ShapeDtypeStruct((t, DIFF_WIDTH), BF16),
        compiler_params=_cparams(("arbitrary", "arbitrary", "arbitrary")),
        name="diff_attention",
    )(*ins)


def _top2_of4(v):
    m1 = jnp.maximum(jnp.maximum(v[0], v[1]), jnp.maximum(v[2], v[3]))
    i1 = jnp.where(v[0] == m1, 0, jnp.where(v[1] == m1, 1, jnp.where(v[2] == m1, 2, 3)))
    neg = jnp.full_like(m1, -jnp.inf)
    w = [jnp.where(i1 == j, neg, v[j]) for j in range(4)]
    m2 = jnp.maximum(jnp.maximum(w[0], w[1]), jnp.maximum(w[2], w[3]))
    i2 = jnp.where(w[0] == m2, 0, jnp.where(w[1] == m2, 1, jnp.where(w[2] == m2, 2, 3)))
    return m1, i1, m2, i2


def _post_kernel(og_ref, od_ref, x_ref, mod_ref, g2_ref, wo_ref, wr_ref, br_ref, x1_ref, h2_ref, route_ref):
    d = D_MODEL
    m = mod_ref[0]
    out = _dot(og_ref[...], wo_ref[0, 0:GLA_WIDTH, :]) + _dot(od_ref[...], wo_ref[0, GLA_WIDTH:, :])
    x1 = x_ref[...] + m[:, 2 * d:3 * d] * out
    x1_ref[...] = x1
    ms = jnp.mean(x1 * x1, axis=-1, keepdims=True)
    h2 = x1 * lax.rsqrt(ms + NORM_EPS) * (g2_ref[0] * (1.0 + m[:, 4 * d:5 * d])) + m[:, 3 * d:4 * d]
    h_hi, h_lo = _split_bf16(h2)
    h2_ref[...] = h_hi
    w_hi, w_lo = _split_bf16(wr_ref[...])
    z = _dot_nt(w_hi, h_hi) + _dot_nt(w_lo, h_hi) + _dot_nt(w_hi, h_lo)
    s = _sigmoid(z)
    sel = s + br_ref[...]

    e = EXPERTS_PER_GROUP
    tops = []
    for g in range(N_GROUPS):
        tops.append(_top2_of4([sel[g * e + j:g * e + j + 1, :] for j in range(e)]))
    score = [t[0] + t[2] for t in tops]
    best = jnp.maximum(jnp.maximum(score[0], score[1]), jnp.maximum(score[2], score[3]))
    gi = jnp.where(score[0] == best, 0, jnp.where(score[1] == best, 1, jnp.where(score[2] == best, 2, 3)))

    def pick(rows):
        return jnp.where(gi == 0, rows[0], jnp.where(gi == 1, rows[1], jnp.where(gi == 2, rows[2], rows[3])))

    i1 = pick([t[1] for t in tops])
    i2 = pick([t[3] for t in tops])

    def gate_of(idx):
        per_group = []
        for g in range(N_GROUPS):
            rows = [s[g * e + j:g * e + j + 1, :] for j in range(e)]
            per_group.append(jnp.where(idx == 0, rows[0], jnp.where(idx == 1, rows[1],
                                                                     jnp.where(idx == 2, rows[2], rows[3]))))
        return pick(per_group)

    s1 = gate_of(i1)
    s2 = gate_of(i2)
    tot = s1 + s2
    w1 = s1 / tot
    w2 = s2 / tot
    lo = jnp.minimum(i1, i2)
    hi = jnp.maximum(i1, i2)
    w_lo = jnp.where(i1 < i2, w1, w2)
    w_hi = jnp.where(i1 < i2, w2, w1)
    pair = jnp.where(lo == 0, hi - 1, jnp.where(lo == 1, jnp.where(hi == 3, 3, 4), 5))
    swapped = pair == 5
    route_ref[...] = jnp.zeros_like(route_ref)
    route_ref[0:1, :] = (gi * N_PAIRS + pair).astype(F32)
    route_ref[1:2, :] = jnp.where(swapped, w_hi, w_lo)
    route_ref[2:3, :] = jnp.where(swapped, w_lo, w_hi)


def _post_mixer(l, og, od, x, mod, prep, seq, is_sample):
    t, d = x.shape
    tt = TOKEN_TILE
    tiles_per_batch = seq // tt
    if is_sample:
        mod_map = lambda i: (1 + i // tiles_per_batch, 0, 0)
    else:
        mod_map = lambda i: (0, 0, 0)
    row = lambda i: (i, 0)
    lay3 = lambda i: (l, 0, 0)
    return pl.pallas_call(
        _post_kernel,
        grid=(t // tt,),
        in_specs=[pl.BlockSpec((tt, GLA_WIDTH), row),
                  pl.BlockSpec((tt, DIFF_WIDTH), row),
                  pl.BlockSpec((tt, d), row),
                  pl.BlockSpec((1, 1, 6 * d), mod_map),
                  pl.BlockSpec((1, 1, d), lay3),
                  pl.BlockSpec((1, d, d), lay3),
                  pl.BlockSpec((N_EXPERTS, d), lambda i: (0, 0)),
                  pl.BlockSpec((N_EXPERTS, 1), lambda i: (0, 0))],
        out_specs=[pl.BlockSpec((tt, d), row),
                   pl.BlockSpec((tt, d), row),
                   pl.BlockSpec((8, tt), lambda i: (0, i))],
        out_shape=[jax.ShapeDtypeStruct((t, d), F32),
                   jax.ShapeDtypeStruct((t, d), BF16),
                   jax.ShapeDtypeStruct((8, t), F32)],
        compiler_params=_cparams(("arbitrary",)),
        name="post_mixer",
    )(og, od, x, mod, prep["g2"], prep["w_out"], prep["w_router_t"], prep["b_router"])


def _moe_kernel(ea_ref, eb_ref, valid_ref, x_ref, gate_ref, wga_ref, wua_ref, wda_ref, wgb_ref, wub_ref, wdb_ref,
                y_ref):
    i = pl.program_id(0)

    @pl.when(valid_ref[i] > 0)
    def _():
        x = x_ref[...]
        gate = gate_ref[...]

        def ffn(wg_ref, wu_ref, wd_ref):
            hg = _dot(x, wg_ref[0, 0])
            hu = _dot(x, wu_ref[0, 0])
            a = (hg * _sigmoid(hg)) * hu
            return _dot(a.astype(BF16), wd_ref[0, 0])

        y = gate[:, 0:1] * ffn(wga_ref, wua_ref, wda_ref) + gate[:, 1:2] * ffn(wgb_ref, wub_ref, wdb_ref)
        y_ref[...] = y.astype(BF16)

    @pl.when(valid_ref[i] == 0)
    def _():
        y_ref[...] = jnp.zeros_like(y_ref)


def _moe(l, h2, route, prep):
    t, d = h2.shape
    tm = MOE_TILE
    p_rows = t + N_COMBOS * tm
    n_tiles = p_rows // tm

    combo = route[0].astype(jnp.int32)
    order = jnp.argsort(combo).astype(jnp.int32)
    rank = jnp.argsort(order).astype(jnp.int32)
    counts = jnp.sum(combo[:, None] == jnp.arange(N_COMBOS, dtype=jnp.int32)[None, :], axis=0).astype(jnp.int32)
    padded = ((counts + tm - 1) // tm) * tm
    pad_end = jnp.cumsum(padded)
    pad_start = pad_end - padded
    start = jnp.cumsum(counts) - counts
    tile_start = jnp.arange(n_tiles, dtype=jnp.int32) * tm
    valid = (tile_start < pad_end[-1]).astype(jnp.int32)
    last_valid_start = jnp.maximum(pad_end[-1] - tm, 0)
    tile_combo = jnp.sum(jnp.minimum(tile_start, last_valid_start)[:, None] >= pad_end[None, :], axis=1)
    tile_combo = jnp.minimum(tile_combo, N_COMBOS - 1).astype(jnp.int32)
    slot = jnp.arange(p_rows, dtype=jnp.int32)
    slot_combo = jnp.repeat(tile_combo, tm)
    within = slot - pad_start[slot_combo]
    slot_valid = (within < counts[slot_combo]) & (slot < pad_end[-1])
    row_ids = jnp.where(slot_valid, order[jnp.clip(within + start[slot_combo], 0, t - 1)], 0)
    gates = jnp.where(slot_valid[:, None], route[1:3].T[row_ids], 0.0)
    pos = pad_start[combo] + rank - start[combo]
    grp = tile_combo // N_PAIRS
    pr = tile_combo % N_PAIRS
    ea = grp * EXPERTS_PER_GROUP + jnp.asarray(PAIR_A, jnp.int32)[pr]
    eb = grp * EXPERTS_PER_GROUP + jnp.asarray(PAIR_B, jnp.int32)[pr]

    xs = jnp.take(h2, row_ids, axis=0)

    wa = lambda i, ea_r, eb_r, v_r: (l, ea_r[i], 0, 0)
    wb = lambda i, ea_r, eb_r, v_r: (l, eb_r[i], 0, 0)
    row = lambda i, ea_r, eb_r, v_r: (i, 0)
    up = pl.BlockSpec((1, 1, d, D_EXPERT), wa)
    dn = pl.BlockSpec((1, 1, D_EXPERT, d), wa)
    upb = pl.BlockSpec((1, 1, d, D_EXPERT), wb)
    dnb = pl.BlockSpec((1, 1, D_EXPERT, d), wb)
    ys = pl.pallas_call(
        _moe_kernel,
        grid_spec=pltpu.PrefetchScalarGridSpec(
            num_scalar_prefetch=3,
            grid=(n_tiles,),
            in_specs=[pl.BlockSpec((tm, d), row), pl.BlockSpec((tm, 2), row), up, up, dn, upb, upb, dnb],
            out_specs=pl.BlockSpec((tm, d), row)),
        out_shape=jax.ShapeDtypeStruct((p_rows, d), BF16),
        compiler_params=_cparams(("arbitrary",)),
        name="moe",
    )(ea, eb, valid, xs, gates, prep["w_eg"], prep["w_eu"], prep["w_ed"], prep["w_eg"], prep["w_eu"], prep["w_ed"])
    return jnp.take(ys, pos, axis=0)


def _resid_kernel(x_ref, y_ref, mod_ref, o_ref):
    d = D_MODEL
    o_ref[...] = x_ref[...] + mod_ref[0][:, 5 * d:6 * d] * y_ref[...].astype(F32)


def _residual(x, y, mod, seq, is_sample):
    t, d = x.shape
    tt = TOKEN_TILE
    tiles_per_batch = seq // tt
    if is_sample:
        mod_map = lambda i: (1 + i // tiles_per_batch, 0, 0)
    else:
        mod_map = lambda i: (0, 0, 0)
    row = lambda i: (i, 0)
    return pl.pallas_call(
        _resid_kernel,
        grid=(t // tt,),
        in_specs=[pl.BlockSpec((tt, d), row), pl.BlockSpec((tt, d), row), pl.BlockSpec((1, 1, 6 * d), mod_map)],
        out_specs=pl.BlockSpec((tt, d), row),
        out_shape=jax.ShapeDtypeStruct((t, d), F32),
        compiler_params=_cparams(("arbitrary",)),
        name="residual",
    )(x, y, mod)


def _rope_tables(length):
    rows = length // GRID_W
    row = jnp.repeat(jnp.arange(rows, dtype=F32), GRID_W)
    col = jnp.tile(jnp.arange(GRID_W, dtype=F32), rows)
    inv = ROPE_BASE ** (-jnp.arange(0, ROPE_HALF, 2, dtype=F32) / ROPE_HALF)
    ang_r = row[:, None] * inv[None, :]
    ang_c = col[:, None] * inv[None, :]
    cos = jnp.concatenate([jnp.cos(ang_r), jnp.cos(ang_r), jnp.cos(ang_c), jnp.cos(ang_c)], axis=-1)
    sin = jnp.concatenate([-jnp.sin(ang_r), jnp.sin(ang_r), -jnp.sin(ang_c), jnp.sin(ang_c)], axis=-1)
    reps = DIFF_WIDTH // DIFF_HEAD_DIM
    return jnp.tile(cos, (1, reps)), jnp.tile(sin, (1, reps))


def _prepare(g_norm1, g_norm2, w_in, w_gate_fwd, b_gate_fwd, w_gate_bwd, b_gate_bwd, g_q_norm, g_k_norm,
             w_out, w_router, b_router, w_exp_gate, w_exp_up, w_exp_down):
    depth, d, _ = w_in.shape
    r = GLA_GATE_RANK
    gate_cols = jnp.pad(w_in[..., 2048:2048 + 2 * r], ((0, 0), (0, 0), (0, GATE_COLS - 2 * r)))
    w_in_r = jnp.concatenate([w_in[..., :2048], w_in[..., 2048 + 2 * r:], gate_cols], axis=-1).astype(BF16)
    w_gate = jnp.zeros((depth, GATE_COLS, 2 * GLA_WIDTH), F32)
    w_gate = w_gate.at[:, 0:r, 0:GLA_WIDTH].set(w_gate_fwd).at[:, r:2 * r, GLA_WIDTH:].set(w_gate_bwd).astype(BF16)
    b_gate = jnp.concatenate([b_gate_fwd, b_gate_bwd], axis=-1).reshape(depth, 1, 2 * GLA_WIDTH)
    grp = jnp.arange(DIFF_WIDTH) // DIFF_HEAD_DIM
    gmat = jnp.where(grp[:, None] == grp[None, :], 1.0 / DIFF_HEAD_DIM, 0.0).astype(BF16)
    reps = DIFF_WIDTH // DIFF_HEAD_DIM
    return {
        "g1": g_norm1.reshape(depth, 1, d),
        "g2": g_norm2.reshape(depth, 1, d),
        "w_in": w_in_r,
        "w_gate": w_gate,
        "b_gate": b_gate,
        "gmat": gmat,
        "gq": jnp.tile(g_q_norm, (1, reps)).reshape(depth, 1, DIFF_WIDTH),
        "gk": jnp.tile(g_k_norm, (1, reps)).reshape(depth, 1, DIFF_WIDTH),
        "w_out": w_out.astype(BF16),
        "w_router_t": w_router.T,
        "b_router": b_router.reshape(N_EXPERTS, 1),
        "w_eg": w_exp_gate.astype(BF16),
        "w_eu": w_exp_up.astype(BF16),
        "w_ed": w_exp_down.astype(BF16),
    }


def kernel(x_prompt, x_sample, cache_k, cache_v, state_gla_fwd, state_gla_bwd, c, c_ctx, w_mod, b_mod, g_norm1,
           g_norm2, w_in, w_gate_fwd, b_gate_fwd, w_gate_bwd, b_gate_bwd, g_gla_out, g_q_norm, g_k_norm, lambda_q1,
           lambda_k1, lambda_q2, lambda_k2, g_diff_out, w_out, w_router, b_router, w_exp_gate, w_exp_up, w_exp_down):
    nb_p, seq_p, d = x_prompt.shape
    nb_s, seq_s, _ = x_sample.shape
    depth = w_in.shape[0]
    past = cache_k.shape[2]

    prep = _prepare(g_norm1, g_norm2, w_in, w_gate_fwd, b_gate_fwd, w_gate_bwd, b_gate_bwd, g_q_norm, g_k_norm,
                    w_out, w_router, b_router, w_exp_gate, w_exp_up, w_exp_down)
    rope = _rope_tables(seq_s)

    mod_rows = 8 * ((1 + nb_s + 7) // 8)
    cs = jnp.zeros((mod_rows, d), F32).at[0].set(c_ctx).at[1:1 + nb_s].set(c)
    mod_all = _modulation(cs, w_mod, b_mod)
    ctx_k = cache_k.reshape(nb_s, depth, past, DIFF_WIDTH)
    ctx_v = cache_v.reshape(nb_s, depth, past, DIFF_WIDTH)
    gg = g_gla_out.reshape(depth, 1, GLA_DV)
    gd = g_diff_out.reshape(depth, DIFF_V_DIM, 1)
    lams = [a.reshape(depth, 1, DIFF_HEAD_DIM) for a in (lambda_q1, lambda_k1, lambda_q2, lambda_k2)]

    xp = x_prompt.reshape(nb_p * seq_p, d)
    xs = x_sample.reshape(nb_s * seq_s, d)
    yp = ys = None
    mod_prev = None
    new_k, new_v, new_sf, new_sb = [], [], [], []
    for l in range(depth):
        lam_init = 0.8 - 0.6 * math.exp(-0.3 * l)
        mod = mod_all[l].reshape(mod_rows, 1, 6 * d)
        xp, gla_in, la, qt, kd, vt, k_l, v_l = _pre_mixer(l, xp, yp, mod_prev, mod, prep, None, nb_p, seq_p, False)
        og, sf_l, sb_l = _gla(l, gla_in, la, gg, None, None, nb_p, seq_p)
        od = _diff_attention(l, lam_init, qt, kd, vt, None, None, lams, gd, nb_p, seq_p)
        xp, h2, route = _post_mixer(l, og, od, xp, mod, prep, seq_p, False)
        yp = _moe(l, h2, route, prep)
        new_k.append(k_l)
        new_v.append(v_l)
        new_sf.append(sf_l)
        new_sb.append(sb_l)
        xs, gla_in, la, qt, kd, vt = _pre_mixer(l, xs, ys, mod_prev, mod, prep, rope, nb_s, seq_s, True)
        og, _, _ = _gla(l, gla_in, la, gg, state_gla_fwd, state_gla_bwd, nb_s, seq_s)
        od = _diff_attention(l, lam_init, qt, kd, vt, ctx_k, ctx_v, lams, gd, nb_s, seq_s)
        xs, h2, route = _post_mixer(l, og, od, xs, mod, prep, seq_s, True)
        ys = _moe(l, h2, route, prep)
        mod_prev = mod
    xp = _residual(xp, yp, mod_prev, seq_p, False)
    xs = _residual(xs, ys, mod_prev, seq_s, True)

    new_cache_k = jnp.concatenate(new_k, axis=1).reshape(nb_p, depth, seq_p, DIFF_HEADS, 2, DIFF_HEAD_DIM)
    new_cache_v = jnp.concatenate(new_v, axis=1).reshape(nb_p, depth, seq_p, DIFF_HEADS, DIFF_V_DIM)
    new_sf = jnp.stack(new_sf, axis=1)
    new_sb = jnp.stack(new_sb, axis=1)
    return (xp.reshape(nb_p, seq_p, d), xs.reshape(nb_s, seq_s, d), new_cache_k, new_cache_v, new_sf, new_sb)
```

```python
import functools
import math

import jax
import jax.numpy as jnp
from jax import lax
from jax.experimental import pallas as pl
from jax.experimental.pallas import tpu as pltpu

F32 = jnp.float32
BF16 = jnp.bfloat16

D_MODEL = 1024
GLA_HEADS = 4
GLA_DK = 128
GLA_DV = 128
GLA_WIDTH = GLA_HEADS * GLA_DK
GLA_GATE_RANK = 16
GLA_GATE_NORMALIZER = 16.0
GLA_CHUNK = 64
GLA_HEADS_PER_STEP = 2
DIFF_HEADS = 4
DIFF_HEAD_DIM = 64
DIFF_V_DIM = 128
DIFF_WIDTH = DIFF_HEADS * 2 * DIFF_HEAD_DIM
ROPE_HALF = DIFF_HEAD_DIM // 2
ROPE_BASE = 10000.0
GRID_W = 64
N_EXPERTS = 16
N_GROUPS = 4
EXPERTS_PER_GROUP = 4
D_EXPERT = 512
NORM_EPS = 1e-6
LOG2E = 1.4426950408889634

PAIR_A = (0, 0, 0, 1, 1, 3)
PAIR_B = (1, 2, 3, 3, 2, 2)
N_PAIRS = len(PAIR_A)
N_COMBOS = N_GROUPS * N_PAIRS

GATE_COLS = 128
MAIN_COLS = 7 * 512
TOKEN_TILE = 256
MOE_TILE = 256
DMA_UNROLL = 8
ATTN_TQ = 1024
ATTN_TK = 256
ATTN_GROUP = 128
ATTN_UNROLL = 4
ONES_ROWS = 16
VMEM_LIMIT = 56 * 1024 * 1024


def _cparams(sem):
    return pltpu.CompilerParams(dimension_semantics=sem, vmem_limit_bytes=VMEM_LIMIT)


def _dot(a, b):
    return jnp.dot(a, b, preferred_element_type=F32)


def _dot_nt(a, b):
    return lax.dot_general(a, b, (((1,), (1,)), ((), ())), preferred_element_type=F32)


def _dot_tn(a, b):
    return lax.dot_general(a, b, (((0,), (0,)), ((), ())), preferred_element_type=F32)


def _sigmoid(x):
    return 1.0 / (1.0 + jnp.exp(-x))


def _split_bf16(x):
    hi = x.astype(BF16)
    lo = (x - hi.astype(F32)).astype(BF16)
    return hi, lo


def _mod_kernel(c_ref, w_ref, b_ref, o_ref):
    c = c_ref[...]
    s = c * _sigmoid(c)
    o_ref[0] = jnp.dot(s, w_ref[0], preferred_element_type=F32,
                       precision=lax.Precision.HIGHEST) + b_ref[0]


def _modulation(cs, w_mod, b_mod):
    depth, d, n = w_mod.shape
    rows = cs.shape[0]
    tn = 1536
    return pl.pallas_call(
        _mod_kernel,
        grid=(depth, n // tn),
        in_specs=[pl.BlockSpec((rows, d), lambda l, j: (0, 0)),
                  pl.BlockSpec((1, d, tn), lambda l, j: (l, 0, j)),
                  pl.BlockSpec((1, 1, tn), lambda l, j: (l, 0, j))],
        out_specs=pl.BlockSpec((1, rows, tn), lambda l, j: (l, 0, j)),
        out_shape=jax.ShapeDtypeStruct((depth, rows, n), F32),
        compiler_params=_cparams(("arbitrary", "arbitrary")),
        name="modulation",
    )(cs, w_mod, b_mod.reshape(depth, 1, n))


def _group_rms(p, gmat_ref, g):
    ms = _dot((p * p).astype(BF16), gmat_ref[...])
    return p * lax.rsqrt(ms + NORM_EPS) * g


def _rope(y, cos, sin):
    w = y.shape[1]
    lane = lax.broadcasted_iota(jnp.int32, y.shape, 1)
    first = (lane % (2 * (ROPE_HALF // 2))) < (ROPE_HALF // 2)
    partner = jnp.where(first, pltpu.roll(y, w - ROPE_HALF // 2, axis=1), pltpu.roll(y, ROPE_HALF // 2, axis=1))
    return y * cos + partner * sin


def _pre_kernel(has_moe, is_sample, *refs):
    refs = list(refs)
    x_ref = refs.pop(0)
    if has_moe:
        y_ref, modp_ref = refs[:2]
        refs = refs[2:]
    mod_ref, g1_ref, win_ref, wgate_ref, bgate_ref, tri_ref, gmat_ref, gq_ref, gk_ref = refs[:9]
    refs = refs[9:]
    if is_sample:
        cos_ref, sin_ref = refs[:2]
        refs = refs[2:]
    xo_ref = refs.pop(0) if has_moe else None
    gla_ref, la_ref, qt_ref, kd_ref, vt_ref = refs[:5]
    refs = refs[5:]
    if not is_sample:
        kc_ref, vc_ref = refs

    d = D_MODEL
    m = mod_ref[0]
    x = x_ref[...]
    if has_moe:
        x = x + modp_ref[0][:, 5 * d:6 * d] * y_ref[...].astype(F32)
        xo_ref[...] = x
    ms = jnp.mean(x * x, axis=-1, keepdims=True)
    gs = g1_ref[0] * (1.0 + m[:, d:2 * d])
    hb = (x * lax.rsqrt(ms + NORM_EPS) * gs + m[:, 0:d]).astype(BF16)

    for j in range(4):
        p = _dot(hb, win_ref[0, :, j * 512:(j + 1) * 512])
        if j == 0:
            p = p * (GLA_DK ** -0.5)
        gla_ref[:, j * 512:(j + 1) * 512] = p.astype(BF16)

    a = _dot(hb, win_ref[0, :, MAIN_COLS:MAIN_COLS + GATE_COLS])
    z = _dot(a.astype(BF16), wgate_ref[0]) + bgate_ref[0]
    la = (jnp.minimum(z, 0.0) - jnp.log1p(jnp.exp(-jnp.abs(z)))) * (1.0 / GLA_GATE_NORMALIZER)
    la_hi, la_lo = _split_bf16(la)
    w = GLA_WIDTH
    la_ref[:, 0:w] = _dot(tri_ref[0], la_hi[:, 0:w]) + _dot(tri_ref[0], la_lo[:, 0:w])
    la_ref[:, w:2 * w] = _dot(tri_ref[1], la_hi[:, w:2 * w]) + _dot(tri_ref[1], la_lo[:, w:2 * w])

    scale = DIFF_HEAD_DIM ** -0.5 * LOG2E
    q = _group_rms(_dot(hb, win_ref[0, :, 2048:2560]), gmat_ref, gq_ref[0])
    k = _group_rms(_dot(hb, win_ref[0, :, 2560:3072]), gmat_ref, gk_ref[0])
    v = _dot(hb, win_ref[0, :, 3072:3584])
    if is_sample:
        cos = cos_ref[...]
        sin = sin_ref[...]
        q = _rope(q, cos, sin)
        k = _rope(k, cos, sin)
    else:
        kc_ref[0, 0] = k
        vc_ref[0, 0] = v
    qt_ref[0] = (q * scale).astype(BF16).T
    kd_ref[...] = k.astype(BF16)
    vt_ref[0] = v.astype(BF16).T


def _pre_mixer(l, x, y, mod_prev, mod, prep, rope, nb, seq, is_sample):
    t, d = x.shape
    tt = TOKEN_TILE
    tiles_per_batch = seq // tt
    has_moe = y is not None
    if is_sample:
        mod_map = lambda i: (1 + i // tiles_per_batch, 0, 0)
    else:
        mod_map = lambda i: (0, 0, 0)
    row = lambda i: (i, 0)
    const2 = lambda i: (0, 0)
    lay3 = lambda i: (l, 0, 0)

    ins = [x]
    in_specs = [pl.BlockSpec((tt, d), row)]
    if has_moe:
        ins += [y, mod_prev]
        in_specs += [pl.BlockSpec((tt, d), row), pl.BlockSpec((1, 1, 6 * d), mod_map)]
    ins += [mod, prep["g1"], prep["w_in"], prep["w_gate"], prep["b_gate"], prep["tri"], prep["gmat"], prep["gq"],
            prep["gk"]]
    in_specs += [pl.BlockSpec((1, 1, 6 * d), mod_map),
                 pl.BlockSpec((1, 1, d), lay3),
                 pl.BlockSpec((1, d, MAIN_COLS + GATE_COLS), lay3),
                 pl.BlockSpec((1, GATE_COLS, 2 * GLA_WIDTH), lay3),
                 pl.BlockSpec((1, 1, 2 * GLA_WIDTH), lay3),
                 pl.BlockSpec((2, tt, tt), lambda i: (0, 0, 0)),
                 pl.BlockSpec((DIFF_WIDTH, DIFF_WIDTH), const2),
                 pl.BlockSpec((1, 1, DIFF_WIDTH), lay3),
                 pl.BlockSpec((1, 1, DIFF_WIDTH), lay3)]
    if is_sample:
        ins += [rope[0], rope[1]]
        in_specs += [pl.BlockSpec((tt, DIFF_WIDTH), lambda i: (i % tiles_per_batch, 0))] * 2

    out_shape = []
    out_specs = []
    if has_moe:
        out_shape.append(jax.ShapeDtypeStruct((t, d), F32))
        out_specs.append(pl.BlockSpec((tt, d), row))
    tr = lambda i: (i // tiles_per_batch, 0, i % tiles_per_batch)
    out_shape += [jax.ShapeDtypeStruct((t, 4 * GLA_WIDTH), BF16),
                  jax.ShapeDtypeStruct((t, 2 * GLA_WIDTH), F32),
                  jax.ShapeDtypeStruct((nb, DIFF_WIDTH, seq), BF16),
                  jax.ShapeDtypeStruct((t, DIFF_WIDTH), BF16),
                  jax.ShapeDtypeStruct((nb, DIFF_WIDTH, seq), BF16)]
    out_specs += [pl.BlockSpec((tt, 4 * GLA_WIDTH), row),
                  pl.BlockSpec((tt, 2 * GLA_WIDTH), row),
                  pl.BlockSpec((1, DIFF_WIDTH, tt), tr),
                  pl.BlockSpec((tt, DIFF_WIDTH), row),
                  pl.BlockSpec((1, DIFF_WIDTH, tt), tr)]
    if not is_sample:
        assert seq == tt
        out_shape += [jax.ShapeDtypeStruct((nb, 1, seq, DIFF_WIDTH), F32)] * 2
        out_specs += [pl.BlockSpec((1, 1, seq, DIFF_WIDTH), lambda i: (i, 0, 0, 0))] * 2

    outs = pl.pallas_call(
        functools.partial(_pre_kernel, has_moe, is_sample),
        grid=(t // tt,),
        in_specs=in_specs,
        out_specs=out_specs,
        out_shape=out_shape,
        compiler_params=_cparams(("arbitrary",)),
        name="pre_mixer",
    )(*ins)
    outs = list(outs)
    x_new = outs.pop(0) if has_moe else x
    return [x_new] + outs


def _gla_kernel(has_state, n_chunks, hg, *refs):
    refs = list(refs)
    q_ref, k_ref, v_ref, g_ref, bf_ref, bb_ref, gg_ref = refs[:7]
    refs = refs[7:]
    if has_state:
        s0f_ref, s0b_ref = refs[:2]
        refs = refs[2:]
    o_ref, sf_ref, sb_ref, st_ref, acc_ref = refs

    c = GLA_CHUNK
    dk = GLA_DK
    for hd in range(hg):
        if has_state:
            st_ref[2 * hd] = s0f_ref[0, 0, hd].T
            st_ref[2 * hd + 1] = s0b_ref[0, 0, hd].T
        else:
            st_ref[2 * hd] = jnp.zeros((GLA_DV, dk), F32)
            st_ref[2 * hd + 1] = jnp.zeros((GLA_DV, dk), F32)

    r = lax.broadcasted_iota(jnp.int32, (c, c), 0)
    s = lax.broadcasted_iota(jnp.int32, (c, c), 1)
    lower = r >= s
    upper = r <= s

    def scores(rows, hd, b_ref, st_i, mid_row, last_row):
        cs = slice(hd * dk, (hd + 1) * dk)
        q = q_ref[rows, cs].astype(F32)
        k = k_ref[rows, cs].astype(F32)
        v = v_ref[rows, cs]
        b = b_ref[rows, cs]
        mid = b[mid_row:mid_row + 1]
        last = b[last_row:last_row + 1]
        qe = (q * jnp.exp(b - mid)).astype(BF16)
        ke = (k * jnp.exp(mid - b)).astype(BF16)
        qi = (q * jnp.exp(b)).astype(BF16)
        ks = (k * jnp.exp(last - b)).astype(BF16)
        st = st_ref[st_i]
        return _dot_nt(qe, ke), _dot_nt(qi, st.astype(BF16)), _dot_tn(v, ks), v, st, last

    def outputs(sc, mask, st_i):
        att, o_inter, kv, v, st, last = sc
        st_ref[st_i] = st * jnp.exp(last) + kv
        return _dot(jnp.where(mask, att, 0.0).astype(BF16), v) + o_inter

    def finish(o, rows, hd):
        cs = slice(hd * dk, (hd + 1) * dk)
        ms = jnp.mean(o * o, axis=-1, keepdims=True)
        g = g_ref[rows, cs].astype(F32)
        o_ref[rows, cs] = (o * lax.rsqrt(ms + NORM_EPS) * gg_ref[0] * (g * _sigmoid(g))).astype(BF16)

    def step(n, second_visit):
        rf = pl.ds(pl.multiple_of(n * c, c), c)
        rb = pl.ds(pl.multiple_of((n_chunks - 1 - n) * c, c), c)
        sc = []
        for hd in range(hg):
            sc.append(scores(rf, hd, bf_ref, 2 * hd, c // 2, c - 1))
            sc.append(scores(rb, hd, bb_ref, 2 * hd + 1, c - 1 - c // 2, 0))
        for hd in range(hg):
            cs = slice(hd * dk, (hd + 1) * dk)
            for rows, sci, mask, st_i in ((rf, sc[2 * hd], lower, 2 * hd), (rb, sc[2 * hd + 1], upper, 2 * hd + 1)):
                o = outputs(sci, mask, st_i)
                if second_visit:
                    finish(acc_ref[rows, cs] + o, rows, hd)
                else:
                    acc_ref[rows, cs] = o

    half = n_chunks // 2
    lax.fori_loop(0, half, lambda n, carry: (step(n, False), carry)[1], 0)
    lax.fori_loop(half, n_chunks, lambda n, carry: (step(n, True), carry)[1], 0)

    for hd in range(hg):
        sf_ref[0, hd] = st_ref[2 * hd].T
        sb_ref[0, hd] = st_ref[2 * hd + 1].T


def _gla(l, gla_in, bsum, gg, s0f, s0b, nb, seq):
    t = gla_in.shape[0]
    h = GLA_HEADS
    hg = GLA_HEADS_PER_STEP
    nhb = h // hg
    w = hg * GLA_DK
    has_state = s0f is not None
    n_chunks = seq // GLA_CHUNK
    assert n_chunks % 2 == 0
    col = lambda off: (lambda b, hh: (b, off + hh))
    ins = [gla_in, gla_in, gla_in, gla_in, bsum, bsum, gg]
    in_specs = [pl.BlockSpec((seq, w), col(0)),
                pl.BlockSpec((seq, w), col(nhb)),
                pl.BlockSpec((seq, w), col(2 * nhb)),
                pl.BlockSpec((seq, w), col(3 * nhb)),
                pl.BlockSpec((seq, w), col(0)),
                pl.BlockSpec((seq, w), col(nhb)),
                pl.BlockSpec((1, 1, GLA_DV), lambda b, hh: (l, 0, 0))]
    if has_state:
        ins += [s0f, s0b]
        in_specs += [pl.BlockSpec((1, 1, hg, GLA_DK, GLA_DV), lambda b, hh: (b, l, hh, 0, 0))] * 2
    return pl.pallas_call(
        functools.partial(_gla_kernel, has_state, n_chunks, hg),
        grid=(nb, nhb),
        in_specs=in_specs,
        out_specs=[pl.BlockSpec((seq, w), col(0)),
                   pl.BlockSpec((1, hg, GLA_DK, GLA_DV), lambda b, hh: (b, hh, 0, 0)),
                   pl.BlockSpec((1, hg, GLA_DK, GLA_DV), lambda b, hh: (b, hh, 0, 0))],
        out_shape=[jax.ShapeDtypeStruct((t, GLA_WIDTH), BF16),
                   jax.ShapeDtypeStruct((nb, h, GLA_DK, GLA_DV), F32),
                   jax.ShapeDtypeStruct((nb, h, GLA_DK, GLA_DV), F32)],
        scratch_shapes=[pltpu.VMEM((2 * hg, GLA_DV, GLA_DK), F32), pltpu.VMEM((seq, w), F32)],
        compiler_params=_cparams(("arbitrary", "arbitrary")),
        name="gla",
    )(*ins)


def _attn_kernel(lam_init, has_ctx, n_kt, tk, *refs):
    refs = list(refs)
    qt_ref, k_ref, vt_ref = refs[:3]
    refs = refs[3:]
    if has_ctx:
        ck_ref, cv_ref = refs[:2]
        refs = refs[2:]
    lq1_ref, lk1_ref, lq2_ref, lk2_ref, gd_ref, o_ref = refs

    qt = qt_ref[0]
    tq = qt.shape[1]
    gq = ATTN_GROUP
    n_groups = tq // gq
    dim = lax.broadcasted_iota(jnp.int32, (2 * DIFF_HEAD_DIM, gq), 0)
    zero = jnp.zeros((2 * DIFF_HEAD_DIM, gq), BF16)
    qws = []
    for g in range(n_groups):
        qg = qt[:, g * gq:(g + 1) * gq]
        qws.append(jnp.concatenate([jnp.where(dim < DIFF_HEAD_DIM, qg, zero),
                                    jnp.where(dim >= DIFF_HEAD_DIM, qg, zero)], axis=1))

    def scores(k):
        return [_dot(k, qws[g]) for g in range(n_groups)]

    def absorb(state, sts, vt):
        new = []
        for g in range(n_groups):
            m, acc = state[g]
            st = sts[g]
            m_new = jnp.maximum(m, jnp.max(st, axis=0, keepdims=True))
            alpha = jnp.exp2(m - m_new)
            p = jnp.exp2(st - m_new)
            acc = alpha * acc + _dot(vt, p.astype(BF16))
            new.append((m_new, acc))
        return tuple(new)

    def process(state, tiles):
        sts = scores(tiles[0][0])
        for u, (_, vt) in enumerate(tiles):
            nxt = scores(tiles[u + 1][0]) if u + 1 < len(tiles) else None
            state = absorb(state, sts, vt)
            sts = nxt
        return state

    def with_ones(vt):
        return jnp.concatenate([vt, jnp.ones((ONES_ROWS, vt.shape[1]), BF16)], axis=0)

    state = tuple((jnp.full((1, 2 * gq), -jnp.inf, F32), jnp.zeros((DIFF_V_DIM + ONES_ROWS, 2 * gq), F32))
                  for _ in range(n_groups))
    if has_ctx:
        past = ck_ref.shape[2]
        ctk = tk if past % tk == 0 else past
        cvt = cv_ref[0, 0].T.astype(BF16)
        tiles = [(ck_ref[0, 0, j * ctk:(j + 1) * ctk, :].astype(BF16), with_ones(cvt[:, j * ctk:(j + 1) * ctk]))
                 for j in range(past // ctk)]
        state = process(state, tiles)

    unroll = ATTN_UNROLL if n_kt % ATTN_UNROLL == 0 else 1

    def body(j, st):
        tiles = []
        for u in range(unroll):
            rr = pl.ds(pl.multiple_of((j * unroll + u) * tk, tk), tk)
            tiles.append((k_ref[rr, :], with_ones(vt_ref[0, :, rr])))
        return process(st, tiles)

    state = lax.fori_loop(0, n_kt // unroll, body, state)

    lam = (jnp.exp(jnp.sum(lq1_ref[0] * lk1_ref[0], axis=-1, keepdims=True))
           - jnp.exp(jnp.sum(lq2_ref[0] * lk2_ref[0], axis=-1, keepdims=True)) + lam_init)
    for g in range(n_groups):
        m, acc = state[g]
        on = acc[:DIFF_V_DIM] / acc[DIFF_V_DIM:DIFF_V_DIM + 1]
        o = on[:, :gq] - lam * on[:, gq:]
        ms = jnp.mean(o * o, axis=0, keepdims=True)
        o = o * lax.rsqrt(ms + NORM_EPS) * gd_ref[0] * (1.0 - lam_init)
        o_ref[g * gq:(g + 1) * gq, :] = o.T.astype(BF16)


def _diff_attention(l, lam_init, qt, kd, vt, ctx_k, ctx_v, lams, gd, nb, seq):
    t = kd.shape[0]
    h = DIFF_HEADS
    has_ctx = ctx_k is not None
    tq = min(ATTN_TQ, seq)
    tk = min(ATTN_TK, seq)
    nq = seq // tq
    w = 2 * DIFF_HEAD_DIM
    ins = [qt, kd, vt]
    in_specs = [pl.BlockSpec((1, w, tq), lambda b, hh, i: (b, hh, i)),
                pl.BlockSpec((seq, w), lambda b, hh, i: (b, hh)),
                pl.BlockSpec((1, DIFF_V_DIM, seq), lambda b, hh, i: (b, hh, 0))]
    if has_ctx:
        past = ctx_k.shape[2]
        ins += [ctx_k, ctx_v]
        in_specs += [pl.BlockSpec((1, 1, past, w), lambda b, hh, i: (b, l, 0, hh)),
                     pl.BlockSpec((1, 1, past, DIFF_V_DIM), lambda b, hh, i: (b, l, 0, hh))]
    lay3 = lambda b, hh, i: (l, 0, 0)
    ins += list(lams) + [gd]
    in_specs += [pl.BlockSpec((1, 1, DIFF_HEAD_DIM), lay3)] * 4 + [pl.BlockSpec((1, DIFF_V_DIM, 1), lay3)]
    return pl.pallas_call(
        functools.partial(_attn_kernel, lam_init, has_ctx, seq // tk, tk),
        grid=(nb, h, nq),
        in_specs=in_specs,
        out_specs=pl.BlockSpec((tq, DIFF_V_DIM), lambda b, hh, i: (b * nq + i, hh)),
        out_shape=jax.ShapeDtypeStruct((t, DIFF_WIDTH), BF16),
        compiler_params=_cparams(("arbitrary", "arbitrary", "arbitrary")),
        name="diff_attention",
    )(*ins)


def _top2_of4(v):
    m1 = jnp.maximum(jnp.maximum(v[0], v[1]), jnp.maximum(v[2], v[3]))
    i1 = jnp.where(v[0] == m1, 0, jnp.where(v[1] == m1, 1, jnp.where(v[2] == m1, 2, 3)))
    neg = jnp.full_like(m1, -jnp.inf)
    w = [jnp.where(i1 == j, neg, v[j]) for j in range(4)]
    m2 = jnp.maximum(jnp.maximum(w[0], w[1]), jnp.maximum(w[2], w[3]))
    i2 = jnp.where(w[0] == m2, 0, jnp.where(w[1] == m2, 1, jnp.where(w[2] == m2, 2, 3)))
    return m1, i1, m2, i2


def _post_kernel(og_ref, od_ref, x_ref, mod_ref, g2_ref, wo_ref, wr_ref, br_ref, x1_ref, hx_ref, route_ref):
    d = D_MODEL
    m = mod_ref[0]
    out = _dot(og_ref[...], wo_ref[0, 0:GLA_WIDTH, :]) + _dot(od_ref[...], wo_ref[0, GLA_WIDTH:, :])
    x1 = x_ref[...] + m[:, 2 * d:3 * d] * out
    x1_ref[...] = x1
    ms = jnp.mean(x1 * x1, axis=-1, keepdims=True)
    h2 = x1 * lax.rsqrt(ms + NORM_EPS) * (g2_ref[0] * (1.0 + m[:, 4 * d:5 * d])) + m[:, 3 * d:4 * d]
    h_hi, h_lo = _split_bf16(h2)
    hx_ref[:, 0:d] = h2
    w_hi, w_lo = _split_bf16(wr_ref[...])
    z = _dot_nt(w_hi, h_hi) + _dot_nt(w_lo, h_hi) + _dot_nt(w_hi, h_lo)
    s = _sigmoid(z)
    sel = s + br_ref[...]

    e = EXPERTS_PER_GROUP
    tops = []
    for g in range(N_GROUPS):
        tops.append(_top2_of4([sel[g * e + j:g * e + j + 1, :] for j in range(e)]))
    score = [t[0] + t[2] for t in tops]
    best = jnp.maximum(jnp.maximum(score[0], score[1]), jnp.maximum(score[2], score[3]))
    gi = jnp.where(score[0] == best, 0, jnp.where(score[1] == best, 1, jnp.where(score[2] == best, 2, 3)))

    def pick(rows):
        return jnp.where(gi == 0, rows[0], jnp.where(gi == 1, rows[1], jnp.where(gi == 2, rows[2], rows[3])))

    i1 = pick([t[1] for t in tops])
    i2 = pick([t[3] for t in tops])

    def gate_of(idx):
        per_group = []
        for g in range(N_GROUPS):
            rows = [s[g * e + j:g * e + j + 1, :] for j in range(e)]
            per_group.append(jnp.where(idx == 0, rows[0], jnp.where(idx == 1, rows[1],
                                                                     jnp.where(idx == 2, rows[2], rows[3]))))
        return pick(per_group)

    s1 = gate_of(i1)
    s2 = gate_of(i2)
    tot = s1 + s2
    w1 = s1 / tot
    w2 = s2 / tot
    lo = jnp.minimum(i1, i2)
    hi = jnp.maximum(i1, i2)
    w_lo = jnp.where(i1 < i2, w1, w2)
    w_hi = jnp.where(i1 < i2, w2, w1)
    pair = jnp.where(lo == 0, hi - 1, jnp.where(lo == 1, jnp.where(hi == 3, 3, 4), 5))
    swapped = pair == 5
    gate_a = jnp.where(swapped, w_hi, w_lo)
    gate_b = jnp.where(swapped, w_lo, w_hi)
    route_ref[...] = jnp.broadcast_to(gi * N_PAIRS + pair, route_ref.shape)
    r = lax.broadcasted_iota(jnp.int32, (GATE_COLS, gate_a.shape[1]), 0)
    gates_t = jnp.where(r == 0, gate_a, jnp.where(r == 1, gate_b, 0.0))
    hx_ref[:, d:d + GATE_COLS] = gates_t.T


def _post_mixer(l, og, od, x, mod, prep, seq, is_sample):
    t, d = x.shape
    tt = TOKEN_TILE
    tiles_per_batch = seq // tt
    if is_sample:
        mod_map = lambda i: (1 + i // tiles_per_batch, 0, 0)
    else:
        mod_map = lambda i: (0, 0, 0)
    row = lambda i: (i, 0)
    lay3 = lambda i: (l, 0, 0)
    return pl.pallas_call(
        _post_kernel,
        grid=(t // tt,),
        in_specs=[pl.BlockSpec((tt, GLA_WIDTH), row),
                  pl.BlockSpec((tt, DIFF_WIDTH), row),
                  pl.BlockSpec((tt, d), row),
                  pl.BlockSpec((1, 1, 6 * d), mod_map),
                  pl.BlockSpec((1, 1, d), lay3),
                  pl.BlockSpec((1, d, d), lay3),
                  pl.BlockSpec((N_EXPERTS, d), lambda i: (0, 0)),
                  pl.BlockSpec((N_EXPERTS, 1), lambda i: (0, 0))],
        out_specs=[pl.BlockSpec((tt, d), row),
                   pl.BlockSpec((tt, d + GATE_COLS), row),
                   pl.BlockSpec((8, tt), lambda i: (0, i))],
        out_shape=[jax.ShapeDtypeStruct((t, d), F32),
                   jax.ShapeDtypeStruct((t, d + GATE_COLS), F32),
                   jax.ShapeDtypeStruct((8, t), jnp.int32)],
        compiler_params=_cparams(("arbitrary",)),
        name="post_mixer",
    )(og, od, x, mod, prep["g2"], prep["w_out"], prep["w_router_t"], prep["b_router"])


def _moe_kernel(n_tiles, order_ref, first_ref, count_ref, ea_ref, eb_ref,
                hx_hbm, wga_ref, wua_ref, wda_ref, wgb_ref, wub_ref, wdb_ref,
                y_hbm, xbuf, ybuf, in_sem, out_sem):
    i = pl.program_id(0)
    slot = i % 2
    d = D_MODEL
    tm = xbuf.shape[1]
    n_tokens = hx_hbm.shape[0]

    def gather_copy(buf_slot, r, tok):
        return pltpu.make_async_copy(hx_hbm.at[pl.ds(tok, 1), :], xbuf.at[buf_slot, pl.ds(r, 1), :],
                                     in_sem.at[buf_slot])

    def scatter_copy(buf_slot, r, row):
        return pltpu.make_async_copy(ybuf.at[buf_slot, pl.ds(r, 1), :], y_hbm.at[pl.ds(row, 1), :],
                                     out_sem.at[buf_slot])

    def for_rows(fn):
        def body(q, c):
            for s in range(DMA_UNROLL):
                fn(q * DMA_UNROLL + s)
            return c
        lax.fori_loop(0, tm // DMA_UNROLL, body, 0)

    def start_gather(tile, buf_slot):
        @pl.when(count_ref[tile] > 0)
        def _():
            base = first_ref[tile]
            for_rows(lambda r: gather_copy(buf_slot, r, order_ref[base + r]).start())

    def wait_gather(tile, buf_slot):
        @pl.when(count_ref[tile] > 0)
        def _():
            for_rows(lambda r: gather_copy(buf_slot, r, 0).wait())

    def start_scatter(tile, buf_slot):
        @pl.when(count_ref[tile] > 0)
        def _():
            base = first_ref[tile]
            n = count_ref[tile]
            spare = n_tokens + buf_slot * tm
            for_rows(lambda r: scatter_copy(buf_slot, r,
                                            jnp.where(r < n, order_ref[base + r], spare + r)).start())

    def wait_scatter(tile, buf_slot):
        @pl.when(count_ref[tile] > 0)
        def _():
            for_rows(lambda r: scatter_copy(buf_slot, r, 0).wait())

    @pl.when(i == 0)
    def _():
        ybuf[...] = jnp.zeros_like(ybuf)
        for s in range(2):
            spare_init = pltpu.make_async_copy(ybuf.at[s], y_hbm.at[pl.ds(n_tokens + s * tm, tm), :], out_sem.at[s])
            spare_init.start()
            spare_init.wait()
        start_gather(0, 0)

    @pl.when(i + 1 < n_tiles)
    def _():
        start_gather(i + 1, 1 - slot)

    wait_gather(i, slot)

    @pl.when(i >= 2)
    def _():
        wait_scatter(i - 2, slot)

    @pl.when(count_ref[i] > 0)
    def _():
        xg = xbuf[slot]
        x = xg[:, 0:d].astype(BF16)

        def ffn(wg_ref, wu_ref, wd_ref):
            hg = _dot(x, wg_ref[0, 0])
            hu = _dot(x, wu_ref[0, 0])
            a = (hg * _sigmoid(hg)) * hu
            return _dot(a.astype(BF16), wd_ref[0, 0])

        ybuf[slot] = (xg[:, d:d + 1] * ffn(wga_ref, wua_ref, wda_ref)
                      + xg[:, d + 1:d + 2] * ffn(wgb_ref, wub_ref, wdb_ref))

    start_scatter(i, slot)

    @pl.when(i == n_tiles - 1)
    def _():
        if n_tiles > 1:
            wait_scatter(i - 1, 1 - slot)
        wait_scatter(i, slot)


def _moe(l, hx, route, prep):
    t = hx.shape[0]
    d = D_MODEL
    tm = MOE_TILE
    n_tiles = t // tm + N_COMBOS

    combo = route[0]
    order = jnp.argsort(combo).astype(jnp.int32)
    order_padded = jnp.concatenate([order, jnp.zeros((tm,), jnp.int32)])
    counts = jnp.sum(combo[:, None] == jnp.arange(N_COMBOS, dtype=jnp.int32)[None, :], axis=0).astype(jnp.int32)
    tiles_of = (counts + tm - 1) // tm
    tile_end = jnp.cumsum(tiles_of)
    start = jnp.cumsum(counts) - counts
    tile = jnp.arange(n_tiles, dtype=jnp.int32)
    n_used = tile_end[-1]
    tile_combo = jnp.sum(jnp.minimum(tile, n_used - 1)[:, None] >= tile_end[None, :], axis=1).astype(jnp.int32)
    onehot = (tile_combo[:, None] == jnp.arange(N_COMBOS, dtype=jnp.int32)[None, :]).astype(jnp.int32)
    pick = lambda table: jnp.sum(onehot * table[None, :], axis=1)
    within = (tile - (pick(tile_end) - pick(tiles_of))) * tm
    first = jnp.clip(pick(start) + within, 0, t - 1).astype(jnp.int32)
    count = jnp.where(tile < n_used, jnp.clip(pick(counts) - within, 0, tm), 0).astype(jnp.int32)
    grp = tile_combo // N_PAIRS
    pr = tile_combo % N_PAIRS
    pair_a = jnp.sum((pr[:, None] == jnp.arange(N_PAIRS)[None, :]) * jnp.asarray(PAIR_A, jnp.int32)[None, :], axis=1)
    pair_b = jnp.sum((pr[:, None] == jnp.arange(N_PAIRS)[None, :]) * jnp.asarray(PAIR_B, jnp.int32)[None, :], axis=1)
    ea = (grp * EXPERTS_PER_GROUP + pair_a).astype(jnp.int32)
    eb = (grp * EXPERTS_PER_GROUP + pair_b).astype(jnp.int32)

    wa = lambda i, o_r, f_r, c_r, ea_r, eb_r: (l, ea_r[i], 0, 0)
    wb = lambda i, o_r, f_r, c_r, ea_r, eb_r: (l, eb_r[i], 0, 0)
    up = pl.BlockSpec((1, 1, d, D_EXPERT), wa)
    dn = pl.BlockSpec((1, 1, D_EXPERT, d), wa)
    upb = pl.BlockSpec((1, 1, d, D_EXPERT), wb)
    dnb = pl.BlockSpec((1, 1, D_EXPERT, d), wb)
    any_spec = pl.BlockSpec(memory_space=pl.ANY)
    return pl.pallas_call(
        functools.partial(_moe_kernel, n_tiles),
        grid_spec=pltpu.PrefetchScalarGridSpec(
            num_scalar_prefetch=5,
            grid=(n_tiles,),
            in_specs=[any_spec, up, up, dn, upb, upb, dnb],
            out_specs=any_spec,
            scratch_shapes=[pltpu.VMEM((2, tm, d + GATE_COLS), F32), pltpu.VMEM((2, tm, d), F32),
                            pltpu.SemaphoreType.DMA((2,)), pltpu.SemaphoreType.DMA((2,))]),
        out_shape=jax.ShapeDtypeStruct((t + 2 * tm, d), F32),
        compiler_params=_cparams(("arbitrary",)),
        name="moe",
    )(order_padded, first, count, ea, eb, hx, prep["w_eg"], prep["w_eu"], prep["w_ed"],
      prep["w_eg"], prep["w_eu"], prep["w_ed"])


def _resid_kernel(x_ref, y_ref, mod_ref, o_ref):
    d = D_MODEL
    o_ref[...] = x_ref[...] + mod_ref[0][:, 5 * d:6 * d] * y_ref[...].astype(F32)


def _residual(x, y, mod, seq, is_sample):
    t, d = x.shape
    tt = TOKEN_TILE
    tiles_per_batch = seq // tt
    if is_sample:
        mod_map = lambda i: (1 + i // tiles_per_batch, 0, 0)
    else:
        mod_map = lambda i: (0, 0, 0)
    row = lambda i: (i, 0)
    return pl.pallas_call(
        _resid_kernel,
        grid=(t // tt,),
        in_specs=[pl.BlockSpec((tt, d), row), pl.BlockSpec((tt, d), row), pl.BlockSpec((1, 1, 6 * d), mod_map)],
        out_specs=pl.BlockSpec((tt, d), row),
        out_shape=jax.ShapeDtypeStruct((t, d), F32),
        compiler_params=_cparams(("arbitrary",)),
        name="residual",
    )(x, y, mod)


def _rope_tables(length):
    rows = length // GRID_W
    row = jnp.repeat(jnp.arange(rows, dtype=F32), GRID_W)
    col = jnp.tile(jnp.arange(GRID_W, dtype=F32), rows)
    inv = ROPE_BASE ** (-jnp.arange(0, ROPE_HALF, 2, dtype=F32) / ROPE_HALF)
    ang_r = row[:, None] * inv[None, :]
    ang_c = col[:, None] * inv[None, :]
    cos = jnp.concatenate([jnp.cos(ang_r), jnp.cos(ang_r), jnp.cos(ang_c), jnp.cos(ang_c)], axis=-1)
    sin = jnp.concatenate([-jnp.sin(ang_r), jnp.sin(ang_r), -jnp.sin(ang_c), jnp.sin(ang_c)], axis=-1)
    reps = DIFF_WIDTH // DIFF_HEAD_DIM
    return jnp.tile(cos, (1, reps)), jnp.tile(sin, (1, reps))


def _prepare(g_norm1, g_norm2, w_in, w_gate_fwd, b_gate_fwd, w_gate_bwd, b_gate_bwd, g_q_norm, g_k_norm,
             w_out, w_router, b_router, w_exp_gate, w_exp_up, w_exp_down):
    depth, d, _ = w_in.shape
    r = GLA_GATE_RANK
    gate_cols = jnp.pad(w_in[..., 2048:2048 + 2 * r], ((0, 0), (0, 0), (0, GATE_COLS - 2 * r)))
    w_in_r = jnp.concatenate([w_in[..., :2048], w_in[..., 2048 + 2 * r:], gate_cols], axis=-1).astype(BF16)
    w_gate = jnp.zeros((depth, GATE_COLS, 2 * GLA_WIDTH), F32)
    w_gate = w_gate.at[:, 0:r, 0:GLA_WIDTH].set(w_gate_fwd).at[:, r:2 * r, GLA_WIDTH:].set(w_gate_bwd).astype(BF16)
    b_gate = jnp.concatenate([b_gate_fwd, b_gate_bwd], axis=-1).reshape(depth, 1, 2 * GLA_WIDTH)
    grp = jnp.arange(DIFF_WIDTH) // DIFF_HEAD_DIM
    gmat = jnp.where(grp[:, None] == grp[None, :], 1.0 / DIFF_HEAD_DIM, 0.0).astype(BF16)
    reps = DIFF_WIDTH // DIFF_HEAD_DIM
    tok = jnp.arange(TOKEN_TILE)
    same_chunk = (tok[:, None] // GLA_CHUNK) == (tok[None, :] // GLA_CHUNK)
    tri = jnp.stack([same_chunk & (tok[:, None] >= tok[None, :]),
                     same_chunk & (tok[:, None] <= tok[None, :])]).astype(BF16)
    return {
        "tri": tri,
        "g1": g_norm1.reshape(depth, 1, d),
        "g2": g_norm2.reshape(depth, 1, d),
        "w_in": w_in_r,
        "w_gate": w_gate,
        "b_gate": b_gate,
        "gmat": gmat,
        "gq": jnp.tile(g_q_norm, (1, reps)).reshape(depth, 1, DIFF_WIDTH),
        "gk": jnp.tile(g_k_norm, (1, reps)).reshape(depth, 1, DIFF_WIDTH),
        "w_out": w_out.astype(BF16),
        "w_router_t": w_router.T,
        "b_router": b_router.reshape(N_EXPERTS, 1),
        "w_eg": w_exp_gate.astype(BF16),
        "w_eu": w_exp_up.astype(BF16),
        "w_ed": w_exp_down.astype(BF16),
    }


def kernel(x_prompt, x_sample, cache_k, cache_v, state_gla_fwd, state_gla_bwd, c, c_ctx, w_mod, b_mod, g_norm1,
           g_norm2, w_in, w_gate_fwd, b_gate_fwd, w_gate_bwd, b_gate_bwd, g_gla_out, g_q_norm, g_k_norm, lambda_q1,
           lambda_k1, lambda_q2, lambda_k2, g_diff_out, w_out, w_router, b_router, w_exp_gate, w_exp_up, w_exp_down):
    nb_p, seq_p, d = x_prompt.shape
    nb_s, seq_s, _ = x_sample.shape
    depth = w_in.shape[0]
    past = cache_k.shape[2]

    prep = _prepare(g_norm1, g_norm2, w_in, w_gate_fwd, b_gate_fwd, w_gate_bwd, b_gate_bwd, g_q_norm, g_k_norm,
                    w_out, w_router, b_router, w_exp_gate, w_exp_up, w_exp_down)
    rope = _rope_tables(seq_s)

    mod_rows = 8 * ((1 + nb_s + 7) // 8)
    cs = jnp.zeros((mod_rows, d), F32).at[0].set(c_ctx).at[1:1 + nb_s].set(c)
    mod_all = _modulation(cs, w_mod, b_mod)
    ctx_k = cache_k.reshape(nb_s, depth, past, DIFF_WIDTH)
    ctx_v = cache_v.reshape(nb_s, depth, past, DIFF_WIDTH)
    gg = g_gla_out.reshape(depth, 1, GLA_DV)
    gd = g_diff_out.reshape(depth, DIFF_V_DIM, 1)
    lams = [a.reshape(depth, 1, DIFF_HEAD_DIM) for a in (lambda_q1, lambda_k1, lambda_q2, lambda_k2)]

    xp = x_prompt.reshape(nb_p * seq_p, d)
    xs = x_sample.reshape(nb_s * seq_s, d)
    yp = ys = None
    mod_prev = None
    new_k, new_v, new_sf, new_sb = [], [], [], []
    for l in range(depth):
        lam_init = 0.8 - 0.6 * math.exp(-0.3 * l)
        mod = mod_all[l].reshape(mod_rows, 1, 6 * d)
        xp, gla_in, la, qt, kd, vt, k_l, v_l = _pre_mixer(l, xp, yp, mod_prev, mod, prep, None, nb_p, seq_p, False)
        og, sf_l, sb_l = _gla(l, gla_in, la, gg, None, None, nb_p, seq_p)
        od = _diff_attention(l, lam_init, qt, kd, vt, None, None, lams, gd, nb_p, seq_p)
        xp, h2, route = _post_mixer(l, og, od, xp, mod, prep, seq_p, False)
        yp = _moe(l, h2, route, prep)
        new_k.append(k_l)
        new_v.append(v_l)
        new_sf.append(sf_l)
        new_sb.append(sb_l)
        xs, gla_in, la, qt, kd, vt = _pre_mixer(l, xs, ys, mod_prev, mod, prep, rope, nb_s, seq_s, True)
        og, _, _ = _gla(l, gla_in, la, gg, state_gla_fwd, state_gla_bwd, nb_s, seq_s)
        od = _diff_attention(l, lam_init, qt, kd, vt, ctx_k, ctx_v, lams, gd, nb_s, seq_s)
        xs, h2, route = _post_mixer(l, og, od, xs, mod, prep, seq_s, True)
        ys = _moe(l, h2, route, prep)
        mod_prev = mod
    xp = _residual(xp, yp, mod_prev, seq_p, False)
    xs = _residual(xs, ys, mod_prev, seq_s, True)

    new_cache_k = jnp.concatenate(new_k, axis=1).reshape(nb_p, depth, seq_p, DIFF_HEADS, 2, DIFF_HEAD_DIM)
    new_cache_v = jnp.concatenate(new_v, axis=1).reshape(nb_p, depth, seq_p, DIFF_HEADS, DIFF_V_DIM)
    new_sf = jnp.stack(new_sf, axis=1)
    new_sb = jnp.stack(new_sb, axis=1)
    return (xp.reshape(nb_p, seq_p, d), xs.reshape(nb_s, seq_s, d), new_cache_k, new_cache_v, new_sf, new_sb)
```

```python
import functools
import math

import jax
import jax.numpy as jnp
from jax import lax
from jax.experimental import pallas as pl
from jax.experimental.pallas import tpu as pltpu

F32 = jnp.float32
BF16 = jnp.bfloat16

D_MODEL = 1024
GLA_HEADS = 4
GLA_DK = 128
GLA_DV = 128
GLA_WIDTH = GLA_HEADS * GLA_DK
GLA_GATE_RANK = 16
GLA_GATE_NORMALIZER = 16.0
GLA_CHUNK = 64
GLA_HEADS_PER_STEP = 4
DIFF_HEADS = 4
DIFF_HEAD_DIM = 64
DIFF_V_DIM = 128
DIFF_WIDTH = DIFF_HEADS * 2 * DIFF_HEAD_DIM
ROPE_HALF = DIFF_HEAD_DIM // 2
ROPE_BASE = 10000.0
GRID_W = 64
N_EXPERTS = 16
N_GROUPS = 4
EXPERTS_PER_GROUP = 4
D_EXPERT = 512
NORM_EPS = 1e-6
LOG2E = 1.4426950408889634

PAIR_A = (0, 0, 0, 1, 1, 3)
PAIR_B = (1, 2, 3, 3, 2, 2)
N_PAIRS = len(PAIR_A)
N_COMBOS = N_GROUPS * N_PAIRS

GATE_COLS = 128
MAIN_COLS = 7 * 512
TOKEN_TILE = 256
POST_TILE = 512
MOE_TILE = 256
DMA_UNROLL = 8
ATTN_TQ = 1024
ATTN_TK = 256
ATTN_GROUP = 128
ATTN_UNROLL = 8
ONES_ROWS = 16
VMEM_LIMIT = 56 * 1024 * 1024


def _cparams(sem):
    return pltpu.CompilerParams(dimension_semantics=sem, vmem_limit_bytes=VMEM_LIMIT)


def _dot(a, b):
    return jnp.dot(a, b, preferred_element_type=F32)


def _dot_nt(a, b):
    return lax.dot_general(a, b, (((1,), (1,)), ((), ())), preferred_element_type=F32)


def _dot_tn(a, b):
    return lax.dot_general(a, b, (((0,), (0,)), ((), ())), preferred_element_type=F32)


def _sigmoid(x):
    return 1.0 / (1.0 + jnp.exp(-x))


def _split_bf16(x):
    hi = x.astype(BF16)
    lo = (x - hi.astype(F32)).astype(BF16)
    return hi, lo


def _mod_kernel(c_ref, w_ref, b_ref, o_ref):
    c = c_ref[...]
    s = c * _sigmoid(c)
    o_ref[0] = jnp.dot(s, w_ref[0], preferred_element_type=F32,
                       precision=lax.Precision.HIGHEST) + b_ref[0]


def _modulation(cs, w_mod, b_mod):
    depth, d, n = w_mod.shape
    rows = cs.shape[0]
    tn = 1536
    return pl.pallas_call(
        _mod_kernel,
        grid=(depth, n // tn),
        in_specs=[pl.BlockSpec((rows, d), lambda l, j: (0, 0)),
                  pl.BlockSpec((1, d, tn), lambda l, j: (l, 0, j)),
                  pl.BlockSpec((1, 1, tn), lambda l, j: (l, 0, j))],
        out_specs=pl.BlockSpec((1, rows, tn), lambda l, j: (l, 0, j)),
        out_shape=jax.ShapeDtypeStruct((depth, rows, n), F32),
        compiler_params=_cparams(("arbitrary", "arbitrary")),
        name="modulation",
    )(cs, w_mod, b_mod.reshape(depth, 1, n))


def _group_rms(p, gmat_ref, g):
    ms = _dot((p * p).astype(BF16), gmat_ref[...])
    return p * lax.rsqrt(ms + NORM_EPS) * g


def _rope(y, cos, sin):
    w = y.shape[1]
    lane = lax.broadcasted_iota(jnp.int32, y.shape, 1)
    first = (lane % (2 * (ROPE_HALF // 2))) < (ROPE_HALF // 2)
    partner = jnp.where(first, pltpu.roll(y, w - ROPE_HALF // 2, axis=1), pltpu.roll(y, ROPE_HALF // 2, axis=1))
    return y * cos + partner * sin


def _pre_kernel(has_moe, is_sample, *refs):
    refs = list(refs)
    x_ref = refs.pop(0)
    if has_moe:
        y_ref, modp_ref = refs[:2]
        refs = refs[2:]
    mod_ref, g1_ref, win_ref, wgate_ref, bgate_ref, tri_ref, gmat_ref, gq_ref, gk_ref = refs[:9]
    refs = refs[9:]
    if is_sample:
        cos_ref, sin_ref = refs[:2]
        refs = refs[2:]
    xo_ref = refs.pop(0) if has_moe else None
    gla_ref, la_ref, qt_ref, kd_ref, vt_ref = refs[:5]
    refs = refs[5:]
    if not is_sample:
        kc_ref, vc_ref = refs

    d = D_MODEL
    m = mod_ref[0]
    x = x_ref[...]
    if has_moe:
        x = x + modp_ref[0][:, 5 * d:6 * d] * y_ref[...].astype(F32)
        xo_ref[...] = x
    ms = jnp.mean(x * x, axis=-1, keepdims=True)
    gs = g1_ref[0] * (1.0 + m[:, d:2 * d])
    hb = (x * lax.rsqrt(ms + NORM_EPS) * gs + m[:, 0:d]).astype(BF16)

    for j in range(4):
        p = _dot(hb, win_ref[0, :, j * 512:(j + 1) * 512])
        if j == 0:
            p = p * (GLA_DK ** -0.5)
        gla_ref[:, j * 512:(j + 1) * 512] = p.astype(BF16)

    a = _dot(hb, win_ref[0, :, MAIN_COLS:MAIN_COLS + GATE_COLS])
    z = _dot(a.astype(BF16), wgate_ref[0]) + bgate_ref[0]
    la = (jnp.minimum(z, 0.0) - jnp.log1p(jnp.exp(-jnp.abs(z)))) * (1.0 / GLA_GATE_NORMALIZER)
    la_hi, la_lo = _split_bf16(la)
    w = GLA_WIDTH
    la_ref[:, 0:w] = _dot(tri_ref[0], la_hi[:, 0:w]) + _dot(tri_ref[0], la_lo[:, 0:w])
    la_ref[:, w:2 * w] = _dot(tri_ref[1], la_hi[:, w:2 * w]) + _dot(tri_ref[1], la_lo[:, w:2 * w])

    scale = DIFF_HEAD_DIM ** -0.5 * LOG2E
    q = _group_rms(_dot(hb, win_ref[0, :, 2048:2560]), gmat_ref, gq_ref[0])
    k = _group_rms(_dot(hb, win_ref[0, :, 2560:3072]), gmat_ref, gk_ref[0])
    v = _dot(hb, win_ref[0, :, 3072:3584])
    if is_sample:
        cos = cos_ref[...]
        sin = sin_ref[...]
        q = _rope(q, cos, sin)
        k = _rope(k, cos, sin)
    else:
        kc_ref[0, 0] = k
        vc_ref[0, 0] = v
    qt_ref[0] = (q * scale).astype(BF16).T
    kd_ref[...] = k.astype(BF16)
    vt_ref[0] = v.astype(BF16).T


def _pre_mixer(l, x, y, mod_prev, mod, prep, rope, nb, seq, is_sample):
    t, d = x.shape
    tt = TOKEN_TILE
    tiles_per_batch = seq // tt
    has_moe = y is not None
    if is_sample:
        mod_map = lambda i: (1 + i // tiles_per_batch, 0, 0)
    else:
        mod_map = lambda i: (0, 0, 0)
    row = lambda i: (i, 0)
    const2 = lambda i: (0, 0)
    lay3 = lambda i: (l, 0, 0)

    ins = [x]
    in_specs = [pl.BlockSpec((tt, d), row)]
    if has_moe:
        ins += [y, mod_prev]
        in_specs += [pl.BlockSpec((tt, d), row), pl.BlockSpec((1, 1, 6 * d), mod_map)]
    ins += [mod, prep["g1"], prep["w_in"], prep["w_gate"], prep["b_gate"], prep["tri"], prep["gmat"], prep["gq"],
            prep["gk"]]
    in_specs += [pl.BlockSpec((1, 1, 6 * d), mod_map),
                 pl.BlockSpec((1, 1, d), lay3),
                 pl.BlockSpec((1, d, MAIN_COLS + GATE_COLS), lay3),
                 pl.BlockSpec((1, GATE_COLS, 2 * GLA_WIDTH), lay3),
                 pl.BlockSpec((1, 1, 2 * GLA_WIDTH), lay3),
                 pl.BlockSpec((2, tt, tt), lambda i: (0, 0, 0)),
                 pl.BlockSpec((DIFF_WIDTH, DIFF_WIDTH), const2),
                 pl.BlockSpec((1, 1, DIFF_WIDTH), lay3),
                 pl.BlockSpec((1, 1, DIFF_WIDTH), lay3)]
    if is_sample:
        ins += [rope[0], rope[1]]
        in_specs += [pl.BlockSpec((tt, DIFF_WIDTH), lambda i: (i % tiles_per_batch, 0))] * 2

    out_shape = []
    out_specs = []
    if has_moe:
        out_shape.append(jax.ShapeDtypeStruct((t, d), F32))
        out_specs.append(pl.BlockSpec((tt, d), row))
    tr = lambda i: (i // tiles_per_batch, 0, i % tiles_per_batch)
    out_shape += [jax.ShapeDtypeStruct((t, 4 * GLA_WIDTH), BF16),
                  jax.ShapeDtypeStruct((t, 2 * GLA_WIDTH), F32),
                  jax.ShapeDtypeStruct((nb, DIFF_WIDTH, seq), BF16),
                  jax.ShapeDtypeStruct((t, DIFF_WIDTH), BF16),
                  jax.ShapeDtypeStruct((nb, DIFF_WIDTH, seq), BF16)]
    out_specs += [pl.BlockSpec((tt, 4 * GLA_WIDTH), row),
                  pl.BlockSpec((tt, 2 * GLA_WIDTH), row),
                  pl.BlockSpec((1, DIFF_WIDTH, tt), tr),
                  pl.BlockSpec((tt, DIFF_WIDTH), row),
                  pl.BlockSpec((1, DIFF_WIDTH, tt), tr)]
    if not is_sample:
        assert seq == tt
        out_shape += [jax.ShapeDtypeStruct((nb, 1, seq, DIFF_WIDTH), F32)] * 2
        out_specs += [pl.BlockSpec((1, 1, seq, DIFF_WIDTH), lambda i: (i, 0, 0, 0))] * 2

    outs = pl.pallas_call(
        functools.partial(_pre_kernel, has_moe, is_sample),
        grid=(t // tt,),
        in_specs=in_specs,
        out_specs=out_specs,
        out_shape=out_shape,
        compiler_params=_cparams(("arbitrary",)),
        name="pre_mixer",
    )(*ins)
    outs = list(outs)
    x_new = outs.pop(0) if has_moe else x
    return [x_new] + outs


def _gla_kernel(has_state, n_chunks, hg, *refs):
    refs = list(refs)
    q_ref, k_ref, v_ref, g_ref, bf_ref, bb_ref, gg_ref = refs[:7]
    refs = refs[7:]
    if has_state:
        s0f_ref, s0b_ref = refs[:2]
        refs = refs[2:]
    o_ref, sf_ref, sb_ref, st_ref, acc_ref = refs

    c = GLA_CHUNK
    dk = GLA_DK
    for hd in range(hg):
        if has_state:
            st_ref[2 * hd] = s0f_ref[0, 0, hd].T
            st_ref[2 * hd + 1] = s0b_ref[0, 0, hd].T
        else:
            st_ref[2 * hd] = jnp.zeros((GLA_DV, dk), F32)
            st_ref[2 * hd + 1] = jnp.zeros((GLA_DV, dk), F32)

    r = lax.broadcasted_iota(jnp.int32, (c, c), 0)
    s = lax.broadcasted_iota(jnp.int32, (c, c), 1)
    lower = r >= s
    upper = r <= s

    def scores(rows, hd, b_ref, st_i, mid_row, last_row):
        cs = slice(hd * dk, (hd + 1) * dk)
        q = q_ref[rows, cs].astype(F32)
        k = k_ref[rows, cs].astype(F32)
        v = v_ref[rows, cs]
        b = b_ref[rows, cs]
        mid = b[mid_row:mid_row + 1]
        last = b[last_row:last_row + 1]
        qe = (q * jnp.exp(b - mid)).astype(BF16)
        ke = (k * jnp.exp(mid - b)).astype(BF16)
        qi = (q * jnp.exp(b)).astype(BF16)
        ks = (k * jnp.exp(last - b)).astype(BF16)
        st = st_ref[st_i]
        return _dot_nt(qe, ke), _dot_nt(qi, st.astype(BF16)), _dot_tn(v, ks), v, st, last

    def outputs(sc, mask, st_i):
        att, o_inter, kv, v, st, last = sc
        st_ref[st_i] = st * jnp.exp(last) + kv
        return _dot(jnp.where(mask, att, 0.0).astype(BF16), v) + o_inter

    def finish(o, rows, hd):
        cs = slice(hd * dk, (hd + 1) * dk)
        ms = jnp.mean(o * o, axis=-1, keepdims=True)
        g = g_ref[rows, cs].astype(F32)
        o_ref[rows, cs] = (o * lax.rsqrt(ms + NORM_EPS) * gg_ref[0] * (g * _sigmoid(g))).astype(BF16)

    def step(n, second_visit):
        rf = pl.ds(pl.multiple_of(n * c, c), c)
        rb = pl.ds(pl.multiple_of((n_chunks - 1 - n) * c, c), c)
        sc = []
        for hd in range(hg):
            sc.append(scores(rf, hd, bf_ref, 2 * hd, c // 2, c - 1))
            sc.append(scores(rb, hd, bb_ref, 2 * hd + 1, c - 1 - c // 2, 0))
        for hd in range(hg):
            cs = slice(hd * dk, (hd + 1) * dk)
            for rows, sci, mask, st_i in ((rf, sc[2 * hd], lower, 2 * hd), (rb, sc[2 * hd + 1], upper, 2 * hd + 1)):
                o = outputs(sci, mask, st_i)
                if second_visit:
                    finish(acc_ref[rows, cs] + o, rows, hd)
                else:
                    acc_ref[rows, cs] = o

    half = n_chunks // 2
    lax.fori_loop(0, half, lambda n, carry: (step(n, False), carry)[1], 0)
    lax.fori_loop(half, n_chunks, lambda n, carry: (step(n, True), carry)[1], 0)

    for hd in range(hg):
        sf_ref[0, hd] = st_ref[2 * hd].T
        sb_ref[0, hd] = st_ref[2 * hd + 1].T


def _gla(l, gla_in, bsum, gg, s0f, s0b, nb, seq):
    t = gla_in.shape[0]
    h = GLA_HEADS
    hg = GLA_HEADS_PER_STEP
    nhb = h // hg
    w = hg * GLA_DK
    has_state = s0f is not None
    n_chunks = seq // GLA_CHUNK
    assert n_chunks % 2 == 0
    col = lambda off: (lambda b, hh: (b, off + hh))
    in_bytes = seq * w * (4 * 2 + 2 * 4)
    mode = pl.Buffered(1) if 2 * in_bytes > VMEM_LIMIT // 2 else None
    big = lambda off: pl.BlockSpec((seq, w), col(off), pipeline_mode=mode)
    ins = [gla_in, gla_in, gla_in, gla_in, bsum, bsum, gg]
    in_specs = [big(0), big(nhb), big(2 * nhb), big(3 * nhb), big(0), big(nhb),
                pl.BlockSpec((1, 1, GLA_DV), lambda b, hh: (l, 0, 0))]
    if has_state:
        ins += [s0f, s0b]
        in_specs += [pl.BlockSpec((1, 1, hg, GLA_DK, GLA_DV), lambda b, hh: (b, l, hh, 0, 0))] * 2
    return pl.pallas_call(
        functools.partial(_gla_kernel, has_state, n_chunks, hg),
        grid=(nb, nhb),
        in_specs=in_specs,
        out_specs=[pl.BlockSpec((seq, w), col(0)),
                   pl.BlockSpec((1, hg, GLA_DK, GLA_DV), lambda b, hh: (b, hh, 0, 0)),
                   pl.BlockSpec((1, hg, GLA_DK, GLA_DV), lambda b, hh: (b, hh, 0, 0))],
        out_shape=[jax.ShapeDtypeStruct((t, GLA_WIDTH), BF16),
                   jax.ShapeDtypeStruct((nb, h, GLA_DK, GLA_DV), F32),
                   jax.ShapeDtypeStruct((nb, h, GLA_DK, GLA_DV), F32)],
        scratch_shapes=[pltpu.VMEM((2 * hg, GLA_DV, GLA_DK), F32), pltpu.VMEM((seq, w), F32)],
        compiler_params=_cparams(("arbitrary", "arbitrary")),
        name="gla",
    )(*ins)


def _attn_kernel(lam_init, has_ctx, n_kt, tk, *refs):
    refs = list(refs)
    qt_ref, k_ref, vt_ref = refs[:3]
    refs = refs[3:]
    if has_ctx:
        ck_ref, cv_ref = refs[:2]
        refs = refs[2:]
    lq1_ref, lk1_ref, lq2_ref, lk2_ref, gd_ref, o_ref = refs

    qt = qt_ref[0]
    tq = qt.shape[1]
    gq = ATTN_GROUP
    n_groups = tq // gq
    dim = lax.broadcasted_iota(jnp.int32, (2 * DIFF_HEAD_DIM, gq), 0)
    zero = jnp.zeros((2 * DIFF_HEAD_DIM, gq), BF16)
    qws = []
    for g in range(n_groups):
        qg = qt[:, g * gq:(g + 1) * gq]
        qws.append(jnp.concatenate([jnp.where(dim < DIFF_HEAD_DIM, qg, zero),
                                    jnp.where(dim >= DIFF_HEAD_DIM, qg, zero)], axis=1))

    def scores(k):
        return [_dot(k, qws[g]) for g in range(n_groups)]

    def absorb(state, sts, vt):
        new = []
        for g in range(n_groups):
            m, acc = state[g]
            st = sts[g]
            m_new = jnp.maximum(m, jnp.max(st, axis=0, keepdims=True))
            alpha = jnp.exp2(m - m_new)
            p = jnp.exp2(st - m_new)
            acc = alpha * acc + _dot(vt, p.astype(BF16))
            new.append((m_new, acc))
        return tuple(new)

    def process(state, tiles):
        sts = scores(tiles[0][0])
        for u, (_, vt) in enumerate(tiles):
            nxt = scores(tiles[u + 1][0]) if u + 1 < len(tiles) else None
            state = absorb(state, sts, vt)
            sts = nxt
        return state

    def with_ones(vt):
        return jnp.concatenate([vt, jnp.ones((ONES_ROWS, vt.shape[1]), BF16)], axis=0)

    state = tuple((jnp.full((1, 2 * gq), -jnp.inf, F32), jnp.zeros((DIFF_V_DIM + ONES_ROWS, 2 * gq), F32))
                  for _ in range(n_groups))
    if has_ctx:
        past = ck_ref.shape[2]
        ctk = tk if past % tk == 0 else past
        cvt = cv_ref[0, 0].T.astype(BF16)
        tiles = [(ck_ref[0, 0, j * ctk:(j + 1) * ctk, :].astype(BF16), with_ones(cvt[:, j * ctk:(j + 1) * ctk]))
                 for j in range(past // ctk)]
        state = process(state, tiles)

    unroll = ATTN_UNROLL if n_kt % ATTN_UNROLL == 0 else 1

    def body(j, st):
        tiles = []
        for u in range(unroll):
            rr = pl.ds(pl.multiple_of((j * unroll + u) * tk, tk), tk)
            tiles.append((k_ref[rr, :], with_ones(vt_ref[0, :, rr])))
        return process(st, tiles)

    state = lax.fori_loop(0, n_kt // unroll, body, state)

    lam = (jnp.exp(jnp.sum(lq1_ref[0] * lk1_ref[0], axis=-1, keepdims=True))
           - jnp.exp(jnp.sum(lq2_ref[0] * lk2_ref[0], axis=-1, keepdims=True)) + lam_init)
    for g in range(n_groups):
        m, acc = state[g]
        on = acc[:DIFF_V_DIM] / acc[DIFF_V_DIM:DIFF_V_DIM + 1]
        o = on[:, :gq] - lam * on[:, gq:]
        ms = jnp.mean(o * o, axis=0, keepdims=True)
        o = o * lax.rsqrt(ms + NORM_EPS) * gd_ref[0] * (1.0 - lam_init)
        o_ref[g * gq:(g + 1) * gq, :] = o.T.astype(BF16)


def _diff_attention(l, lam_init, qt, kd, vt, ctx_k, ctx_v, lams, gd, nb, seq):
    t = kd.shape[0]
    h = DIFF_HEADS
    has_ctx = ctx_k is not None
    tq = min(ATTN_TQ, seq)
    tk = min(ATTN_TK, seq)
    nq = seq // tq
    w = 2 * DIFF_HEAD_DIM
    ins = [qt, kd, vt]
    in_specs = [pl.BlockSpec((1, w, tq), lambda b, hh, i: (b, hh, i)),
                pl.BlockSpec((seq, w), lambda b, hh, i: (b, hh)),
                pl.BlockSpec((1, DIFF_V_DIM, seq), lambda b, hh, i: (b, hh, 0))]
    if has_ctx:
        past = ctx_k.shape[2]
        ins += [ctx_k, ctx_v]
        in_specs += [pl.BlockSpec((1, 1, past, w), lambda b, hh, i: (b, l, 0, hh)),
                     pl.BlockSpec((1, 1, past, DIFF_V_DIM), lambda b, hh, i: (b, l, 0, hh))]
    lay3 = lambda b, hh, i: (l, 0, 0)
    ins += list(lams) + [gd]
    in_specs += [pl.BlockSpec((1, 1, DIFF_HEAD_DIM), lay3)] * 4 + [pl.BlockSpec((1, DIFF_V_DIM, 1), lay3)]
    return pl.pallas_call(
        functools.partial(_attn_kernel, lam_init, has_ctx, seq // tk, tk),
        grid=(nb, h, nq),
        in_specs=in_specs,
        out_specs=pl.BlockSpec((tq, DIFF_V_DIM), lambda b, hh, i: (b * nq + i, hh)),
        out_shape=jax.ShapeDtypeStruct((t, DIFF_WIDTH), BF16),
        compiler_params=_cparams(("arbitrary", "arbitrary", "arbitrary")),
        name="diff_attention",
    )(*ins)


def _top2_of4(v):
    m1 = jnp.maximum(jnp.maximum(v[0], v[1]), jnp.maximum(v[2], v[3]))
    i1 = jnp.where(v[0] == m1, 0, jnp.where(v[1] == m1, 1, jnp.where(v[2] == m1, 2, 3)))
    neg = jnp.full_like(m1, -jnp.inf)
    w = [jnp.where(i1 == j, neg, v[j]) for j in range(4)]
    m2 = jnp.maximum(jnp.maximum(w[0], w[1]), jnp.maximum(w[2], w[3]))
    i2 = jnp.where(w[0] == m2, 0, jnp.where(w[1] == m2, 1, jnp.where(w[2] == m2, 2, 3)))
    return m1, i1, m2, i2


def _post_kernel(og_ref, od_ref, x_ref, mod_ref, g2_ref, wo_ref, wr_ref, br_ref, x1_ref, hx_ref, route_ref):
    d = D_MODEL
    sub = TOKEN_TILE
    n_sub = x_ref.shape[0] // sub
    m = mod_ref[0]
    rows = [slice(j * sub, (j + 1) * sub) for j in range(n_sub)]
    outs = [_dot(og_ref[rs, :], wo_ref[0, 0:GLA_WIDTH, :]) + _dot(od_ref[rs, :], wo_ref[0, GLA_WIDTH:, :])
            for rs in rows]
    w_hi, w_lo = _split_bf16(wr_ref[...])
    w_hl = jnp.concatenate([w_hi, w_lo], axis=0)
    zs = []
    for rs, out in zip(rows, outs):
        x1 = x_ref[rs, :] + m[:, 2 * d:3 * d] * out
        x1_ref[rs, :] = x1
        ms = jnp.mean(x1 * x1, axis=-1, keepdims=True)
        h2 = x1 * lax.rsqrt(ms + NORM_EPS) * (g2_ref[0] * (1.0 + m[:, 4 * d:5 * d])) + m[:, 3 * d:4 * d]
        h_hi, h_lo = _split_bf16(h2)
        hx_ref[rs, 0:d] = h2
        za = _dot_nt(w_hl, h_hi)
        zs.append(za[0:N_EXPERTS] + za[N_EXPERTS:2 * N_EXPERTS] + _dot_nt(w_hi, h_lo))
    for rs, z in zip(rows, zs):
        _route(z, br_ref, hx_ref, route_ref, rs)


def _route(z, br_ref, hx_ref, route_ref, rs):
    d = D_MODEL
    s = _sigmoid(z)
    sel = s + br_ref[...]

    e = EXPERTS_PER_GROUP
    tops = []
    for g in range(N_GROUPS):
        tops.append(_top2_of4([sel[g * e + j:g * e + j + 1, :] for j in range(e)]))
    score = [t[0] + t[2] for t in tops]
    best = jnp.maximum(jnp.maximum(score[0], score[1]), jnp.maximum(score[2], score[3]))
    gi = jnp.where(score[0] == best, 0, jnp.where(score[1] == best, 1, jnp.where(score[2] == best, 2, 3)))

    def pick(rows):
        return jnp.where(gi == 0, rows[0], jnp.where(gi == 1, rows[1], jnp.where(gi == 2, rows[2], rows[3])))

    i1 = pick([t[1] for t in tops])
    i2 = pick([t[3] for t in tops])

    def gate_of(idx):
        per_group = []
        for g in range(N_GROUPS):
            rows = [s[g * e + j:g * e + j + 1, :] for j in range(e)]
            per_group.append(jnp.where(idx == 0, rows[0], jnp.where(idx == 1, rows[1],
                                                                     jnp.where(idx == 2, rows[2], rows[3]))))
        return pick(per_group)

    s1 = gate_of(i1)
    s2 = gate_of(i2)
    tot = s1 + s2
    w1 = s1 / tot
    w2 = s2 / tot
    lo = jnp.minimum(i1, i2)
    hi = jnp.maximum(i1, i2)
    w_lo = jnp.where(i1 < i2, w1, w2)
    w_hi = jnp.where(i1 < i2, w2, w1)
    pair = jnp.where(lo == 0, hi - 1, jnp.where(lo == 1, jnp.where(hi == 3, 3, 4), 5))
    swapped = pair == 5
    gate_a = jnp.where(swapped, w_hi, w_lo)
    gate_b = jnp.where(swapped, w_lo, w_hi)
    route_ref[:, rs] = jnp.broadcast_to(gi * N_PAIRS + pair, (route_ref.shape[0], gate_a.shape[1]))
    r = lax.broadcasted_iota(jnp.int32, (GATE_COLS, gate_a.shape[1]), 0)
    gates_t = jnp.where(r == 0, gate_a, jnp.where(r == 1, gate_b, 0.0))
    hx_ref[rs, d:d + GATE_COLS] = gates_t.T


def _post_mixer(l, og, od, x, mod, prep, seq, is_sample):
    t, d = x.shape
    tt = POST_TILE if (seq % POST_TILE == 0 or not is_sample) and t % POST_TILE == 0 else TOKEN_TILE
    tiles_per_batch = max(seq // tt, 1)
    if is_sample:
        mod_map = lambda i: (1 + i // tiles_per_batch, 0, 0)
    else:
        mod_map = lambda i: (0, 0, 0)
    row = lambda i: (i, 0)
    lay3 = lambda i: (l, 0, 0)
    return pl.pallas_call(
        _post_kernel,
        grid=(t // tt,),
        in_specs=[pl.BlockSpec((tt, GLA_WIDTH), row),
                  pl.BlockSpec((tt, DIFF_WIDTH), row),
                  pl.BlockSpec((tt, d), row),
                  pl.BlockSpec((1, 1, 6 * d), mod_map),
                  pl.BlockSpec((1, 1, d), lay3),
                  pl.BlockSpec((1, d, d), lay3),
                  pl.BlockSpec((N_EXPERTS, d), lambda i: (0, 0)),
                  pl.BlockSpec((N_EXPERTS, 1), lambda i: (0, 0))],
        out_specs=[pl.BlockSpec((tt, d), row),
                   pl.BlockSpec((tt, d + GATE_COLS), row),
                   pl.BlockSpec((8, tt), lambda i: (0, i))],
        out_shape=[jax.ShapeDtypeStruct((t, d), F32),
                   jax.ShapeDtypeStruct((t, d + GATE_COLS), F32),
                   jax.ShapeDtypeStruct((8, t), jnp.int32)],
        compiler_params=_cparams(("arbitrary",)),
        name="post_mixer",
    )(og, od, x, mod, prep["g2"], prep["w_out"], prep["w_router_t"], prep["b_router"])


def _moe_kernel(n_tiles, order_ref, first_ref, count_ref, ea_ref, eb_ref,
                hx_hbm, wga_ref, wua_ref, wda_ref, wgb_ref, wub_ref, wdb_ref,
                y_hbm, xbuf, ybuf, in_sem, out_sem):
    i = pl.program_id(0)
    slot = i % 2
    d = D_MODEL
    tm = xbuf.shape[1]
    n_tokens = hx_hbm.shape[0]

    def gather_copy(buf_slot, r, tok):
        return pltpu.make_async_copy(hx_hbm.at[pl.ds(tok, 1), :], xbuf.at[buf_slot, pl.ds(r, 1), :],
                                     in_sem.at[buf_slot])

    def scatter_copy(buf_slot, r, row):
        return pltpu.make_async_copy(ybuf.at[buf_slot, pl.ds(r, 1), :], y_hbm.at[pl.ds(row, 1), :],
                                     out_sem.at[buf_slot])

    def for_rows(fn):
        def body(q, c):
            for s in range(DMA_UNROLL):
                fn(q * DMA_UNROLL + s)
            return c
        lax.fori_loop(0, tm // DMA_UNROLL, body, 0)

    def start_gather(tile, buf_slot):
        @pl.when(count_ref[tile] > 0)
        def _():
            base = first_ref[tile]
            for_rows(lambda r: gather_copy(buf_slot, r, order_ref[base + r]).start())

    def wait_gather(tile, buf_slot):
        @pl.when(count_ref[tile] > 0)
        def _():
            for_rows(lambda r: gather_copy(buf_slot, r, 0).wait())

    def start_scatter(tile, buf_slot):
        @pl.when(count_ref[tile] > 0)
        def _():
            base = first_ref[tile]
            n = count_ref[tile]
            spare = n_tokens + buf_slot * tm
            for_rows(lambda r: scatter_copy(buf_slot, r,
                                            jnp.where(r < n, order_ref[base + r], spare + r)).start())

    def wait_scatter(tile, buf_slot):
        @pl.when(count_ref[tile] > 0)
        def _():
            for_rows(lambda r: scatter_copy(buf_slot, r, 0).wait())

    @pl.when(i == 0)
    def _():
        ybuf[...] = jnp.zeros_like(ybuf)
        for s in range(2):
            spare_init = pltpu.make_async_copy(ybuf.at[s], y_hbm.at[pl.ds(n_tokens + s * tm, tm), :], out_sem.at[s])
            spare_init.start()
            spare_init.wait()
        start_gather(0, 0)

    @pl.when(i + 1 < n_tiles)
    def _():
        start_gather(i + 1, 1 - slot)

    wait_gather(i, slot)

    @pl.when(i >= 2)
    def _():
        wait_scatter(i - 2, slot)

    @pl.when(count_ref[i] > 0)
    def _():
        xg = xbuf[slot]
        x = xg[:, 0:d].astype(BF16)

        def ffn(wg_ref, wu_ref, wd_ref):
            hg = _dot(x, wg_ref[0, 0])
            hu = _dot(x, wu_ref[0, 0])
            a = (hg * _sigmoid(hg)) * hu
            return _dot(a.astype(BF16), wd_ref[0, 0])

        ybuf[slot] = (xg[:, d:d + 1] * ffn(wga_ref, wua_ref, wda_ref)
                      + xg[:, d + 1:d + 2] * ffn(wgb_ref, wub_ref, wdb_ref))

    start_scatter(i, slot)

    @pl.when(i == n_tiles - 1)
    def _():
        if n_tiles > 1:
            wait_scatter(i - 1, 1 - slot)
        wait_scatter(i, slot)


def _moe(l, hx, route, prep):
    t = hx.shape[0]
    d = D_MODEL
    tm = MOE_TILE
    n_tiles = t // tm + N_COMBOS

    combo = route[0]
    order = jnp.argsort(combo).astype(jnp.int32)
    order_padded = jnp.concatenate([order, jnp.zeros((tm,), jnp.int32)])
    counts = jnp.sum(combo[:, None] == jnp.arange(N_COMBOS, dtype=jnp.int32)[None, :], axis=0).astype(jnp.int32)
    tiles_of = (counts + tm - 1) // tm
    tile_end = jnp.cumsum(tiles_of)
    start = jnp.cumsum(counts) - counts
    tile = jnp.arange(n_tiles, dtype=jnp.int32)
    n_used = tile_end[-1]
    tile_combo = jnp.sum(jnp.minimum(tile, n_used - 1)[:, None] >= tile_end[None, :], axis=1).astype(jnp.int32)
    onehot = (tile_combo[:, None] == jnp.arange(N_COMBOS, dtype=jnp.int32)[None, :]).astype(jnp.int32)
    pick = lambda table: jnp.sum(onehot * table[None, :], axis=1)
    within = (tile - (pick(tile_end) - pick(tiles_of))) * tm
    first = jnp.clip(pick(start) + within, 0, t - 1).astype(jnp.int32)
    count = jnp.where(tile < n_used, jnp.clip(pick(counts) - within, 0, tm), 0).astype(jnp.int32)
    grp = tile_combo // N_PAIRS
    pr = tile_combo % N_PAIRS
    pair_a = jnp.sum((pr[:, None] == jnp.arange(N_PAIRS)[None, :]) * jnp.asarray(PAIR_A, jnp.int32)[None, :], axis=1)
    pair_b = jnp.sum((pr[:, None] == jnp.arange(N_PAIRS)[None, :]) * jnp.asarray(PAIR_B, jnp.int32)[None, :], axis=1)
    ea = (grp * EXPERTS_PER_GROUP + pair_a).astype(jnp.int32)
    eb = (grp * EXPERTS_PER_GROUP + pair_b).astype(jnp.int32)

    wa = lambda i, o_r, f_r, c_r, ea_r, eb_r: (l, ea_r[i], 0, 0)
    wb = lambda i, o_r, f_r, c_r, ea_r, eb_r: (l, eb_r[i], 0, 0)
    up = pl.BlockSpec((1, 1, d, D_EXPERT), wa)
    dn = pl.BlockSpec((1, 1, D_EXPERT, d), wa)
    upb = pl.BlockSpec((1, 1, d, D_EXPERT), wb)
    dnb = pl.BlockSpec((1, 1, D_EXPERT, d), wb)
    any_spec = pl.BlockSpec(memory_space=pl.ANY)
    return pl.pallas_call(
        functools.partial(_moe_kernel, n_tiles),
        grid_spec=pltpu.PrefetchScalarGridSpec(
            num_scalar_prefetch=5,
            grid=(n_tiles,),
            in_specs=[any_spec, up, up, dn, upb, upb, dnb],
            out_specs=any_spec,
            scratch_shapes=[pltpu.VMEM((2, tm, d + GATE_COLS), F32), pltpu.VMEM((2, tm, d), F32),
                            pltpu.SemaphoreType.DMA((2,)), pltpu.SemaphoreType.DMA((2,))]),
        out_shape=jax.ShapeDtypeStruct((t + 2 * tm, d), F32),
        compiler_params=_cparams(("arbitrary",)),
        name="moe",
    )(order_padded, first, count, ea, eb, hx, prep["w_eg"], prep["w_eu"], prep["w_ed"],
      prep["w_eg"], prep["w_eu"], prep["w_ed"])


def _resid_kernel(x_ref, y_ref, mod_ref, o_ref):
    d = D_MODEL
    o_ref[...] = x_ref[...] + mod_ref[0][:, 5 * d:6 * d] * y_ref[...].astype(F32)


def _residual(x, y, mod, seq, is_sample):
    t, d = x.shape
    tt = TOKEN_TILE
    tiles_per_batch = seq // tt
    if is_sample:
        mod_map = lambda i: (1 + i // tiles_per_batch, 0, 0)
    else:
        mod_map = lambda i: (0, 0, 0)
    row = lambda i: (i, 0)
    return pl.pallas_call(
        _resid_kernel,
        grid=(t // tt,),
        in_specs=[pl.BlockSpec((tt, d), row), pl.BlockSpec((tt, d), row), pl.BlockSpec((1, 1, 6 * d), mod_map)],
        out_specs=pl.BlockSpec((tt, d), row),
        out_shape=jax.ShapeDtypeStruct((t, d), F32),
        compiler_params=_cparams(("arbitrary",)),
        name="residual",
    )(x, y, mod)


def _rope_tables(length):
    rows = length // GRID_W
    row = jnp.repeat(jnp.arange(rows, dtype=F32), GRID_W)
    col = jnp.tile(jnp.arange(GRID_W, dtype=F32), rows)
    inv = ROPE_BASE ** (-jnp.arange(0, ROPE_HALF, 2, dtype=F32) / ROPE_HALF)
    ang_r = row[:, None] * inv[None, :]
    ang_c = col[:, None] * inv[None, :]
    cos = jnp.concatenate([jnp.cos(ang_r), jnp.cos(ang_r), jnp.cos(ang_c), jnp.cos(ang_c)], axis=-1)
    sin = jnp.concatenate([-jnp.sin(ang_r), jnp.sin(ang_r), -jnp.sin(ang_c), jnp.sin(ang_c)], axis=-1)
    reps = DIFF_WIDTH // DIFF_HEAD_DIM
    return jnp.tile(cos, (1, reps)), jnp.tile(sin, (1, reps))


def _prepare(g_norm1, g_norm2, w_in, w_gate_fwd, b_gate_fwd, w_gate_bwd, b_gate_bwd, g_q_norm, g_k_norm,
             w_out, w_router, b_router, w_exp_gate, w_exp_up, w_exp_down):
    depth, d, _ = w_in.shape
    r = GLA_GATE_RANK
    gate_cols = jnp.pad(w_in[..., 2048:2048 + 2 * r], ((0, 0), (0, 0), (0, GATE_COLS - 2 * r)))
    w_in_r = jnp.concatenate([w_in[..., :2048], w_in[..., 2048 + 2 * r:], gate_cols], axis=-1).astype(BF16)
    w_gate = jnp.zeros((depth, GATE_COLS, 2 * GLA_WIDTH), F32)
    w_gate = w_gate.at[:, 0:r, 0:GLA_WIDTH].set(w_gate_fwd).at[:, r:2 * r, GLA_WIDTH:].set(w_gate_bwd).astype(BF16)
    b_gate = jnp.concatenate([b_gate_fwd, b_gate_bwd], axis=-1).reshape(depth, 1, 2 * GLA_WIDTH)
    grp = jnp.arange(DIFF_WIDTH) // DIFF_HEAD_DIM
    gmat = jnp.where(grp[:, None] == grp[None, :], 1.0 / DIFF_HEAD_DIM, 0.0).astype(BF16)
    reps = DIFF_WIDTH // DIFF_HEAD_DIM
    tok = jnp.arange(TOKEN_TILE)
    same_chunk = (tok[:, None] // GLA_CHUNK) == (tok[None, :] // GLA_CHUNK)
    tri = jnp.stack([same_chunk & (tok[:, None] >= tok[None, :]),
                     same_chunk & (tok[:, None] <= tok[None, :])]).astype(BF16)
    return {
        "tri": tri,
        "g1": g_norm1.reshape(depth, 1, d),
        "g2": g_norm2.reshape(depth, 1, d),
        "w_in": w_in_r,
        "w_gate": w_gate,
        "b_gate": b_gate,
        "gmat": gmat,
        "gq": jnp.tile(g_q_norm, (1, reps)).reshape(depth, 1, DIFF_WIDTH),
        "gk": jnp.tile(g_k_norm, (1, reps)).reshape(depth, 1, DIFF_WIDTH),
        "w_out": w_out.astype(BF16),
        "w_router_t": w_router.T,
        "b_router": b_router.reshape(N_EXPERTS, 1),
        "w_eg": w_exp_gate.astype(BF16),
        "w_eu": w_exp_up.astype(BF16),
        "w_ed": w_exp_down.astype(BF16),
    }


def kernel(x_prompt, x_sample, cache_k, cache_v, state_gla_fwd, state_gla_bwd, c, c_ctx, w_mod, b_mod, g_norm1,
           g_norm2, w_in, w_gate_fwd, b_gate_fwd, w_gate_bwd, b_gate_bwd, g_gla_out, g_q_norm, g_k_norm, lambda_q1,
           lambda_k1, lambda_q2, lambda_k2, g_diff_out, w_out, w_router, b_router, w_exp_gate, w_exp_up, w_exp_down):
    nb_p, seq_p, d = x_prompt.shape
    nb_s, seq_s, _ = x_sample.shape
    depth = w_in.shape[0]
    past = cache_k.shape[2]

    prep = _prepare(g_norm1, g_norm2, w_in, w_gate_fwd, b_gate_fwd, w_gate_bwd, b_gate_bwd, g_q_norm, g_k_norm,
                    w_out, w_router, b_router, w_exp_gate, w_exp_up, w_exp_down)
    rope = _rope_tables(seq_s)

    mod_rows = 8 * ((1 + nb_s + 7) // 8)
    cs = jnp.zeros((mod_rows, d), F32).at[0].set(c_ctx).at[1:1 + nb_s].set(c)
    mod_all = _modulation(cs, w_mod, b_mod)
    ctx_k = cache_k.reshape(nb_s, depth, past, DIFF_WIDTH)
    ctx_v = cache_v.reshape(nb_s, depth, past, DIFF_WIDTH)
    gg = g_gla_out.reshape(depth, 1, GLA_DV)
    gd = g_diff_out.reshape(depth, DIFF_V_DIM, 1)
    lams = [a.reshape(depth, 1, DIFF_HEAD_DIM) for a in (lambda_q1, lambda_k1, lambda_q2, lambda_k2)]

    xp = x_prompt.reshape(nb_p * seq_p, d)
    xs = x_sample.reshape(nb_s * seq_s, d)
    yp = ys = None
    mod_prev = None
    new_k, new_v, new_sf, new_sb = [], [], [], []
    for l in range(depth):
        lam_init = 0.8 - 0.6 * math.exp(-0.3 * l)
        mod = mod_all[l].reshape(mod_rows, 1, 6 * d)
        xp, gla_in, la, qt, kd, vt, k_l, v_l = _pre_mixer(l, xp, yp, mod_prev, mod, prep, None, nb_p, seq_p, False)
        og, sf_l, sb_l = _gla(l, gla_in, la, gg, None, None, nb_p, seq_p)
        od = _diff_attention(l, lam_init, qt, kd, vt, None, None, lams, gd, nb_p, seq_p)
        xp, h2, route = _post_mixer(l, og, od, xp, mod, prep, seq_p, False)
        yp = _moe(l, h2, route, prep)
        new_k.append(k_l)
        new_v.append(v_l)
        new_sf.append(sf_l)
        new_sb.append(sb_l)
        xs, gla_in, la, qt, kd, vt = _pre_mixer(l, xs, ys, mod_prev, mod, prep, rope, nb_s, seq_s, True)
        og, _, _ = _gla(l, gla_in, la, gg, state_gla_fwd, state_gla_bwd, nb_s, seq_s)
        od = _diff_attention(l, lam_init, qt, kd, vt, ctx_k, ctx_v, lams, gd, nb_s, seq_s)
        xs, h2, route = _post_mixer(l, og, od, xs, mod, prep, seq_s, True)
        ys = _moe(l, h2, route, prep)
        mod_prev = mod
    xp = _residual(xp, yp, mod_prev, seq_p, False)
    xs = _residual(xs, ys, mod_prev, seq_s, True)

    new_cache_k = jnp.concatenate(new_k, axis=1).reshape(nb_p, depth, seq_p, DIFF_HEADS, 2, DIFF_HEAD_DIM)
    new_cache_v = jnp.concatenate(new_v, axis=1).reshape(nb_p, depth, seq_p, DIFF_HEADS, DIFF_V_DIM)
    new_sf = jnp.stack(new_sf, axis=1)
    new_sb = jnp.stack(new_sb, axis=1)
    return (xp.reshape(nb_p, seq_p, d), xs.reshape(nb_s, seq_s, d), new_cache_k, new_cache_v, new_sf, new_sb)
```

```python
import functools
import math

import jax
import jax.numpy as jnp
from jax import lax
from jax.experimental import pallas as pl
from jax.experimental.pallas import tpu as pltpu

F32 = jnp.float32
BF16 = jnp.bfloat16

D_MODEL = 1024
GLA_HEADS = 4
GLA_DK = 128
GLA_DV = 128
GLA_WIDTH = GLA_HEADS * GLA_DK
GLA_GATE_RANK = 16
GLA_GATE_NORMALIZER = 16.0
GLA_CHUNK = 64
GLA_HEADS_PER_STEP = 4
DIFF_HEADS = 4
DIFF_HEAD_DIM = 64
DIFF_V_DIM = 128
DIFF_WIDTH = DIFF_HEADS * 2 * DIFF_HEAD_DIM
ROPE_HALF = DIFF_HEAD_DIM // 2
ROPE_BASE = 10000.0
GRID_W = 64
N_EXPERTS = 16
N_GROUPS = 4
EXPERTS_PER_GROUP = 4
D_EXPERT = 512
NORM_EPS = 1e-6
LOG2E = 1.4426950408889634

PAIR_A = (0, 0, 0, 1, 1, 3)
PAIR_B = (1, 2, 3, 3, 2, 2)
N_PAIRS = len(PAIR_A)
N_COMBOS = N_GROUPS * N_PAIRS

GATE_COLS = 128
MAIN_COLS = 7 * 512
TOKEN_TILE = 256
POST_TILE = 512
MOE_TILE = 256
DMA_UNROLL = 8
ATTN_TQ = 1024
ATTN_TK = 256
ATTN_GROUP = 128
ATTN_UNROLL = 8
ONES_ROWS = 16
VMEM_LIMIT = 56 * 1024 * 1024


def _cparams(sem):
    return pltpu.CompilerParams(dimension_semantics=sem, vmem_limit_bytes=VMEM_LIMIT)


def _dot(a, b):
    return jnp.dot(a, b, preferred_element_type=F32)


def _dot_nt(a, b):
    return lax.dot_general(a, b, (((1,), (1,)), ((), ())), preferred_element_type=F32)


def _dot_tn(a, b):
    return lax.dot_general(a, b, (((0,), (0,)), ((), ())), preferred_element_type=F32)


def _sigmoid(x):
    return 1.0 / (1.0 + jnp.exp(-x))


def _split_bf16(x):
    hi = x.astype(BF16)
    lo = (x - hi.astype(F32)).astype(BF16)
    return hi, lo


def _mod_kernel(c_ref, w_ref, b_ref, o_ref):
    c = c_ref[...]
    s = c * _sigmoid(c)
    o_ref[0] = jnp.dot(s, w_ref[0], preferred_element_type=F32,
                       precision=lax.Precision.HIGHEST) + b_ref[0]


def _modulation(cs, w_mod, b_mod):
    depth, d, n = w_mod.shape
    rows = cs.shape[0]
    tn = 1536
    return pl.pallas_call(
        _mod_kernel,
        grid=(depth, n // tn),
        in_specs=[pl.BlockSpec((rows, d), lambda l, j: (0, 0)),
                  pl.BlockSpec((1, d, tn), lambda l, j: (l, 0, j)),
                  pl.BlockSpec((1, 1, tn), lambda l, j: (l, 0, j))],
        out_specs=pl.BlockSpec((1, rows, tn), lambda l, j: (l, 0, j)),
        out_shape=jax.ShapeDtypeStruct((depth, rows, n), F32),
        compiler_params=_cparams(("arbitrary", "arbitrary")),
        name="modulation",
    )(cs, w_mod, b_mod.reshape(depth, 1, n))


def _group_rms(p, gmat_ref, g):
    ms = _dot((p * p).astype(BF16), gmat_ref[...])
    return p * lax.rsqrt(ms + NORM_EPS) * g


def _rope(y, cos, sin):
    w = y.shape[1]
    lane = lax.broadcasted_iota(jnp.int32, y.shape, 1)
    first = (lane % (2 * (ROPE_HALF // 2))) < (ROPE_HALF // 2)
    partner = jnp.where(first, pltpu.roll(y, w - ROPE_HALF // 2, axis=1), pltpu.roll(y, ROPE_HALF // 2, axis=1))
    return y * cos + partner * sin


def _pre_kernel(has_moe, is_sample, *refs):
    refs = list(refs)
    x_ref = refs.pop(0)
    if has_moe:
        y_ref, modp_ref = refs[:2]
        refs = refs[2:]
    mod_ref, g1_ref, win_ref, wgate_ref, bgate_ref, tri_ref, gmat_ref, gq_ref, gk_ref = refs[:9]
    refs = refs[9:]
    if is_sample:
        cos_ref, sin_ref = refs[:2]
        refs = refs[2:]
    xo_ref = refs.pop(0) if has_moe else None
    gla_ref, la_ref, qt_ref, kd_ref, vt_ref = refs[:5]
    refs = refs[5:]
    if not is_sample:
        kc_ref, vc_ref = refs

    d = D_MODEL
    m = mod_ref[0]
    x = x_ref[...]
    if has_moe:
        x = x + modp_ref[0][:, 5 * d:6 * d] * y_ref[...].astype(F32)
        xo_ref[...] = x
    ms = jnp.mean(x * x, axis=-1, keepdims=True)
    gs = g1_ref[0] * (1.0 + m[:, d:2 * d])
    hb = (x * lax.rsqrt(ms + NORM_EPS) * gs + m[:, 0:d]).astype(BF16)

    for j in range(4):
        p = _dot(hb, win_ref[0, :, j * 512:(j + 1) * 512])
        if j == 0:
            p = p * (GLA_DK ** -0.5)
        gla_ref[:, j * 512:(j + 1) * 512] = p.astype(BF16)

    a = _dot(hb, win_ref[0, :, MAIN_COLS:MAIN_COLS + GATE_COLS])
    z = _dot(a.astype(BF16), wgate_ref[0]) + bgate_ref[0]
    la = (jnp.minimum(z, 0.0) - jnp.log1p(jnp.exp(-jnp.abs(z)))) * (1.0 / GLA_GATE_NORMALIZER)
    la_hi, la_lo = _split_bf16(la)
    w = GLA_WIDTH
    la_ref[:, 0:w] = _dot(tri_ref[0], la_hi[:, 0:w]) + _dot(tri_ref[0], la_lo[:, 0:w])
    la_ref[:, w:2 * w] = _dot(tri_ref[1], la_hi[:, w:2 * w]) + _dot(tri_ref[1], la_lo[:, w:2 * w])

    scale = DIFF_HEAD_DIM ** -0.5 * LOG2E
    q = _group_rms(_dot(hb, win_ref[0, :, 2048:2560]), gmat_ref, gq_ref[0])
    k = _group_rms(_dot(hb, win_ref[0, :, 2560:3072]), gmat_ref, gk_ref[0])
    v = _dot(hb, win_ref[0, :, 3072:3584])
    if is_sample:
        cos = cos_ref[...]
        sin = sin_ref[...]
        q = _rope(q, cos, sin)
        k = _rope(k, cos, sin)
    else:
        kc_ref[0, 0] = k
        vc_ref[0, 0] = v
    qt_ref[0] = (q * scale).astype(BF16).T
    kd_ref[...] = k.astype(BF16)
    vt_ref[0] = v.astype(BF16).T


def _pre_mixer(l, x, y, mod_prev, mod, prep, rope, nb, seq, is_sample):
    t, d = x.shape
    tt = TOKEN_TILE
    tiles_per_batch = seq // tt
    has_moe = y is not None
    if is_sample:
        mod_map = lambda i: (1 + i // tiles_per_batch, 0, 0)
    else:
        mod_map = lambda i: (0, 0, 0)
    row = lambda i: (i, 0)
    const2 = lambda i: (0, 0)
    lay3 = lambda i: (l, 0, 0)

    ins = [x]
    in_specs = [pl.BlockSpec((tt, d), row)]
    if has_moe:
        ins += [y, mod_prev]
        in_specs += [pl.BlockSpec((tt, d), row), pl.BlockSpec((1, 1, 6 * d), mod_map)]
    ins += [mod, prep["g1"], prep["w_in"], prep["w_gate"], prep["b_gate"], prep["tri"], prep["gmat"], prep["gq"],
            prep["gk"]]
    in_specs += [pl.BlockSpec((1, 1, 6 * d), mod_map),
                 pl.BlockSpec((1, 1, d), lay3),
                 pl.BlockSpec((1, d, MAIN_COLS + GATE_COLS), lay3),
                 pl.BlockSpec((1, GATE_COLS, 2 * GLA_WIDTH), lay3),
                 pl.BlockSpec((1, 1, 2 * GLA_WIDTH), lay3),
                 pl.BlockSpec((2, tt, tt), lambda i: (0, 0, 0)),
                 pl.BlockSpec((DIFF_WIDTH, DIFF_WIDTH), const2),
                 pl.BlockSpec((1, 1, DIFF_WIDTH), lay3),
                 pl.BlockSpec((1, 1, DIFF_WIDTH), lay3)]
    if is_sample:
        ins += [rope[0], rope[1]]
        in_specs += [pl.BlockSpec((tt, DIFF_WIDTH), lambda i: (i % tiles_per_batch, 0))] * 2

    out_shape = []
    out_specs = []
    if has_moe:
        out_shape.append(jax.ShapeDtypeStruct((t, d), F32))
        out_specs.append(pl.BlockSpec((tt, d), row))
    tr = lambda i: (i // tiles_per_batch, 0, i % tiles_per_batch)
    out_shape += [jax.ShapeDtypeStruct((t, 4 * GLA_WIDTH), BF16),
                  jax.ShapeDtypeStruct((t, 2 * GLA_WIDTH), F32),
                  jax.ShapeDtypeStruct((nb, DIFF_WIDTH, seq), BF16),
                  jax.ShapeDtypeStruct((t, DIFF_WIDTH), BF16),
                  jax.ShapeDtypeStruct((nb, DIFF_WIDTH, seq), BF16)]
    out_specs += [pl.BlockSpec((tt, 4 * GLA_WIDTH), row),
                  pl.BlockSpec((tt, 2 * GLA_WIDTH), row),
                  pl.BlockSpec((1, DIFF_WIDTH, tt), tr),
                  pl.BlockSpec((tt, DIFF_WIDTH), row),
                  pl.BlockSpec((1, DIFF_WIDTH, tt), tr)]
    if not is_sample:
        assert seq == tt
        out_shape += [jax.ShapeDtypeStruct((nb, 1, seq, DIFF_WIDTH), F32)] * 2
        out_specs += [pl.BlockSpec((1, 1, seq, DIFF_WIDTH), lambda i: (i, 0, 0, 0))] * 2

    outs = pl.pallas_call(
        functools.partial(_pre_kernel, has_moe, is_sample),
        grid=(t // tt,),
        in_specs=in_specs,
        out_specs=out_specs,
        out_shape=out_shape,
        compiler_params=_cparams(("arbitrary",)),
        name="pre_mixer",
    )(*ins)
    outs = list(outs)
    x_new = outs.pop(0) if has_moe else x
    return [x_new] + outs


def _gla_kernel(has_state, n_chunks, hg, *refs):
    refs = list(refs)
    q_ref, k_ref, v_ref, g_ref, bf_ref, bb_ref, gg_ref = refs[:7]
    refs = refs[7:]
    if has_state:
        s0f_ref, s0b_ref = refs[:2]
        refs = refs[2:]
    o_ref, sf_ref, sb_ref, st_ref, acc_ref = refs

    c = GLA_CHUNK
    dk = GLA_DK
    for hd in range(hg):
        if has_state:
            st_ref[2 * hd] = s0f_ref[0, 0, hd].T
            st_ref[2 * hd + 1] = s0b_ref[0, 0, hd].T
        else:
            st_ref[2 * hd] = jnp.zeros((GLA_DV, dk), F32)
            st_ref[2 * hd + 1] = jnp.zeros((GLA_DV, dk), F32)

    r = lax.broadcasted_iota(jnp.int32, (c, c), 0)
    s = lax.broadcasted_iota(jnp.int32, (c, c), 1)
    lower = r >= s
    upper = r <= s

    def scores(rows, hd, b_ref, st_i, mid_row, last_row):
        cs = slice(hd * dk, (hd + 1) * dk)
        q = q_ref[rows, cs].astype(F32)
        k = k_ref[rows, cs].astype(F32)
        v = v_ref[rows, cs]
        b = b_ref[rows, cs]
        mid = b[mid_row:mid_row + 1]
        last = b[last_row:last_row + 1]
        qe = (q * jnp.exp(b - mid)).astype(BF16)
        ke = (k * jnp.exp(mid - b)).astype(BF16)
        qi = (q * jnp.exp(b)).astype(BF16)
        ks = (k * jnp.exp(last - b)).astype(BF16)
        st = st_ref[st_i]
        return _dot_nt(qe, ke), _dot_nt(qi, st.astype(BF16)), _dot_tn(v, ks), v, st, last

    def outputs(sc, mask, st_i):
        att, o_inter, kv, v, st, last = sc
        st_ref[st_i] = st * jnp.exp(last) + kv
        return _dot(jnp.where(mask, att, 0.0).astype(BF16), v) + o_inter

    def finish(o, rows, hd):
        cs = slice(hd * dk, (hd + 1) * dk)
        ms = jnp.mean(o * o, axis=-1, keepdims=True)
        g = g_ref[rows, cs].astype(F32)
        o_ref[rows, cs] = (o * lax.rsqrt(ms + NORM_EPS) * gg_ref[0] * (g * _sigmoid(g))).astype(BF16)

    def step(n, second_visit):
        rf = pl.ds(pl.multiple_of(n * c, c), c)
        rb = pl.ds(pl.multiple_of((n_chunks - 1 - n) * c, c), c)
        sc = []
        for hd in range(hg):
            sc.append(scores(rf, hd, bf_ref, 2 * hd, c // 2, c - 1))
            sc.append(scores(rb, hd, bb_ref, 2 * hd + 1, c - 1 - c // 2, 0))
        for hd in range(hg):
            cs = slice(hd * dk, (hd + 1) * dk)
            for rows, sci, mask, st_i in ((rf, sc[2 * hd], lower, 2 * hd), (rb, sc[2 * hd + 1], upper, 2 * hd + 1)):
                o = outputs(sci, mask, st_i)
                if second_visit:
                    finish(acc_ref[rows, cs] + o, rows, hd)
                else:
                    acc_ref[rows, cs] = o

    half = n_chunks // 2
    lax.fori_loop(0, half, lambda n, carry: (step(n, False), carry)[1], 0)
    lax.fori_loop(half, n_chunks, lambda n, carry: (step(n, True), carry)[1], 0)

    for hd in range(hg):
        sf_ref[0, hd] = st_ref[2 * hd].T
        sb_ref[0, hd] = st_ref[2 * hd + 1].T


def _gla(l, gla_in, bsum, gg, s0f, s0b, nb, seq):
    t = gla_in.shape[0]
    h = GLA_HEADS
    hg = GLA_HEADS_PER_STEP
    nhb = h // hg
    w = hg * GLA_DK
    has_state = s0f is not None
    n_chunks = seq // GLA_CHUNK
    assert n_chunks % 2 == 0
    col = lambda off: (lambda b, hh: (b, off + hh))
    in_bytes = seq * w * (4 * 2 + 2 * 4)
    mode = pl.Buffered(1) if 2 * in_bytes > VMEM_LIMIT // 2 else None
    big = lambda off: pl.BlockSpec((seq, w), col(off), pipeline_mode=mode)
    ins = [gla_in, gla_in, gla_in, gla_in, bsum, bsum, gg]
    in_specs = [big(0), big(nhb), big(2 * nhb), big(3 * nhb), big(0), big(nhb),
                pl.BlockSpec((1, 1, GLA_DV), lambda b, hh: (l, 0, 0))]
    if has_state:
        ins += [s0f, s0b]
        in_specs += [pl.BlockSpec((1, 1, hg, GLA_DK, GLA_DV), lambda b, hh: (b, l, hh, 0, 0))] * 2
    return pl.pallas_call(
        functools.partial(_gla_kernel, has_state, n_chunks, hg),
        grid=(nb, nhb),
        in_specs=in_specs,
        out_specs=[pl.BlockSpec((seq, w), col(0)),
                   pl.BlockSpec((1, hg, GLA_DK, GLA_DV), lambda b, hh: (b, hh, 0, 0)),
                   pl.BlockSpec((1, hg, GLA_DK, GLA_DV), lambda b, hh: (b, hh, 0, 0))],
        out_shape=[jax.ShapeDtypeStruct((t, GLA_WIDTH), BF16),
                   jax.ShapeDtypeStruct((nb, h, GLA_DK, GLA_DV), F32),
                   jax.ShapeDtypeStruct((nb, h, GLA_DK, GLA_DV), F32)],
        scratch_shapes=[pltpu.VMEM((2 * hg, GLA_DV, GLA_DK), F32), pltpu.VMEM((seq, w), F32)],
        compiler_params=_cparams(("arbitrary", "arbitrary")),
        name="gla",
    )(*ins)


def _attn_kernel(lam_init, has_ctx, n_kt, tk, *refs):
    refs = list(refs)
    qt_ref, k_ref, vt_ref = refs[:3]
    refs = refs[3:]
    if has_ctx:
        ck_ref, cv_ref = refs[:2]
        refs = refs[2:]
    lq1_ref, lk1_ref, lq2_ref, lk2_ref, gd_ref, o_ref = refs

    qt = qt_ref[0]
    tq = qt.shape[1]
    gq = ATTN_GROUP
    n_groups = tq // gq
    dim = lax.broadcasted_iota(jnp.int32, (2 * DIFF_HEAD_DIM, gq), 0)
    zero = jnp.zeros((2 * DIFF_HEAD_DIM, gq), BF16)
    qws = []
    for g in range(n_groups):
        qg = qt[:, g * gq:(g + 1) * gq]
        qws.append(jnp.concatenate([jnp.where(dim < DIFF_HEAD_DIM, qg, zero),
                                    jnp.where(dim >= DIFF_HEAD_DIM, qg, zero)], axis=1))

    def scores(k):
        return [_dot(k, qws[g]) for g in range(n_groups)]

    def absorb(state, sts, vt):
        new = []
        for g in range(n_groups):
            m, acc = state[g]
            st = sts[g]
            m_new = jnp.maximum(m, jnp.max(st, axis=0, keepdims=True))
            alpha = jnp.exp2(m - m_new)
            p = jnp.exp2(st - m_new)
            acc = alpha * acc + _dot(vt, p.astype(BF16))
            new.append((m_new, acc))
        return tuple(new)

    def process(state, tiles):
        sts = scores(tiles[0][0])
        for u, (_, vt) in enumerate(tiles):
            nxt = scores(tiles[u + 1][0]) if u + 1 < len(tiles) else None
            state = absorb(state, sts, vt)
            sts = nxt
        return state

    def with_ones(vt):
        return jnp.concatenate([vt, jnp.ones((ONES_ROWS, vt.shape[1]), BF16)], axis=0)

    state = tuple((jnp.full((1, 2 * gq), -jnp.inf, F32), jnp.zeros((DIFF_V_DIM + ONES_ROWS, 2 * gq), F32))
                  for _ in range(n_groups))
    if has_ctx:
        past = ck_ref.shape[2]
        ctk = tk if past % tk == 0 else past
        cvt = cv_ref[0, 0].T.astype(BF16)
        tiles = [(ck_ref[0, 0, j * ctk:(j + 1) * ctk, :].astype(BF16), with_ones(cvt[:, j * ctk:(j + 1) * ctk]))
                 for j in range(past // ctk)]
        state = process(state, tiles)

    unroll = max(u for u in range(1, ATTN_UNROLL + 1) if n_kt % u == 0)

    def body(j, st):
        tiles = []
        for u in range(unroll):
            rr = pl.ds(pl.multiple_of((j * unroll + u) * tk, tk), tk)
            tiles.append((k_ref[rr, :], with_ones(vt_ref[0, :, rr])))
        return process(st, tiles)

    state = lax.fori_loop(0, n_kt // unroll, body, state)

    lam = (jnp.exp(jnp.sum(lq1_ref[0] * lk1_ref[0], axis=-1, keepdims=True))
           - jnp.exp(jnp.sum(lq2_ref[0] * lk2_ref[0], axis=-1, keepdims=True)) + lam_init)
    for g in range(n_groups):
        m, acc = state[g]
        on = acc[:DIFF_V_DIM] / acc[DIFF_V_DIM:DIFF_V_DIM + 1]
        o = on[:, :gq] - lam * on[:, gq:]
        ms = jnp.mean(o * o, axis=0, keepdims=True)
        o = o * lax.rsqrt(ms + NORM_EPS) * gd_ref[0] * (1.0 - lam_init)
        o_ref[g * gq:(g + 1) * gq, :] = o.T.astype(BF16)


def _diff_attention(l, lam_init, qt, kd, vt, ctx_k, ctx_v, lams, gd, nb, seq):
    t = kd.shape[0]
    h = DIFF_HEADS
    has_ctx = ctx_k is not None
    tq = min(ATTN_TQ, seq)
    tk = min(ATTN_TK, seq)
    nq = seq // tq
    w = 2 * DIFF_HEAD_DIM
    ins = [qt, kd, vt]
    in_specs = [pl.BlockSpec((1, w, tq), lambda b, hh, i: (b, hh, i)),
                pl.BlockSpec((seq, w), lambda b, hh, i: (b, hh)),
                pl.BlockSpec((1, DIFF_V_DIM, seq), lambda b, hh, i: (b, hh, 0))]
    if has_ctx:
        past = ctx_k.shape[2]
        ins += [ctx_k, ctx_v]
        in_specs += [pl.BlockSpec((1, 1, past, w), lambda b, hh, i: (b, l, 0, hh)),
                     pl.BlockSpec((1, 1, past, DIFF_V_DIM), lambda b, hh, i: (b, l, 0, hh))]
    lay3 = lambda b, hh, i: (l, 0, 0)
    ins += list(lams) + [gd]
    in_specs += [pl.BlockSpec((1, 1, DIFF_HEAD_DIM), lay3)] * 4 + [pl.BlockSpec((1, DIFF_V_DIM, 1), lay3)]
    return pl.pallas_call(
        functools.partial(_attn_kernel, lam_init, has_ctx, seq // tk, tk),
        grid=(nb, h, nq),
        in_specs=in_specs,
        out_specs=pl.BlockSpec((tq, DIFF_V_DIM), lambda b, hh, i: (b * nq + i, hh)),
        out_shape=jax.ShapeDtypeStruct((t, DIFF_WIDTH), BF16),
        compiler_params=_cparams(("arbitrary", "arbitrary", "arbitrary")),
        name="diff_attention",
    )(*ins)


def _top2_of4(v):
    m1 = jnp.maximum(jnp.maximum(v[0], v[1]), jnp.maximum(v[2], v[3]))
    i1 = jnp.where(v[0] == m1, 0, jnp.where(v[1] == m1, 1, jnp.where(v[2] == m1, 2, 3)))
    neg = jnp.full_like(m1, -jnp.inf)
    w = [jnp.where(i1 == j, neg, v[j]) for j in range(4)]
    m2 = jnp.maximum(jnp.maximum(w[0], w[1]), jnp.maximum(w[2], w[3]))
    i2 = jnp.where(w[0] == m2, 0, jnp.where(w[1] == m2, 1, jnp.where(w[2] == m2, 2, 3)))
    return m1, i1, m2, i2


def _post_kernel(og_ref, od_ref, x_ref, mod_ref, g2_ref, wo_ref, wr_ref, br_ref, x1_ref, hx_ref, route_ref):
    d = D_MODEL
    sub = TOKEN_TILE
    n_sub = x_ref.shape[0] // sub
    m = mod_ref[0]
    rows = [slice(j * sub, (j + 1) * sub) for j in range(n_sub)]
    outs = [_dot(og_ref[rs, :], wo_ref[0, 0:GLA_WIDTH, :]) + _dot(od_ref[rs, :], wo_ref[0, GLA_WIDTH:, :])
            for rs in rows]
    w_hi, w_lo = _split_bf16(wr_ref[...])
    w_hl = jnp.concatenate([w_hi, w_lo], axis=0)
    zs = []
    for rs, out in zip(rows, outs):
        x1 = x_ref[rs, :] + m[:, 2 * d:3 * d] * out
        x1_ref[rs, :] = x1
        ms = jnp.mean(x1 * x1, axis=-1, keepdims=True)
        h2 = x1 * lax.rsqrt(ms + NORM_EPS) * (g2_ref[0] * (1.0 + m[:, 4 * d:5 * d])) + m[:, 3 * d:4 * d]
        h_hi, h_lo = _split_bf16(h2)
        hx_ref[rs, 0:d] = h2
        za = _dot_nt(w_hl, h_hi)
        zs.append(za[0:N_EXPERTS] + za[N_EXPERTS:2 * N_EXPERTS] + _dot_nt(w_hi, h_lo))
    for rs, z in zip(rows, zs):
        _route(z, br_ref, hx_ref, route_ref, rs)


def _route(z, br_ref, hx_ref, route_ref, rs):
    d = D_MODEL
    s = _sigmoid(z)
    sel = s + br_ref[...]

    e = EXPERTS_PER_GROUP
    tops = []
    for g in range(N_GROUPS):
        tops.append(_top2_of4([sel[g * e + j:g * e + j + 1, :] for j in range(e)]))
    score = [t[0] + t[2] for t in tops]
    best = jnp.maximum(jnp.maximum(score[0], score[1]), jnp.maximum(score[2], score[3]))
    gi = jnp.where(score[0] == best, 0, jnp.where(score[1] == best, 1, jnp.where(score[2] == best, 2, 3)))

    def pick(rows):
        return jnp.where(gi == 0, rows[0], jnp.where(gi == 1, rows[1], jnp.where(gi == 2, rows[2], rows[3])))

    i1 = pick([t[1] for t in tops])
    i2 = pick([t[3] for t in tops])

    def gate_of(idx):
        per_group = []
        for g in range(N_GROUPS):
            rows = [s[g * e + j:g * e + j + 1, :] for j in range(e)]
            per_group.append(jnp.where(idx == 0, rows[0], jnp.where(idx == 1, rows[1],
                                                                     jnp.where(idx == 2, rows[2], rows[3]))))
        return pick(per_group)

    s1 = gate_of(i1)
    s2 = gate_of(i2)
    tot = s1 + s2
    w1 = s1 / tot
    w2 = s2 / tot
    lo = jnp.minimum(i1, i2)
    hi = jnp.maximum(i1, i2)
    w_lo = jnp.where(i1 < i2, w1, w2)
    w_hi = jnp.where(i1 < i2, w2, w1)
    pair = jnp.where(lo == 0, hi - 1, jnp.where(lo == 1, jnp.where(hi == 3, 3, 4), 5))
    swapped = pair == 5
    gate_a = jnp.where(swapped, w_hi, w_lo)
    gate_b = jnp.where(swapped, w_lo, w_hi)
    route_ref[:, rs] = jnp.broadcast_to(gi * N_PAIRS + pair, (route_ref.shape[0], gate_a.shape[1]))
    r = lax.broadcasted_iota(jnp.int32, (GATE_COLS, gate_a.shape[1]), 0)
    gates_t = jnp.where(r == 0, gate_a, jnp.where(r == 1, gate_b, 0.0))
    hx_ref[rs, d:d + GATE_COLS] = gates_t.T


def _post_mixer(l, og, od, x, mod, prep, seq, is_sample):
    t, d = x.shape
    tt = POST_TILE if (seq % POST_TILE == 0 or not is_sample) and t % POST_TILE == 0 else TOKEN_TILE
    tiles_per_batch = max(seq // tt, 1)
    if is_sample:
        mod_map = lambda i: (1 + i // tiles_per_batch, 0, 0)
    else:
        mod_map = lambda i: (0, 0, 0)
    row = lambda i: (i, 0)
    lay3 = lambda i: (l, 0, 0)
    return pl.pallas_call(
        _post_kernel,
        grid=(t // tt,),
        in_specs=[pl.BlockSpec((tt, GLA_WIDTH), row),
                  pl.BlockSpec((tt, DIFF_WIDTH), row),
                  pl.BlockSpec((tt, d), row),
                  pl.BlockSpec((1, 1, 6 * d), mod_map),
                  pl.BlockSpec((1, 1, d), lay3),
                  pl.BlockSpec((1, d, d), lay3),
                  pl.BlockSpec((N_EXPERTS, d), lambda i: (0, 0)),
                  pl.BlockSpec((N_EXPERTS, 1), lambda i: (0, 0))],
        out_specs=[pl.BlockSpec((tt, d), row),
                   pl.BlockSpec((tt, d + GATE_COLS), row),
                   pl.BlockSpec((8, tt), lambda i: (0, i))],
        out_shape=[jax.ShapeDtypeStruct((t, d), F32),
                   jax.ShapeDtypeStruct((t, d + GATE_COLS), F32),
                   jax.ShapeDtypeStruct((8, t), jnp.int32)],
        compiler_params=_cparams(("arbitrary",)),
        name="post_mixer",
    )(og, od, x, mod, prep["g2"], prep["w_out"], prep["w_router_t"], prep["b_router"])


def _moe_kernel(n_tiles, order_ref, first_ref, count_ref, ea_ref, eb_ref,
                hx_hbm, wga_ref, wua_ref, wda_ref, wgb_ref, wub_ref, wdb_ref,
                y_hbm, xbuf, ybuf, in_sem, out_sem):
    i = pl.program_id(0)
    slot = i % 2
    d = D_MODEL
    tm = xbuf.shape[1]
    n_tokens = hx_hbm.shape[0]

    def gather_copy(buf_slot, r, tok):
        return pltpu.make_async_copy(hx_hbm.at[pl.ds(tok, 1), :], xbuf.at[buf_slot, pl.ds(r, 1), :],
                                     in_sem.at[buf_slot])

    def scatter_copy(buf_slot, r, row):
        return pltpu.make_async_copy(ybuf.at[buf_slot, pl.ds(r, 1), :], y_hbm.at[pl.ds(row, 1), :],
                                     out_sem.at[buf_slot])

    def for_rows(fn):
        def body(q, c):
            for s in range(DMA_UNROLL):
                fn(q * DMA_UNROLL + s)
            return c
        lax.fori_loop(0, tm // DMA_UNROLL, body, 0)

    def wait_gather(buf_slot):
        for_rows(lambda r: gather_copy(buf_slot, r, 0).wait())

    def wait_scatter(buf_slot):
        for_rows(lambda r: scatter_copy(buf_slot, r, 0).wait())

    def scatter_row(buf_slot, base, n, r):
        return jnp.where(r < n, order_ref[base + r], n_tokens + buf_slot * tm + r)

    prev = jnp.maximum(i - 1, 0)

    @pl.when(i == 0)
    def _():
        ybuf[...] = jnp.zeros_like(ybuf)
        for s in range(2):
            spare_init = pltpu.make_async_copy(ybuf.at[s], y_hbm.at[pl.ds(n_tokens + s * tm, tm), :], out_sem.at[s])
            spare_init.start()
            spare_init.wait()
        base0 = first_ref[0]
        for_rows(lambda r: gather_copy(0, r, order_ref[base0 + r]).start())

    @pl.when((i == 0) | (count_ref[prev] > 0))
    def _():
        wait_gather(slot)

    @pl.when((i == 1) | ((i >= 2) & (count_ref[jnp.maximum(i - 2, 0)] > 0)))
    def _():
        wait_scatter(slot)

    @pl.when(count_ref[i] > 0)
    def _():
        xg = xbuf[slot]
        x = xg[:, 0:d].astype(BF16)
        base_next = first_ref[i + 1]
        base_prev = first_ref[prev]
        n_prev = jnp.where(i > 0, count_ref[prev], 0)
        per_dot = -(-tm // 6)

        def issue(part):
            for r in range(part * per_dot, min((part + 1) * per_dot, tm)):
                gather_copy(1 - slot, r, order_ref[base_next + r]).start()
                scatter_copy(1 - slot, r, scatter_row(1 - slot, base_prev, n_prev, r)).start()

        def ffn(wg_ref, wu_ref, wd_ref, part):
            hg = _dot(x, wg_ref[0, 0])
            issue(part)
            hu = _dot(x, wu_ref[0, 0])
            issue(part + 1)
            a = (hg * _sigmoid(hg)) * hu
            out = _dot(a.astype(BF16), wd_ref[0, 0])
            issue(part + 2)
            return out

        ybuf[slot] = (xg[:, d:d + 1] * ffn(wga_ref, wua_ref, wda_ref, 0)
                      + xg[:, d + 1:d + 2] * ffn(wgb_ref, wub_ref, wdb_ref, 3))

    @pl.when((count_ref[i] == 0) & (i > 0) & (count_ref[prev] > 0))
    def _():
        base_prev = first_ref[prev]
        n_prev = count_ref[prev]
        for_rows(lambda r: scatter_copy(1 - slot, r, scatter_row(1 - slot, base_prev, n_prev, r)).start())

    @pl.when((i == n_tiles - 1) & (count_ref[prev] > 0))
    def _():
        wait_scatter(1 - slot)


def _moe(l, hx, route, prep):
    t = hx.shape[0]
    d = D_MODEL
    tm = MOE_TILE
    n_tiles = t // tm + N_COMBOS

    combo = route[0]
    order = jnp.argsort(combo).astype(jnp.int32)
    order_padded = jnp.concatenate([order, jnp.zeros((tm,), jnp.int32)])
    counts = jnp.sum(combo[:, None] == jnp.arange(N_COMBOS, dtype=jnp.int32)[None, :], axis=0).astype(jnp.int32)
    tiles_of = (counts + tm - 1) // tm
    tile_end = jnp.cumsum(tiles_of)
    start = jnp.cumsum(counts) - counts
    tile = jnp.arange(n_tiles, dtype=jnp.int32)
    n_used = tile_end[-1]
    tile_combo = jnp.sum(jnp.minimum(tile, n_used - 1)[:, None] >= tile_end[None, :], axis=1).astype(jnp.int32)
    onehot = (tile_combo[:, None] == jnp.arange(N_COMBOS, dtype=jnp.int32)[None, :]).astype(jnp.int32)
    pick = lambda table: jnp.sum(onehot * table[None, :], axis=1)
    within = (tile - (pick(tile_end) - pick(tiles_of))) * tm
    first = jnp.clip(pick(start) + within, 0, t - 1).astype(jnp.int32)
    count = jnp.where(tile < n_used, jnp.clip(pick(counts) - within, 0, tm), 0).astype(jnp.int32)
    grp = tile_combo // N_PAIRS
    pr = tile_combo % N_PAIRS
    pair_a = jnp.sum((pr[:, None] == jnp.arange(N_PAIRS)[None, :]) * jnp.asarray(PAIR_A, jnp.int32)[None, :], axis=1)
    pair_b = jnp.sum((pr[:, None] == jnp.arange(N_PAIRS)[None, :]) * jnp.asarray(PAIR_B, jnp.int32)[None, :], axis=1)
    ea = (grp * EXPERTS_PER_GROUP + pair_a).astype(jnp.int32)
    eb = (grp * EXPERTS_PER_GROUP + pair_b).astype(jnp.int32)

    wa = lambda i, o_r, f_r, c_r, ea_r, eb_r: (l, ea_r[i], 0, 0)
    wb = lambda i, o_r, f_r, c_r, ea_r, eb_r: (l, eb_r[i], 0, 0)
    up = pl.BlockSpec((1, 1, d, D_EXPERT), wa)
    dn = pl.BlockSpec((1, 1, D_EXPERT, d), wa)
    upb = pl.BlockSpec((1, 1, d, D_EXPERT), wb)
    dnb = pl.BlockSpec((1, 1, D_EXPERT, d), wb)
    any_spec = pl.BlockSpec(memory_space=pl.ANY)
    return pl.pallas_call(
        functools.partial(_moe_kernel, n_tiles),
        grid_spec=pltpu.PrefetchScalarGridSpec(
            num_scalar_prefetch=5,
            grid=(n_tiles,),
            in_specs=[any_spec, up, up, dn, upb, upb, dnb],
            out_specs=any_spec,
            scratch_shapes=[pltpu.VMEM((2, tm, d + GATE_COLS), F32), pltpu.VMEM((2, tm, d), F32),
                            pltpu.SemaphoreType.DMA((2,)), pltpu.SemaphoreType.DMA((2,))]),
        out_shape=jax.ShapeDtypeStruct((t + 2 * tm, d), F32),
        compiler_params=_cparams(("arbitrary",)),
        name="moe",
    )(order_padded, first, count, ea, eb, hx, prep["w_eg"], prep["w_eu"], prep["w_ed"],
      prep["w_eg"], prep["w_eu"], prep["w_ed"])


def _resid_kernel(x_ref, y_ref, mod_ref, o_ref):
    d = D_MODEL
    o_ref[...] = x_ref[...] + mod_ref[0][:, 5 * d:6 * d] * y_ref[...].astype(F32)


def _residual(x, y, mod, seq, is_sample):
    t, d = x.shape
    tt = TOKEN_TILE
    tiles_per_batch = seq // tt
    if is_sample:
        mod_map = lambda i: (1 + i // tiles_per_batch, 0, 0)
    else:
        mod_map = lambda i: (0, 0, 0)
    row = lambda i: (i, 0)
    return pl.pallas_call(
        _resid_kernel,
        grid=(t // tt,),
        in_specs=[pl.BlockSpec((tt, d), row), pl.BlockSpec((tt, d), row), pl.BlockSpec((1, 1, 6 * d), mod_map)],
        out_specs=pl.BlockSpec((tt, d), row),
        out_shape=jax.ShapeDtypeStruct((t, d), F32),
        compiler_params=_cparams(("arbitrary",)),
        name="residual",
    )(x, y, mod)


def _rope_tables(length):
    rows = length // GRID_W
    row = jnp.repeat(jnp.arange(rows, dtype=F32), GRID_W)
    col = jnp.tile(jnp.arange(GRID_W, dtype=F32), rows)
    inv = ROPE_BASE ** (-jnp.arange(0, ROPE_HALF, 2, dtype=F32) / ROPE_HALF)
    ang_r = row[:, None] * inv[None, :]
    ang_c = col[:, None] * inv[None, :]
    cos = jnp.concatenate([jnp.cos(ang_r), jnp.cos(ang_r), jnp.cos(ang_c), jnp.cos(ang_c)], axis=-1)
    sin = jnp.concatenate([-jnp.sin(ang_r), jnp.sin(ang_r), -jnp.sin(ang_c), jnp.sin(ang_c)], axis=-1)
    reps = DIFF_WIDTH // DIFF_HEAD_DIM
    return jnp.tile(cos, (1, reps)), jnp.tile(sin, (1, reps))


def _prepare(g_norm1, g_norm2, w_in, w_gate_fwd, b_gate_fwd, w_gate_bwd, b_gate_bwd, g_q_norm, g_k_norm,
             w_out, w_router, b_router, w_exp_gate, w_exp_up, w_exp_down):
    depth, d, _ = w_in.shape
    r = GLA_GATE_RANK
    gate_cols = jnp.pad(w_in[..., 2048:2048 + 2 * r], ((0, 0), (0, 0), (0, GATE_COLS - 2 * r)))
    w_in_r = jnp.concatenate([w_in[..., :2048], w_in[..., 2048 + 2 * r:], gate_cols], axis=-1).astype(BF16)
    w_gate = jnp.zeros((depth, GATE_COLS, 2 * GLA_WIDTH), F32)
    w_gate = w_gate.at[:, 0:r, 0:GLA_WIDTH].set(w_gate_fwd).at[:, r:2 * r, GLA_WIDTH:].set(w_gate_bwd).astype(BF16)
    b_gate = jnp.concatenate([b_gate_fwd, b_gate_bwd], axis=-1).reshape(depth, 1, 2 * GLA_WIDTH)
    grp = jnp.arange(DIFF_WIDTH) // DIFF_HEAD_DIM
    gmat = jnp.where(grp[:, None] == grp[None, :], 1.0 / DIFF_HEAD_DIM, 0.0).astype(BF16)
    reps = DIFF_WIDTH // DIFF_HEAD_DIM
    tok = jnp.arange(TOKEN_TILE)
    same_chunk = (tok[:, None] // GLA_CHUNK) == (tok[None, :] // GLA_CHUNK)
    tri = jnp.stack([same_chunk & (tok[:, None] >= tok[None, :]),
                     same_chunk & (tok[:, None] <= tok[None, :])]).astype(BF16)
    return {
        "tri": tri,
        "g1": g_norm1.reshape(depth, 1, d),
        "g2": g_norm2.reshape(depth, 1, d),
        "w_in": w_in_r,
        "w_gate": w_gate,
        "b_gate": b_gate,
        "gmat": gmat,
        "gq": jnp.tile(g_q_norm, (1, reps)).reshape(depth, 1, DIFF_WIDTH),
        "gk": jnp.tile(g_k_norm, (1, reps)).reshape(depth, 1, DIFF_WIDTH),
        "w_out": w_out.astype(BF16),
        "w_router_t": w_router.T,
        "b_router": b_router.reshape(N_EXPERTS, 1),
        "w_eg": w_exp_gate.astype(BF16),
        "w_eu": w_exp_up.astype(BF16),
        "w_ed": w_exp_down.astype(BF16),
    }


def kernel(x_prompt, x_sample, cache_k, cache_v, state_gla_fwd, state_gla_bwd, c, c_ctx, w_mod, b_mod, g_norm1,
           g_norm2, w_in, w_gate_fwd, b_gate_fwd, w_gate_bwd, b_gate_bwd, g_gla_out, g_q_norm, g_k_norm, lambda_q1,
           lambda_k1, lambda_q2, lambda_k2, g_diff_out, w_out, w_router, b_router, w_exp_gate, w_exp_up, w_exp_down):
    nb_p, seq_p, d = x_prompt.shape
    nb_s, seq_s, _ = x_sample.shape
    depth = w_in.shape[0]
    past = cache_k.shape[2]

    prep = _prepare(g_norm1, g_norm2, w_in, w_gate_fwd, b_gate_fwd, w_gate_bwd, b_gate_bwd, g_q_norm, g_k_norm,
                    w_out, w_router, b_router, w_exp_gate, w_exp_up, w_exp_down)
    rope = _rope_tables(seq_s)

    mod_rows = 8 * ((1 + nb_s + 7) // 8)
    cs = jnp.zeros((mod_rows, d), F32).at[0].set(c_ctx).at[1:1 + nb_s].set(c)
    mod_all = _modulation(cs, w_mod, b_mod)
    ctx_k = cache_k.reshape(nb_s, depth, past, DIFF_WIDTH)
    ctx_v = cache_v.reshape(nb_s, depth, past, DIFF_WIDTH)
    gg = g_gla_out.reshape(depth, 1, GLA_DV)
    gd = g_diff_out.reshape(depth, DIFF_V_DIM, 1)
    lams = [a.reshape(depth, 1, DIFF_HEAD_DIM) for a in (lambda_q1, lambda_k1, lambda_q2, lambda_k2)]

    xp = x_prompt.reshape(nb_p * seq_p, d)
    xs = x_sample.reshape(nb_s * seq_s, d)
    yp = ys = None
    mod_prev = None
    new_k, new_v, new_sf, new_sb = [], [], [], []
    for l in range(depth):
        lam_init = 0.8 - 0.6 * math.exp(-0.3 * l)
        mod = mod_all[l].reshape(mod_rows, 1, 6 * d)
        xp, gla_in, la, qt, kd, vt, k_l, v_l = _pre_mixer(l, xp, yp, mod_prev, mod, prep, None, nb_p, seq_p, False)
        og, sf_l, sb_l = _gla(l, gla_in, la, gg, None, None, nb_p, seq_p)
        od = _diff_attention(l, lam_init, qt, kd, vt, None, None, lams, gd, nb_p, seq_p)
        xp, h2, route = _post_mixer(l, og, od, xp, mod, prep, seq_p, False)
        yp = _moe(l, h2, route, prep)
        new_k.append(k_l)
        new_v.append(v_l)
        new_sf.append(sf_l)
        new_sb.append(sb_l)
        xs, gla_in, la, qt, kd, vt = _pre_mixer(l, xs, ys, mod_prev, mod, prep, rope, nb_s, seq_s, True)
        og, _, _ = _gla(l, gla_in, la, gg, state_gla_fwd, state_gla_bwd, nb_s, seq_s)
        od = _diff_attention(l, lam_init, qt, kd, vt, ctx_k, ctx_v, lams, gd, nb_s, seq_s)
        xs, h2, route = _post_mixer(l, og, od, xs, mod, prep, seq_s, True)
        ys = _moe(l, h2, route, prep)
        mod_prev = mod
    xp = _residual(xp, yp, mod_prev, seq_p, False)
    xs = _residual(xs, ys, mod_prev, seq_s, True)

    new_cache_k = jnp.concatenate(new_k, axis=1).reshape(nb_p, depth, seq_p, DIFF_HEADS, 2, DIFF_HEAD_DIM)
    new_cache_v = jnp.concatenate(new_v, axis=1).reshape(nb_p, depth, seq_p, DIFF_HEADS, DIFF_V_DIM)
    new_sf = jnp.stack(new_sf, axis=1)
    new_sb = jnp.stack(new_sb, axis=1)
    return (xp.reshape(nb_p, seq_p, d), xs.reshape(nb_s, seq_s, d), new_cache_k, new_cache_v, new_sf, new_sb)
```

```python
import functools
import math

import jax
import jax.numpy as jnp
from jax import lax
from jax.experimental import pallas as pl
from jax.experimental.pallas import tpu as pltpu

F32 = jnp.float32
BF16 = jnp.bfloat16

D_MODEL = 1024
GLA_HEADS = 4
GLA_DK = 128
GLA_DV = 128
GLA_WIDTH = GLA_HEADS * GLA_DK
GLA_GATE_RANK = 16
GLA_GATE_NORMALIZER = 16.0
GLA_CHUNK = 64
GLA_HEADS_PER_STEP = 4
DIFF_HEADS = 4
DIFF_HEAD_DIM = 64
DIFF_V_DIM = 128
DIFF_WIDTH = DIFF_HEADS * 2 * DIFF_HEAD_DIM
ROPE_HALF = DIFF_HEAD_DIM // 2
ROPE_BASE = 10000.0
GRID_W = 64
N_EXPERTS = 16
N_GROUPS = 4
EXPERTS_PER_GROUP = 4
D_EXPERT = 512
NORM_EPS = 1e-6
LOG2E = 1.4426950408889634

PAIR_A = (0, 0, 0, 1, 1, 3)
PAIR_B = (1, 2, 3, 3, 2, 2)
N_PAIRS = len(PAIR_A)
N_COMBOS = N_GROUPS * N_PAIRS

GATE_COLS = 128
MAIN_COLS = 7 * 512
TOKEN_TILE = 256
POST_TILE = 1024
PRE_TILE = 512
MOE_TILE = 256
DMA_UNROLL = 8
ATTN_TQ = 1024
ATTN_TK = 256
ATTN_GROUP = 128
ATTN_UNROLL = 8
ONES_ROWS = 16
VMEM_LIMIT = 56 * 1024 * 1024


def _cparams(sem):
    return pltpu.CompilerParams(dimension_semantics=sem, vmem_limit_bytes=VMEM_LIMIT)


def _dot(a, b):
    return jnp.dot(a, b, preferred_element_type=F32)


def _dot_nt(a, b):
    return lax.dot_general(a, b, (((1,), (1,)), ((), ())), preferred_element_type=F32)


def _dot_tn(a, b):
    return lax.dot_general(a, b, (((0,), (0,)), ((), ())), preferred_element_type=F32)


def _sigmoid(x):
    return 1.0 / (1.0 + jnp.exp(-x))


def _split_bf16(x):
    hi = x.astype(BF16)
    lo = (x - hi.astype(F32)).astype(BF16)
    return hi, lo


def _mod_kernel(c_ref, w_ref, b_ref, o_ref):
    c = c_ref[...]
    s = c * _sigmoid(c)
    o_ref[0] = jnp.dot(s, w_ref[0], preferred_element_type=F32,
                       precision=lax.Precision.HIGHEST) + b_ref[0]


def _modulation(cs, w_mod, b_mod):
    depth, d, n = w_mod.shape
    rows = cs.shape[0]
    tn = 1536
    return pl.pallas_call(
        _mod_kernel,
        grid=(depth, n // tn),
        in_specs=[pl.BlockSpec((rows, d), lambda l, j: (0, 0)),
                  pl.BlockSpec((1, d, tn), lambda l, j: (l, 0, j)),
                  pl.BlockSpec((1, 1, tn), lambda l, j: (l, 0, j))],
        out_specs=pl.BlockSpec((1, rows, tn), lambda l, j: (l, 0, j)),
        out_shape=jax.ShapeDtypeStruct((depth, rows, n), F32),
        compiler_params=_cparams(("arbitrary", "arbitrary")),
        name="modulation",
    )(cs, w_mod, b_mod.reshape(depth, 1, n))


def _rope(y, cos, sin):
    w = y.shape[1]
    lane = lax.broadcasted_iota(jnp.int32, y.shape, 1)
    first = (lane % (2 * (ROPE_HALF // 2))) < (ROPE_HALF // 2)
    partner = jnp.where(first, pltpu.roll(y, w - ROPE_HALF // 2, axis=1), pltpu.roll(y, ROPE_HALF // 2, axis=1))
    return y * cos + partner * sin


def _pre_kernel(has_moe, is_sample, *refs):
    refs = list(refs)
    x_ref = refs.pop(0)
    if has_moe:
        y_ref, modp_ref = refs[:2]
        refs = refs[2:]
    mod_ref, g1_ref, win_ref, wgate_ref, bgate_ref, tri_ref, gmat_ref, gq_ref, gk_ref = refs[:9]
    refs = refs[9:]
    if is_sample:
        cos_ref, sin_ref = refs[:2]
        refs = refs[2:]
    xo_ref = refs.pop(0) if has_moe else None
    gla_ref, la_ref, qt_ref, kd_ref, vt_ref = refs[:5]
    refs = refs[5:]
    if not is_sample:
        kc_ref, vc_ref = refs

    d = D_MODEL
    sub = TOKEN_TILE
    m = mod_ref[0]
    gs = g1_ref[0] * (1.0 + m[:, d:2 * d])
    scale = DIFF_HEAD_DIM ** -0.5 * LOG2E
    w = GLA_WIDTH
    subs = [slice(j * sub, (j + 1) * sub) for j in range(x_ref.shape[0] // sub)]
    hbs = []
    for rs in subs:
        x = x_ref[rs, :]
        if has_moe:
            x = x + modp_ref[0][:, 5 * d:6 * d] * y_ref[rs, :].astype(F32)
            xo_ref[rs, :] = x
        ms = jnp.mean(x * x, axis=-1, keepdims=True)
        hbs.append((x * lax.rsqrt(ms + NORM_EPS) * gs + m[:, 0:d]).astype(BF16))

    def put_t(ref, j, rs, val):
        if is_sample:
            ref[0, :, rs] = val.astype(BF16).T
        else:
            ref[j] = val.astype(BF16).T

    hb_all = jnp.concatenate(hbs, axis=0)

    def plain(c):
        p = _dot(hb_all, win_ref[0, :, c * 512:(c + 1) * 512])
        if c == 0:
            p = p * (GLA_DK ** -0.5)
        gla_ref[:, c * 512:(c + 1) * 512] = p.astype(BF16)

    q1 = [_dot(hb, win_ref[0, :, 2048:2560]) for hb in hbs]
    k1 = [_dot(hb, win_ref[0, :, 2560:3072]) for hb in hbs]
    a1 = [_dot(hb, win_ref[0, :, MAIN_COLS:MAIN_COLS + GATE_COLS]) for hb in hbs]
    plain(0)
    msq = [_dot((p * p).astype(BF16), gmat_ref[...]) for p in q1]
    msk = [_dot((p * p).astype(BF16), gmat_ref[...]) for p in k1]
    zs = [_dot(a.astype(BF16), wgate_ref[0]) + bgate_ref[0] for a in a1]
    plain(1)
    for rs, z in zip(subs, zs):
        la = (jnp.minimum(z, 0.0) - jnp.log1p(jnp.exp(-jnp.abs(z)))) * (1.0 / GLA_GATE_NORMALIZER)
        la_hi, la_lo = _split_bf16(la)
        la_ref[rs, 0:w] = _dot(tri_ref[0], la_hi[:, 0:w]) + _dot(tri_ref[0], la_lo[:, 0:w])
        la_ref[rs, w:2 * w] = _dot(tri_ref[1], la_hi[:, w:2 * w]) + _dot(tri_ref[1], la_lo[:, w:2 * w])
    for j, rs in enumerate(subs):
        q = q1[j] * lax.rsqrt(msq[j] + NORM_EPS) * gq_ref[0]
        k = k1[j] * lax.rsqrt(msk[j] + NORM_EPS) * gk_ref[0]
        if is_sample:
            q = _rope(q, cos_ref[rs, :], sin_ref[rs, :])
            k = _rope(k, cos_ref[rs, :], sin_ref[rs, :])
        else:
            kc_ref[j, 0] = k
        put_t(qt_ref, j, rs, q * scale)
        kd_ref[rs, :] = k.astype(BF16)
    plain(2)
    for j, (rs, hb) in enumerate(zip(subs, hbs)):
        v = _dot(hb, win_ref[0, :, 3072:3584])
        if not is_sample:
            vc_ref[j, 0] = v
        put_t(vt_ref, j, rs, v)
    plain(3)


def _pre_mixer(l, x, y, mod_prev, mod, prep, rope, nb, seq, is_sample):
    t, d = x.shape
    sub = TOKEN_TILE
    tt = PRE_TILE
    has_moe = y is not None
    if is_sample:
        assert seq % tt == 0
        tiles_per_batch = seq // tt
        mod_map = lambda i: (1 + i // tiles_per_batch, 0, 0)
    else:
        assert seq == sub and t % tt == 0
        mod_map = lambda i: (0, 0, 0)
    row = lambda i: (i, 0)
    const2 = lambda i: (0, 0)
    lay3 = lambda i: (l, 0, 0)

    ins = [x]
    in_specs = [pl.BlockSpec((tt, d), row)]
    if has_moe:
        ins += [y, mod_prev]
        in_specs += [pl.BlockSpec((tt, d), row), pl.BlockSpec((1, 1, 6 * d), mod_map)]
    ins += [mod, prep["g1"], prep["w_in"], prep["w_gate"], prep["b_gate"], prep["tri"], prep["gmat"], prep["gq"],
            prep["gk"]]
    in_specs += [pl.BlockSpec((1, 1, 6 * d), mod_map),
                 pl.BlockSpec((1, 1, d), lay3),
                 pl.BlockSpec((1, d, MAIN_COLS + GATE_COLS), lay3, pipeline_mode=pl.Buffered(1)),
                 pl.BlockSpec((1, GATE_COLS, 2 * GLA_WIDTH), lay3),
                 pl.BlockSpec((1, 1, 2 * GLA_WIDTH), lay3),
                 pl.BlockSpec((2, sub, sub), lambda i: (0, 0, 0)),
                 pl.BlockSpec((DIFF_WIDTH, DIFF_WIDTH), const2),
                 pl.BlockSpec((1, 1, DIFF_WIDTH), lay3),
                 pl.BlockSpec((1, 1, DIFF_WIDTH), lay3)]
    if is_sample:
        ins += [rope[0], rope[1]]
        in_specs += [pl.BlockSpec((tt, DIFF_WIDTH), lambda i: (i % tiles_per_batch, 0))] * 2

    out_shape = []
    out_specs = []
    if has_moe:
        out_shape.append(jax.ShapeDtypeStruct((t, d), F32))
        out_specs.append(pl.BlockSpec((tt, d), row))
    if is_sample:
        tr_spec = pl.BlockSpec((1, DIFF_WIDTH, tt), lambda i: (i // tiles_per_batch, 0, i % tiles_per_batch))
    else:
        tr_spec = pl.BlockSpec((tt // sub, DIFF_WIDTH, seq), lambda i: (i, 0, 0))
    out_shape += [jax.ShapeDtypeStruct((t, 4 * GLA_WIDTH), BF16),
                  jax.ShapeDtypeStruct((t, 2 * GLA_WIDTH), F32),
                  jax.ShapeDtypeStruct((nb, DIFF_WIDTH, seq), BF16),
                  jax.ShapeDtypeStruct((t, DIFF_WIDTH), BF16),
                  jax.ShapeDtypeStruct((nb, DIFF_WIDTH, seq), BF16)]
    out_specs += [pl.BlockSpec((tt, 4 * GLA_WIDTH), row),
                  pl.BlockSpec((tt, 2 * GLA_WIDTH), row),
                  tr_spec,
                  pl.BlockSpec((tt, DIFF_WIDTH), row),
                  tr_spec]
    if not is_sample:
        out_shape += [jax.ShapeDtypeStruct((nb, 1, seq, DIFF_WIDTH), F32)] * 2
        out_specs += [pl.BlockSpec((tt // sub, 1, seq, DIFF_WIDTH), lambda i: (i, 0, 0, 0))] * 2

    outs = pl.pallas_call(
        functools.partial(_pre_kernel, has_moe, is_sample),
        grid=(t // tt,),
        in_specs=in_specs,
        out_specs=out_specs,
        out_shape=out_shape,
        compiler_params=_cparams(("arbitrary",)),
        name="pre_mixer",
    )(*ins)
    outs = list(outs)
    x_new = outs.pop(0) if has_moe else x
    return [x_new] + outs


def _gla_kernel(has_state, n_chunks, hg, *refs):
    refs = list(refs)
    q_ref, k_ref, v_ref, g_ref, bf_ref, bb_ref, gg_ref = refs[:7]
    refs = refs[7:]
    if has_state:
        s0f_ref, s0b_ref = refs[:2]
        refs = refs[2:]
    o_ref, sf_ref, sb_ref, st_ref, acc_ref = refs

    c = GLA_CHUNK
    dk = GLA_DK
    for hd in range(hg):
        if has_state:
            st_ref[2 * hd] = s0f_ref[0, 0, hd].T
            st_ref[2 * hd + 1] = s0b_ref[0, 0, hd].T
        else:
            st_ref[2 * hd] = jnp.zeros((GLA_DV, dk), F32)
            st_ref[2 * hd + 1] = jnp.zeros((GLA_DV, dk), F32)

    r = lax.broadcasted_iota(jnp.int32, (c, c), 0)
    s = lax.broadcasted_iota(jnp.int32, (c, c), 1)
    lower = r >= s
    upper = r <= s

    def scores(rows, hd, b_ref, st_i, mid_row, last_row):
        cs = slice(hd * dk, (hd + 1) * dk)
        q = q_ref[rows, cs].astype(F32)
        k = k_ref[rows, cs].astype(F32)
        v = v_ref[rows, cs]
        b = b_ref[rows, cs]
        mid = b[mid_row:mid_row + 1]
        last = b[last_row:last_row + 1]
        qe = (q * jnp.exp(b - mid)).astype(BF16)
        ke = (k * jnp.exp(mid - b)).astype(BF16)
        qi = (q * jnp.exp(b)).astype(BF16)
        ks = (k * jnp.exp(last - b)).astype(BF16)
        st = st_ref[st_i]
        return _dot_nt(qe, ke), _dot_nt(qi, st.astype(BF16)), _dot_tn(v, ks), v, st, last

    def outputs(sc, mask, st_i):
        att, o_inter, kv, v, st, last = sc
        st_ref[st_i] = st * jnp.exp(last) + kv
        return _dot(jnp.where(mask, att, 0.0).astype(BF16), v) + o_inter

    def finish(o, rows, hd):
        cs = slice(hd * dk, (hd + 1) * dk)
        ms = jnp.mean(o * o, axis=-1, keepdims=True)
        g = g_ref[rows, cs].astype(F32)
        o_ref[rows, cs] = (o * lax.rsqrt(ms + NORM_EPS) * gg_ref[0] * (g * _sigmoid(g))).astype(BF16)

    def step(n, second_visit):
        rf = pl.ds(pl.multiple_of(n * c, c), c)
        rb = pl.ds(pl.multiple_of((n_chunks - 1 - n) * c, c), c)
        sc = []
        for hd in range(hg):
            sc.append(scores(rf, hd, bf_ref, 2 * hd, c // 2, c - 1))
            sc.append(scores(rb, hd, bb_ref, 2 * hd + 1, c - 1 - c // 2, 0))
        for hd in range(hg):
            cs = slice(hd * dk, (hd + 1) * dk)
            for rows, sci, mask, st_i in ((rf, sc[2 * hd], lower, 2 * hd), (rb, sc[2 * hd + 1], upper, 2 * hd + 1)):
                o = outputs(sci, mask, st_i)
                if second_visit:
                    finish(acc_ref[rows, cs] + o, rows, hd)
                else:
                    acc_ref[rows, cs] = o

    half = n_chunks // 2
    lax.fori_loop(0, half, lambda n, carry: (step(n, False), carry)[1], 0)
    lax.fori_loop(half, n_chunks, lambda n, carry: (step(n, True), carry)[1], 0)

    for hd in range(hg):
        sf_ref[0, hd] = st_ref[2 * hd].T
        sb_ref[0, hd] = st_ref[2 * hd + 1].T


def _gla(l, gla_in, bsum, gg, s0f, s0b, nb, seq):
    t = gla_in.shape[0]
    h = GLA_HEADS
    hg = GLA_HEADS_PER_STEP
    nhb = h // hg
    w = hg * GLA_DK
    has_state = s0f is not None
    n_chunks = seq // GLA_CHUNK
    assert n_chunks % 2 == 0
    col = lambda off: (lambda b, hh: (b, off + hh))
    in_bytes = seq * w * (4 * 2 + 2 * 4)
    mode = pl.Buffered(1) if 2 * in_bytes > VMEM_LIMIT // 2 else None
    big = lambda off: pl.BlockSpec((seq, w), col(off), pipeline_mode=mode)
    ins = [gla_in, gla_in, gla_in, gla_in, bsum, bsum, gg]
    in_specs = [big(0), big(nhb), big(2 * nhb), big(3 * nhb), big(0), big(nhb),
                pl.BlockSpec((1, 1, GLA_DV), lambda b, hh: (l, 0, 0))]
    if has_state:
        ins += [s0f, s0b]
        in_specs += [pl.BlockSpec((1, 1, hg, GLA_DK, GLA_DV), lambda b, hh: (b, l, hh, 0, 0))] * 2
    return pl.pallas_call(
        functools.partial(_gla_kernel, has_state, n_chunks, hg),
        grid=(nb, nhb),
        in_specs=in_specs,
        out_specs=[pl.BlockSpec((seq, w), col(0)),
                   pl.BlockSpec((1, hg, GLA_DK, GLA_DV), lambda b, hh: (b, hh, 0, 0)),
                   pl.BlockSpec((1, hg, GLA_DK, GLA_DV), lambda b, hh: (b, hh, 0, 0))],
        out_shape=[jax.ShapeDtypeStruct((t, GLA_WIDTH), BF16),
                   jax.ShapeDtypeStruct((nb, h, GLA_DK, GLA_DV), F32),
                   jax.ShapeDtypeStruct((nb, h, GLA_DK, GLA_DV), F32)],
        scratch_shapes=[pltpu.VMEM((2 * hg, GLA_DV, GLA_DK), F32), pltpu.VMEM((seq, w), F32)],
        compiler_params=_cparams(("arbitrary", "arbitrary")),
        name="gla",
    )(*ins)


def _attn_kernel(lam_init, has_ctx, n_kt, tk, *refs):
    refs = list(refs)
    qt_ref, k_ref, vt_ref = refs[:3]
    refs = refs[3:]
    if has_ctx:
        ck_ref, cv_ref = refs[:2]
        refs = refs[2:]
    lq1_ref, lk1_ref, lq2_ref, lk2_ref, gd_ref, o_ref = refs

    qt = qt_ref[0]
    tq = qt.shape[1]
    gq = ATTN_GROUP
    n_groups = tq // gq
    dim = lax.broadcasted_iota(jnp.int32, (2 * DIFF_HEAD_DIM, gq), 0)
    zero = jnp.zeros((2 * DIFF_HEAD_DIM, gq), BF16)
    qws = []
    for g in range(n_groups):
        qg = qt[:, g * gq:(g + 1) * gq]
        qws.append(jnp.concatenate([jnp.where(dim < DIFF_HEAD_DIM, qg, zero),
                                    jnp.where(dim >= DIFF_HEAD_DIM, qg, zero)], axis=1))

    def scores(k):
        return [_dot(k, qws[g]) for g in range(n_groups)]

    def absorb(state, sts, vt):
        new = []
        for g in range(n_groups):
            m, acc = state[g]
            st = sts[g]
            m_new = jnp.maximum(m, jnp.max(st, axis=0, keepdims=True))
            alpha = jnp.exp2(m - m_new)
            p = jnp.exp2(st - m_new)
            acc = alpha * acc + _dot(vt, p.astype(BF16))
            new.append((m_new, acc))
        return tuple(new)

    def process(state, tiles):
        sts = scores(tiles[0][0])
        for u, (_, vt) in enumerate(tiles):
            nxt = scores(tiles[u + 1][0]) if u + 1 < len(tiles) else None
            state = absorb(state, sts, vt)
            sts = nxt
        return state

    def with_ones(vt):
        return jnp.concatenate([vt, jnp.ones((ONES_ROWS, vt.shape[1]), BF16)], axis=0)

    state = tuple((jnp.full((1, 2 * gq), -jnp.inf, F32), jnp.zeros((DIFF_V_DIM + ONES_ROWS, 2 * gq), F32))
                  for _ in range(n_groups))
    if has_ctx:
        past = ck_ref.shape[2]
        ctk = tk if past % tk == 0 else past
        cvt = cv_ref[0, 0].T.astype(BF16)
        tiles = [(ck_ref[0, 0, j * ctk:(j + 1) * ctk, :].astype(BF16), with_ones(cvt[:, j * ctk:(j + 1) * ctk]))
                 for j in range(past // ctk)]
        state = process(state, tiles)

    unroll = max(u for u in range(1, ATTN_UNROLL + 1) if n_kt % u == 0)

    def body(j, st):
        tiles = []
        for u in range(unroll):
            rr = pl.ds(pl.multiple_of((j * unroll + u) * tk, tk), tk)
            tiles.append((k_ref[rr, :], with_ones(vt_ref[0, :, rr])))
        return process(st, tiles)

    state = lax.fori_loop(0, n_kt // unroll, body, state)

    lam = (jnp.exp(jnp.sum(lq1_ref[0] * lk1_ref[0], axis=-1, keepdims=True))
           - jnp.exp(jnp.sum(lq2_ref[0] * lk2_ref[0], axis=-1, keepdims=True)) + lam_init)
    for g in range(n_groups):
        m, acc = state[g]
        on = acc[:DIFF_V_DIM] / acc[DIFF_V_DIM:DIFF_V_DIM + 1]
        o = on[:, :gq] - lam * on[:, gq:]
        ms = jnp.mean(o * o, axis=0, keepdims=True)
        o = o * lax.rsqrt(ms + NORM_EPS) * gd_ref[0] * (1.0 - lam_init)
        o_ref[g * gq:(g + 1) * gq, :] = o.T.astype(BF16)


def _diff_attention(l, lam_init, qt, kd, vt, ctx_k, ctx_v, lams, gd, nb, seq):
    t = kd.shape[0]
    h = DIFF_HEADS
    has_ctx = ctx_k is not None
    tq = min(ATTN_TQ, seq)
    tk = min(ATTN_TK, seq)
    nq = seq // tq
    w = 2 * DIFF_HEAD_DIM
    ins = [qt, kd, vt]
    in_specs = [pl.BlockSpec((1, w, tq), lambda b, hh, i: (b, hh, i)),
                pl.BlockSpec((seq, w), lambda b, hh, i: (b, hh)),
                pl.BlockSpec((1, DIFF_V_DIM, seq), lambda b, hh, i: (b, hh, 0))]
    if has_ctx:
        past = ctx_k.shape[2]
        ins += [ctx_k, ctx_v]
        in_specs += [pl.BlockSpec((1, 1, past, w), lambda b, hh, i: (b, l, 0, hh)),
                     pl.BlockSpec((1, 1, past, DIFF_V_DIM), lambda b, hh, i: (b, l, 0, hh))]
    lay3 = lambda b, hh, i: (l, 0, 0)
    ins += list(lams) + [gd]
    in_specs += [pl.BlockSpec((1, 1, DIFF_HEAD_DIM), lay3)] * 4 + [pl.BlockSpec((1, DIFF_V_DIM, 1), lay3)]
    return pl.pallas_call(
        functools.partial(_attn_kernel, lam_init, has_ctx, seq // tk, tk),
        grid=(nb, h, nq),
        in_specs=in_specs,
        out_specs=pl.BlockSpec((tq, DIFF_V_DIM), lambda b, hh, i: (b * nq + i, hh)),
        out_shape=jax.ShapeDtypeStruct((t, DIFF_WIDTH), BF16),
        compiler_params=_cparams(("arbitrary", "arbitrary", "arbitrary")),
        name="diff_attention",
    )(*ins)


def _top2_of4(v):
    m1 = jnp.maximum(jnp.maximum(v[0], v[1]), jnp.maximum(v[2], v[3]))
    i1 = jnp.where(v[0] == m1, 0, jnp.where(v[1] == m1, 1, jnp.where(v[2] == m1, 2, 3)))
    neg = jnp.full_like(m1, -jnp.inf)
    w = [jnp.where(i1 == j, neg, v[j]) for j in range(4)]
    m2 = jnp.maximum(jnp.maximum(w[0], w[1]), jnp.maximum(w[2], w[3]))
    i2 = jnp.where(w[0] == m2, 0, jnp.where(w[1] == m2, 1, jnp.where(w[2] == m2, 2, 3)))
    return m1, i1, m2, i2


def _post_kernel(og_ref, od_ref, x_ref, mod_ref, g2_ref, wo_ref, wr_ref, br_ref, x1_ref, hx_ref, route_ref):
    d = D_MODEL
    sub = TOKEN_TILE
    n_sub = x_ref.shape[0] // sub
    m = mod_ref[0]
    rows = [slice(j * sub, (j + 1) * sub) for j in range(n_sub)]
    outs = [_dot(og_ref[rs, :], wo_ref[0, 0:GLA_WIDTH, :]) + _dot(od_ref[rs, :], wo_ref[0, GLA_WIDTH:, :])
            for rs in rows]
    w_hi, w_lo = _split_bf16(wr_ref[...])
    w_hl = jnp.concatenate([w_hi, w_lo], axis=0)
    zs = []
    for rs, out in zip(rows, outs):
        x1 = x_ref[rs, :] + m[:, 2 * d:3 * d] * out
        x1_ref[rs, :] = x1
        ms = jnp.mean(x1 * x1, axis=-1, keepdims=True)
        h2 = x1 * lax.rsqrt(ms + NORM_EPS) * (g2_ref[0] * (1.0 + m[:, 4 * d:5 * d])) + m[:, 3 * d:4 * d]
        h_hi, h_lo = _split_bf16(h2)
        hx_ref[rs, 0:d] = h2
        za = _dot_nt(w_hl, h_hi)
        zs.append(za[0:N_EXPERTS] + za[N_EXPERTS:2 * N_EXPERTS] + _dot_nt(w_hi, h_lo))
    for rs, z in zip(rows, zs):
        _route(z, br_ref, hx_ref, route_ref, rs)


def _route(z, br_ref, hx_ref, route_ref, rs):
    d = D_MODEL
    s = _sigmoid(z)
    sel = s + br_ref[...]

    e = EXPERTS_PER_GROUP
    tops = []
    for g in range(N_GROUPS):
        tops.append(_top2_of4([sel[g * e + j:g * e + j + 1, :] for j in range(e)]))
    score = [t[0] + t[2] for t in tops]
    best = jnp.maximum(jnp.maximum(score[0], score[1]), jnp.maximum(score[2], score[3]))
    gi = jnp.where(score[0] == best, 0, jnp.where(score[1] == best, 1, jnp.where(score[2] == best, 2, 3)))

    def pick(rows):
        return jnp.where(gi == 0, rows[0], jnp.where(gi == 1, rows[1], jnp.where(gi == 2, rows[2], rows[3])))

    i1 = pick([t[1] for t in tops])
    i2 = pick([t[3] for t in tops])

    def gate_of(idx):
        per_group = []
        for g in range(N_GROUPS):
            rows = [s[g * e + j:g * e + j + 1, :] for j in range(e)]
            per_group.append(jnp.where(idx == 0, rows[0], jnp.where(idx == 1, rows[1],
                                                                     jnp.where(idx == 2, rows[2], rows[3]))))
        return pick(per_group)

    s1 = gate_of(i1)
    s2 = gate_of(i2)
    tot = s1 + s2
    w1 = s1 / tot
    w2 = s2 / tot
    lo = jnp.minimum(i1, i2)
    hi = jnp.maximum(i1, i2)
    w_lo = jnp.where(i1 < i2, w1, w2)
    w_hi = jnp.where(i1 < i2, w2, w1)
    pair = jnp.where(lo == 0, hi - 1, jnp.where(lo == 1, jnp.where(hi == 3, 3, 4), 5))
    swapped = pair == 5
    gate_a = jnp.where(swapped, w_hi, w_lo)
    gate_b = jnp.where(swapped, w_lo, w_hi)
    route_ref[:, rs] = jnp.broadcast_to(gi * N_PAIRS + pair, (route_ref.shape[0], gate_a.shape[1]))
    r = lax.broadcasted_iota(jnp.int32, (GATE_COLS, gate_a.shape[1]), 0)
    gates_t = jnp.where(r == 0, gate_a, jnp.where(r == 1, gate_b, 0.0))
    hx_ref[rs, d:d + GATE_COLS] = gates_t.T


def _post_mixer(l, og, od, x, mod, prep, seq, is_sample):
    t, d = x.shape
    tt = POST_TILE if (seq % POST_TILE == 0 or not is_sample) and t % POST_TILE == 0 else TOKEN_TILE
    tiles_per_batch = max(seq // tt, 1)
    if is_sample:
        mod_map = lambda i: (1 + i // tiles_per_batch, 0, 0)
    else:
        mod_map = lambda i: (0, 0, 0)
    row = lambda i: (i, 0)
    lay3 = lambda i: (l, 0, 0)
    return pl.pallas_call(
        _post_kernel,
        grid=(t // tt,),
        in_specs=[pl.BlockSpec((tt, GLA_WIDTH), row),
                  pl.BlockSpec((tt, DIFF_WIDTH), row),
                  pl.BlockSpec((tt, d), row),
                  pl.BlockSpec((1, 1, 6 * d), mod_map),
                  pl.BlockSpec((1, 1, d), lay3),
                  pl.BlockSpec((1, d, d), lay3),
                  pl.BlockSpec((N_EXPERTS, d), lambda i: (0, 0)),
                  pl.BlockSpec((N_EXPERTS, 1), lambda i: (0, 0))],
        out_specs=[pl.BlockSpec((tt, d), row),
                   pl.BlockSpec((tt, d + GATE_COLS), row),
                   pl.BlockSpec((8, tt), lambda i: (0, i))],
        out_shape=[jax.ShapeDtypeStruct((t, d), F32),
                   jax.ShapeDtypeStruct((t, d + GATE_COLS), F32),
                   jax.ShapeDtypeStruct((8, t), jnp.int32)],
        compiler_params=_cparams(("arbitrary",)),
        name="post_mixer",
    )(og, od, x, mod, prep["g2"], prep["w_out"], prep["w_router_t"], prep["b_router"])


def _moe_kernel(n_tiles, order_ref, first_ref, count_ref, ea_ref, eb_ref,
                hx_hbm, wga_ref, wua_ref, wda_ref, wgb_ref, wub_ref, wdb_ref,
                y_hbm, xbuf, ybuf, in_sem, out_sem):
    i = pl.program_id(0)
    slot = i % 2
    d = D_MODEL
    tm = xbuf.shape[1]
    n_tokens = hx_hbm.shape[0]

    def gather_copy(buf_slot, r, tok):
        return pltpu.make_async_copy(hx_hbm.at[pl.ds(tok, 1), :], xbuf.at[buf_slot, pl.ds(r, 1), :],
                                     in_sem.at[buf_slot])

    def scatter_copy(buf_slot, r, row):
        return pltpu.make_async_copy(ybuf.at[buf_slot, pl.ds(r, 1), :], y_hbm.at[pl.ds(row, 1), :],
                                     out_sem.at[buf_slot])

    def for_rows(fn):
        def body(q, c):
            for s in range(DMA_UNROLL):
                fn(q * DMA_UNROLL + s)
            return c
        lax.fori_loop(0, tm // DMA_UNROLL, body, 0)

    def wait_gather(buf_slot):
        for_rows(lambda r: gather_copy(buf_slot, r, 0).wait())

    def wait_scatter(buf_slot):
        for_rows(lambda r: scatter_copy(buf_slot, r, 0).wait())

    def scatter_row(buf_slot, base, n, r):
        return jnp.where(r < n, order_ref[base + r], n_tokens + buf_slot * tm + r)

    prev = jnp.maximum(i - 1, 0)

    @pl.when(i == 0)
    def _():
        ybuf[...] = jnp.zeros_like(ybuf)
        for s in range(2):
            spare_init = pltpu.make_async_copy(ybuf.at[s], y_hbm.at[pl.ds(n_tokens + s * tm, tm), :], out_sem.at[s])
            spare_init.start()
            spare_init.wait()
        base0 = first_ref[0]
        for_rows(lambda r: gather_copy(0, r, order_ref[base0 + r]).start())

    @pl.when((i == 0) | (count_ref[prev] > 0))
    def _():
        wait_gather(slot)

    @pl.when((i == 1) | ((i >= 2) & (count_ref[jnp.maximum(i - 2, 0)] > 0)))
    def _():
        wait_scatter(slot)

    @pl.when(count_ref[i] > 0)
    def _():
        xg = xbuf[slot]
        x = xg[:, 0:d].astype(BF16)
        base_next = first_ref[i + 1]
        base_prev = first_ref[prev]
        n_prev = jnp.where(i > 0, count_ref[prev], 0)
        per_dot = -(-tm // 6)

        def issue(part):
            for r in range(part * per_dot, min((part + 1) * per_dot, tm)):
                gather_copy(1 - slot, r, order_ref[base_next + r]).start()
                scatter_copy(1 - slot, r, scatter_row(1 - slot, base_prev, n_prev, r)).start()

        hg_a = _dot(x, wga_ref[0, 0])
        issue(0)
        hu_a = _dot(x, wua_ref[0, 0])
        issue(1)
        hg_b = _dot(x, wgb_ref[0, 0])
        issue(2)
        hu_b = _dot(x, wub_ref[0, 0])
        issue(3)
        out_a = _dot(((hg_a * _sigmoid(hg_a)) * hu_a).astype(BF16), wda_ref[0, 0])
        issue(4)
        out_b = _dot(((hg_b * _sigmoid(hg_b)) * hu_b).astype(BF16), wdb_ref[0, 0])
        issue(5)
        ybuf[slot] = xg[:, d:d + 1] * out_a + xg[:, d + 1:d + 2] * out_b

    @pl.when((count_ref[i] == 0) & (i > 0) & (count_ref[prev] > 0))
    def _():
        base_prev = first_ref[prev]
        n_prev = count_ref[prev]
        for_rows(lambda r: scatter_copy(1 - slot, r, scatter_row(1 - slot, base_prev, n_prev, r)).start())

    @pl.when((i == n_tiles - 1) & (count_ref[prev] > 0))
    def _():
        wait_scatter(1 - slot)


def _moe(l, hx, route, prep):
    t = hx.shape[0]
    d = D_MODEL
    tm = MOE_TILE
    n_tiles = t // tm + N_COMBOS

    combo = route[0]
    order = jnp.argsort(combo).astype(jnp.int32)
    order_padded = jnp.concatenate([order, jnp.zeros((tm,), jnp.int32)])
    counts = jnp.sum(combo[:, None] == jnp.arange(N_COMBOS, dtype=jnp.int32)[None, :], axis=0).astype(jnp.int32)
    tiles_of = (counts + tm - 1) // tm
    tile_end = jnp.cumsum(tiles_of)
    start = jnp.cumsum(counts) - counts
    tile = jnp.arange(n_tiles, dtype=jnp.int32)
    n_used = tile_end[-1]
    tile_combo = jnp.sum(jnp.minimum(tile, n_used - 1)[:, None] >= tile_end[None, :], axis=1).astype(jnp.int32)
    onehot = (tile_combo[:, None] == jnp.arange(N_COMBOS, dtype=jnp.int32)[None, :]).astype(jnp.int32)
    pick = lambda table: jnp.sum(onehot * table[None, :], axis=1)
    within = (tile - (pick(tile_end) - pick(tiles_of))) * tm
    first = jnp.clip(pick(start) + within, 0, t - 1).astype(jnp.int32)
    count = jnp.where(tile < n_used, jnp.clip(pick(counts) - within, 0, tm), 0).astype(jnp.int32)
    grp = tile_combo // N_PAIRS
    pr = tile_combo % N_PAIRS
    pair_a = jnp.sum((pr[:, None] == jnp.arange(N_PAIRS)[None, :]) * jnp.asarray(PAIR_A, jnp.int32)[None, :], axis=1)
    pair_b = jnp.sum((pr[:, None] == jnp.arange(N_PAIRS)[None, :]) * jnp.asarray(PAIR_B, jnp.int32)[None, :], axis=1)
    ea = (grp * EXPERTS_PER_GROUP + pair_a).astype(jnp.int32)
    eb = (grp * EXPERTS_PER_GROUP + pair_b).astype(jnp.int32)

    wa = lambda i, o_r, f_r, c_r, ea_r, eb_r: (l, ea_r[i], 0, 0)
    wb = lambda i, o_r, f_r, c_r, ea_r, eb_r: (l, eb_r[i], 0, 0)
    up = pl.BlockSpec((1, 1, d, D_EXPERT), wa)
    dn = pl.BlockSpec((1, 1, D_EXPERT, d), wa)
    upb = pl.BlockSpec((1, 1, d, D_EXPERT), wb)
    dnb = pl.BlockSpec((1, 1, D_EXPERT, d), wb)
    any_spec = pl.BlockSpec(memory_space=pl.ANY)
    return pl.pallas_call(
        functools.partial(_moe_kernel, n_tiles),
        grid_spec=pltpu.PrefetchScalarGridSpec(
            num_scalar_prefetch=5,
            grid=(n_tiles,),
            in_specs=[any_spec, up, up, dn, upb, upb, dnb],
            out_specs=any_spec,
            scratch_shapes=[pltpu.VMEM((2, tm, d + GATE_COLS), F32), pltpu.VMEM((2, tm, d), F32),
                            pltpu.SemaphoreType.DMA((2,)), pltpu.SemaphoreType.DMA((2,))]),
        out_shape=jax.ShapeDtypeStruct((t + 2 * tm, d), F32),
        compiler_params=_cparams(("arbitrary",)),
        name="moe",
    )(order_padded, first, count, ea, eb, hx, prep["w_eg"], prep["w_eu"], prep["w_ed"],
      prep["w_eg"], prep["w_eu"], prep["w_ed"])


def _resid_kernel(x_ref, y_ref, mod_ref, o_ref):
    d = D_MODEL
    o_ref[...] = x_ref[...] + mod_ref[0][:, 5 * d:6 * d] * y_ref[...].astype(F32)


def _residual(x, y, mod, seq, is_sample):
    t, d = x.shape
    tt = TOKEN_TILE
    tiles_per_batch = seq // tt
    if is_sample:
        mod_map = lambda i: (1 + i // tiles_per_batch, 0, 0)
    else:
        mod_map = lambda i: (0, 0, 0)
    row = lambda i: (i, 0)
    return pl.pallas_call(
        _resid_kernel,
        grid=(t // tt,),
        in_specs=[pl.BlockSpec((tt, d), row), pl.BlockSpec((tt, d), row), pl.BlockSpec((1, 1, 6 * d), mod_map)],
        out_specs=pl.BlockSpec((tt, d), row),
        out_shape=jax.ShapeDtypeStruct((t, d), F32),
        compiler_params=_cparams(("arbitrary",)),
        name="residual",
    )(x, y, mod)


def _rope_tables(length):
    rows = length // GRID_W
    row = jnp.repeat(jnp.arange(rows, dtype=F32), GRID_W)
    col = jnp.tile(jnp.arange(GRID_W, dtype=F32), rows)
    inv = ROPE_BASE ** (-jnp.arange(0, ROPE_HALF, 2, dtype=F32) / ROPE_HALF)
    ang_r = row[:, None] * inv[None, :]
    ang_c = col[:, None] * inv[None, :]
    cos = jnp.concatenate([jnp.cos(ang_r), jnp.cos(ang_r), jnp.cos(ang_c), jnp.cos(ang_c)], axis=-1)
    sin = jnp.concatenate([-jnp.sin(ang_r), jnp.sin(ang_r), -jnp.sin(ang_c), jnp.sin(ang_c)], axis=-1)
    reps = DIFF_WIDTH // DIFF_HEAD_DIM
    return jnp.tile(cos, (1, reps)), jnp.tile(sin, (1, reps))


def _prepare(g_norm1, g_norm2, w_in, w_gate_fwd, b_gate_fwd, w_gate_bwd, b_gate_bwd, g_q_norm, g_k_norm,
             w_out, w_router, b_router, w_exp_gate, w_exp_up, w_exp_down):
    depth, d, _ = w_in.shape
    r = GLA_GATE_RANK
    gate_cols = jnp.pad(w_in[..., 2048:2048 + 2 * r], ((0, 0), (0, 0), (0, GATE_COLS - 2 * r)))
    w_in_r = jnp.concatenate([w_in[..., :2048], w_in[..., 2048 + 2 * r:], gate_cols], axis=-1).astype(BF16)
    w_gate = jnp.zeros((depth, GATE_COLS, 2 * GLA_WIDTH), F32)
    w_gate = w_gate.at[:, 0:r, 0:GLA_WIDTH].set(w_gate_fwd).at[:, r:2 * r, GLA_WIDTH:].set(w_gate_bwd).astype(BF16)
    b_gate = jnp.concatenate([b_gate_fwd, b_gate_bwd], axis=-1).reshape(depth, 1, 2 * GLA_WIDTH)
    grp = jnp.arange(DIFF_WIDTH) // DIFF_HEAD_DIM
    gmat = jnp.where(grp[:, None] == grp[None, :], 1.0 / DIFF_HEAD_DIM, 0.0).astype(BF16)
    reps = DIFF_WIDTH // DIFF_HEAD_DIM
    tok = jnp.arange(TOKEN_TILE)
    same_chunk = (tok[:, None] // GLA_CHUNK) == (tok[None, :] // GLA_CHUNK)
    tri = jnp.stack([same_chunk & (tok[:, None] >= tok[None, :]),
                     same_chunk & (tok[:, None] <= tok[None, :])]).astype(BF16)
    return {
        "tri": tri,
        "g1": g_norm1.reshape(depth, 1, d),
        "g2": g_norm2.reshape(depth, 1, d),
        "w_in": w_in_r,
        "w_gate": w_gate,
        "b_gate": b_gate,
        "gmat": gmat,
        "gq": jnp.tile(g_q_norm, (1, reps)).reshape(depth, 1, DIFF_WIDTH),
        "gk": jnp.tile(g_k_norm, (1, reps)).reshape(depth, 1, DIFF_WIDTH),
        "w_out": w_out.astype(BF16),
        "w_router_t": w_router.T,
        "b_router": b_router.reshape(N_EXPERTS, 1),
        "w_eg": w_exp_gate.astype(BF16),
        "w_eu": w_exp_up.astype(BF16),
        "w_ed": w_exp_down.astype(BF16),
    }


def kernel(x_prompt, x_sample, cache_k, cache_v, state_gla_fwd, state_gla_bwd, c, c_ctx, w_mod, b_mod, g_norm1,
           g_norm2, w_in, w_gate_fwd, b_gate_fwd, w_gate_bwd, b_gate_bwd, g_gla_out, g_q_norm, g_k_norm, lambda_q1,
           lambda_k1, lambda_q2, lambda_k2, g_diff_out, w_out, w_router, b_router, w_exp_gate, w_exp_up, w_exp_down):
    nb_p, seq_p, d = x_prompt.shape
    nb_s, seq_s, _ = x_sample.shape
    depth = w_in.shape[0]
    past = cache_k.shape[2]

    prep = _prepare(g_norm1, g_norm2, w_in, w_gate_fwd, b_gate_fwd, w_gate_bwd, b_gate_bwd, g_q_norm, g_k_norm,
                    w_out, w_router, b_router, w_exp_gate, w_exp_up, w_exp_down)
    rope = _rope_tables(seq_s)

    mod_rows = 8 * ((1 + nb_s + 7) // 8)
    cs = jnp.zeros((mod_rows, d), F32).at[0].set(c_ctx).at[1:1 + nb_s].set(c)
    mod_all = _modulation(cs, w_mod, b_mod)
    ctx_k = cache_k.reshape(nb_s, depth, past, DIFF_WIDTH)
    ctx_v = cache_v.reshape(nb_s, depth, past, DIFF_WIDTH)
    gg = g_gla_out.reshape(depth, 1, GLA_DV)
    gd = g_diff_out.reshape(depth, DIFF_V_DIM, 1)
    lams = [a.reshape(depth, 1, DIFF_HEAD_DIM) for a in (lambda_q1, lambda_k1, lambda_q2, lambda_k2)]

    xp = x_prompt.reshape(nb_p * seq_p, d)
    xs = x_sample.reshape(nb_s * seq_s, d)
    yp = ys = None
    mod_prev = None
    new_k, new_v, new_sf, new_sb = [], [], [], []
    for l in range(depth):
        lam_init = 0.8 - 0.6 * math.exp(-0.3 * l)
        mod = mod_all[l].reshape(mod_rows, 1, 6 * d)
        xp, gla_in, la, qt, kd, vt, k_l, v_l = _pre_mixer(l, xp, yp, mod_prev, mod, prep, None, nb_p, seq_p, False)
        og, sf_l, sb_l = _gla(l, gla_in, la, gg, None, None, nb_p, seq_p)
        od = _diff_attention(l, lam_init, qt, kd, vt, None, None, lams, gd, nb_p, seq_p)
        xp, h2, route = _post_mixer(l, og, od, xp, mod, prep, seq_p, False)
        yp = _moe(l, h2, route, prep)
        new_k.append(k_l)
        new_v.append(v_l)
        new_sf.append(sf_l)
        new_sb.append(sb_l)
        xs, gla_in, la, qt, kd, vt = _pre_mixer(l, xs, ys, mod_prev, mod, prep, rope, nb_s, seq_s, True)
        og, _, _ = _gla(l, gla_in, la, gg, state_gla_fwd, state_gla_bwd, nb_s, seq_s)
        od = _diff_attention(l, lam_init, qt, kd, vt, ctx_k, ctx_v, lams, gd, nb_s, seq_s)
        xs, h2, route = _post_mixer(l, og, od, xs, mod, prep, seq_s, True)
        ys = _moe(l, h2, route, prep)
        mod_prev = mod
    xp = _residual(xp, yp, mod_prev, seq_p, False)
    xs = _residual(xs, ys, mod_prev, seq_s, True)

    new_cache_k = jnp.concatenate(new_k, axis=1).reshape(nb_p, depth, seq_p, DIFF_HEADS, 2, DIFF_HEAD_DIM)
    new_cache_v = jnp.concatenate(new_v, axis=1).reshape(nb_p, depth, seq_p, DIFF_HEADS, DIFF_V_DIM)
    new_sf = jnp.stack(new_sf, axis=1)
    new_sb = jnp.stack(new_sb, axis=1)
    return (xp.reshape(nb_p, seq_p, d), xs.reshape(nb_s, seq_s, d), new_cache_k, new_cache_v, new_sf, new_sb)
```

```python
import functools
import math

import jax
import jax.numpy as jnp
from jax import lax
from jax.experimental import pallas as pl
from jax.experimental.pallas import tpu as pltpu

F32 = jnp.float32
BF16 = jnp.bfloat16

D_MODEL = 1024
GLA_HEADS = 4
GLA_DK = 128
GLA_DV = 128
GLA_WIDTH = GLA_HEADS * GLA_DK
GLA_GATE_RANK = 16
GLA_GATE_NORMALIZER = 16.0
GLA_CHUNK = 64
GLA_HEADS_PER_STEP = 4
DIFF_HEADS = 4
DIFF_HEAD_DIM = 64
DIFF_V_DIM = 128
DIFF_WIDTH = DIFF_HEADS * 2 * DIFF_HEAD_DIM
ROPE_HALF = DIFF_HEAD_DIM // 2
ROPE_BASE = 10000.0
GRID_W = 64
N_EXPERTS = 16
N_GROUPS = 4
EXPERTS_PER_GROUP = 4
D_EXPERT = 512
NORM_EPS = 1e-6
LOG2E = 1.4426950408889634

PAIR_A = (0, 0, 0, 1, 1, 3)
PAIR_B = (1, 2, 3, 3, 2, 2)
N_PAIRS = len(PAIR_A)
N_COMBOS = N_GROUPS * N_PAIRS

GATE_COLS = 128
MAIN_COLS = 7 * 512
TOKEN_TILE = 256
POST_TILE = 1024
PRE_TILE = 512
MOE_TILE = 256
DMA_UNROLL = 8
ATTN_TQ = 1024
ATTN_TK = 256
ATTN_GROUP = 128
ATTN_UNROLL = 8
ATTN_LOOKAHEAD = 8
ONES_ROWS = 16
VMEM_LIMIT = 56 * 1024 * 1024


def _cparams(sem):
    return pltpu.CompilerParams(dimension_semantics=sem, vmem_limit_bytes=VMEM_LIMIT)


def _dot(a, b):
    return jnp.dot(a, b, preferred_element_type=F32)


def _dot_nt(a, b):
    return lax.dot_general(a, b, (((1,), (1,)), ((), ())), preferred_element_type=F32)


def _dot_tn(a, b):
    return lax.dot_general(a, b, (((0,), (0,)), ((), ())), preferred_element_type=F32)


def _sigmoid(x):
    return 1.0 / (1.0 + jnp.exp(-x))


def _split_bf16(x):
    hi = x.astype(BF16)
    lo = (x - hi.astype(F32)).astype(BF16)
    return hi, lo


def _mod_kernel(c_ref, w_ref, b_ref, o_ref):
    c = c_ref[...]
    s = c * _sigmoid(c)
    o_ref[0] = jnp.dot(s, w_ref[0], preferred_element_type=F32,
                       precision=lax.Precision.HIGHEST) + b_ref[0]


def _modulation(cs, w_mod, b_mod):
    depth, d, n = w_mod.shape
    rows = cs.shape[0]
    tn = 1536
    return pl.pallas_call(
        _mod_kernel,
        grid=(depth, n // tn),
        in_specs=[pl.BlockSpec((rows, d), lambda l, j: (0, 0)),
                  pl.BlockSpec((1, d, tn), lambda l, j: (l, 0, j)),
                  pl.BlockSpec((1, 1, tn), lambda l, j: (l, 0, j))],
        out_specs=pl.BlockSpec((1, rows, tn), lambda l, j: (l, 0, j)),
        out_shape=jax.ShapeDtypeStruct((depth, rows, n), F32),
        compiler_params=_cparams(("arbitrary", "arbitrary")),
        name="modulation",
    )(cs, w_mod, b_mod.reshape(depth, 1, n))


def _rope(y, cos, sin):
    w = y.shape[1]
    lane = lax.broadcasted_iota(jnp.int32, y.shape, 1)
    first = (lane % (2 * (ROPE_HALF // 2))) < (ROPE_HALF // 2)
    partner = jnp.where(first, pltpu.roll(y, w - ROPE_HALF // 2, axis=1), pltpu.roll(y, ROPE_HALF // 2, axis=1))
    return y * cos + partner * sin


def _pre_kernel(has_moe, is_sample, *refs):
    refs = list(refs)
    x_ref = refs.pop(0)
    if has_moe:
        y_ref, modp_ref = refs[:2]
        refs = refs[2:]
    mod_ref, g1_ref, win_ref, wgate_ref, bgate_ref, tri_ref, gmat_ref, gq_ref, gk_ref = refs[:9]
    refs = refs[9:]
    if is_sample:
        cos_ref, sin_ref = refs[:2]
        refs = refs[2:]
    xo_ref = refs.pop(0) if has_moe else None
    gla_ref, la_ref, qt_ref, kd_ref, vt_ref = refs[:5]
    refs = refs[5:]
    if not is_sample:
        kc_ref, vc_ref = refs

    d = D_MODEL
    sub = TOKEN_TILE
    m = mod_ref[0]
    gs = g1_ref[0] * (1.0 + m[:, d:2 * d])
    scale = DIFF_HEAD_DIM ** -0.5 * LOG2E
    w = GLA_WIDTH
    subs = [slice(j * sub, (j + 1) * sub) for j in range(x_ref.shape[0] // sub)]
    hbs = []
    for rs in subs:
        x = x_ref[rs, :]
        if has_moe:
            x = x + modp_ref[0][:, 5 * d:6 * d] * y_ref[rs, :].astype(F32)
            xo_ref[rs, :] = x
        ms = jnp.mean(x * x, axis=-1, keepdims=True)
        hbs.append((x * lax.rsqrt(ms + NORM_EPS) * gs + m[:, 0:d]).astype(BF16))

    def put_t(ref, j, rs, val):
        if is_sample:
            ref[0, :, rs] = val.astype(BF16).T
        else:
            ref[j] = val.astype(BF16).T

    hb_all = jnp.concatenate(hbs, axis=0)

    def plain(c):
        p = _dot(hb_all, win_ref[0, :, c * 512:(c + 1) * 512])
        if c == 0:
            p = p * (GLA_DK ** -0.5)
        gla_ref[:, c * 512:(c + 1) * 512] = p.astype(BF16)

    q1 = [_dot(hb, win_ref[0, :, 2048:2560]) for hb in hbs]
    k1 = [_dot(hb, win_ref[0, :, 2560:3072]) for hb in hbs]
    a1 = [_dot(hb, win_ref[0, :, MAIN_COLS:MAIN_COLS + GATE_COLS]) for hb in hbs]
    plain(0)
    msq = [_dot((p * p).astype(BF16), gmat_ref[...]) for p in q1]
    msk = [_dot((p * p).astype(BF16), gmat_ref[...]) for p in k1]
    zs = [_dot(a.astype(BF16), wgate_ref[0]) + bgate_ref[0] for a in a1]
    plain(1)
    for rs, z in zip(subs, zs):
        la = (jnp.minimum(z, 0.0) - jnp.log1p(jnp.exp(-jnp.abs(z)))) * (1.0 / GLA_GATE_NORMALIZER)
        la_hi, la_lo = _split_bf16(la)
        la_ref[rs, 0:w] = _dot(tri_ref[0], la_hi[:, 0:w]) + _dot(tri_ref[0], la_lo[:, 0:w])
        la_ref[rs, w:2 * w] = _dot(tri_ref[1], la_hi[:, w:2 * w]) + _dot(tri_ref[1], la_lo[:, w:2 * w])
    for j, rs in enumerate(subs):
        q = q1[j] * lax.rsqrt(msq[j] + NORM_EPS) * gq_ref[0]
        k = k1[j] * lax.rsqrt(msk[j] + NORM_EPS) * gk_ref[0]
        if is_sample:
            q = _rope(q, cos_ref[rs, :], sin_ref[rs, :])
            k = _rope(k, cos_ref[rs, :], sin_ref[rs, :])
        else:
            kc_ref[j, 0] = k
        put_t(qt_ref, j, rs, q * scale)
        kd_ref[rs, :] = k.astype(BF16)
    plain(2)
    for j, (rs, hb) in enumerate(zip(subs, hbs)):
        v = _dot(hb, win_ref[0, :, 3072:3584])
        if not is_sample:
            vc_ref[j, 0] = v
        put_t(vt_ref, j, rs, v)
    plain(3)


def _pre_mixer(l, x, y, mod_prev, mod, prep, rope, nb, seq, is_sample):
    t, d = x.shape
    sub = TOKEN_TILE
    tt = PRE_TILE
    has_moe = y is not None
    if is_sample:
        assert seq % tt == 0
        tiles_per_batch = seq // tt
        mod_map = lambda i: (1 + i // tiles_per_batch, 0, 0)
    else:
        assert seq == sub and t % tt == 0
        mod_map = lambda i: (0, 0, 0)
    row = lambda i: (i, 0)
    const2 = lambda i: (0, 0)
    lay3 = lambda i: (l, 0, 0)

    ins = [x]
    in_specs = [pl.BlockSpec((tt, d), row)]
    if has_moe:
        ins += [y, mod_prev]
        in_specs += [pl.BlockSpec((tt, d), row), pl.BlockSpec((1, 1, 6 * d), mod_map)]
    ins += [mod, prep["g1"], prep["w_in"], prep["w_gate"], prep["b_gate"], prep["tri"], prep["gmat"], prep["gq"],
            prep["gk"]]
    in_specs += [pl.BlockSpec((1, 1, 6 * d), mod_map),
                 pl.BlockSpec((1, 1, d), lay3),
                 pl.BlockSpec((1, d, MAIN_COLS + GATE_COLS), lay3, pipeline_mode=pl.Buffered(1)),
                 pl.BlockSpec((1, GATE_COLS, 2 * GLA_WIDTH), lay3),
                 pl.BlockSpec((1, 1, 2 * GLA_WIDTH), lay3),
                 pl.BlockSpec((2, sub, sub), lambda i: (0, 0, 0)),
                 pl.BlockSpec((DIFF_WIDTH, DIFF_WIDTH), const2),
                 pl.BlockSpec((1, 1, DIFF_WIDTH), lay3),
                 pl.BlockSpec((1, 1, DIFF_WIDTH), lay3)]
    if is_sample:
        ins += [rope[0], rope[1]]
        in_specs += [pl.BlockSpec((tt, DIFF_WIDTH), lambda i: (i % tiles_per_batch, 0))] * 2

    out_shape = []
    out_specs = []
    if has_moe:
        out_shape.append(jax.ShapeDtypeStruct((t, d), F32))
        out_specs.append(pl.BlockSpec((tt, d), row))
    if is_sample:
        tr_spec = pl.BlockSpec((1, DIFF_WIDTH, tt), lambda i: (i // tiles_per_batch, 0, i % tiles_per_batch))
    else:
        tr_spec = pl.BlockSpec((tt // sub, DIFF_WIDTH, seq), lambda i: (i, 0, 0))
    out_shape += [jax.ShapeDtypeStruct((t, 4 * GLA_WIDTH), BF16),
                  jax.ShapeDtypeStruct((t, 2 * GLA_WIDTH), F32),
                  jax.ShapeDtypeStruct((nb, DIFF_WIDTH, seq), BF16),
                  jax.ShapeDtypeStruct((t, DIFF_WIDTH), BF16),
                  jax.ShapeDtypeStruct((nb, DIFF_WIDTH, seq), BF16)]
    out_specs += [pl.BlockSpec((tt, 4 * GLA_WIDTH), row),
                  pl.BlockSpec((tt, 2 * GLA_WIDTH), row),
                  tr_spec,
                  pl.BlockSpec((tt, DIFF_WIDTH), row),
                  tr_spec]
    if not is_sample:
        out_shape += [jax.ShapeDtypeStruct((nb, 1, seq, DIFF_WIDTH), F32)] * 2
        out_specs += [pl.BlockSpec((tt // sub, 1, seq, DIFF_WIDTH), lambda i: (i, 0, 0, 0))] * 2

    outs = pl.pallas_call(
        functools.partial(_pre_kernel, has_moe, is_sample),
        grid=(t // tt,),
        in_specs=in_specs,
        out_specs=out_specs,
        out_shape=out_shape,
        compiler_params=_cparams(("arbitrary",)),
        name="pre_mixer",
    )(*ins)
    outs = list(outs)
    x_new = outs.pop(0) if has_moe else x
    return [x_new] + outs


def _gla_kernel(has_state, n_chunks, hg, *refs):
    refs = list(refs)
    q_ref, k_ref, v_ref, g_ref, bf_ref, bb_ref, gg_ref = refs[:7]
    refs = refs[7:]
    if has_state:
        s0f_ref, s0b_ref = refs[:2]
        refs = refs[2:]
    o_ref, sf_ref, sb_ref, st_ref, acc_ref = refs

    c = GLA_CHUNK
    dk = GLA_DK
    for hd in range(hg):
        if has_state:
            st_ref[2 * hd] = s0f_ref[0, 0, hd].T
            st_ref[2 * hd + 1] = s0b_ref[0, 0, hd].T
        else:
            st_ref[2 * hd] = jnp.zeros((GLA_DV, dk), F32)
            st_ref[2 * hd + 1] = jnp.zeros((GLA_DV, dk), F32)

    r = lax.broadcasted_iota(jnp.int32, (c, c), 0)
    s = lax.broadcasted_iota(jnp.int32, (c, c), 1)
    lower = r >= s
    upper = r <= s

    def scores(rows, hd, b_ref, st_i, mid_row, last_row):
        cs = slice(hd * dk, (hd + 1) * dk)
        q = q_ref[rows, cs].astype(F32)
        k = k_ref[rows, cs].astype(F32)
        v = v_ref[rows, cs]
        b = b_ref[rows, cs]
        mid = b[mid_row:mid_row + 1]
        last = b[last_row:last_row + 1]
        qe = q * jnp.exp(b - mid)
        ke = k * jnp.exp(mid - b)
        qi = (qe * jnp.exp(mid)).astype(BF16)
        ks = (ke * jnp.exp(last - mid)).astype(BF16)
        qe = qe.astype(BF16)
        ke = ke.astype(BF16)
        st = st_ref[st_i]
        return _dot_nt(qe, ke), _dot_nt(qi, st.astype(BF16)), _dot_tn(v, ks), v, st, last

    def outputs(sc, mask, st_i):
        att, o_inter, kv, v, st, last = sc
        st_ref[st_i] = st * jnp.exp(last) + kv
        return _dot(jnp.where(mask, att, 0.0).astype(BF16), v) + o_inter

    def finish(o, rows, hd):
        cs = slice(hd * dk, (hd + 1) * dk)
        ms = jnp.mean(o * o, axis=-1, keepdims=True)
        g = g_ref[rows, cs].astype(F32)
        o_ref[rows, cs] = (o * lax.rsqrt(ms + NORM_EPS) * gg_ref[0] * (g * _sigmoid(g))).astype(BF16)

    def step(n, second_visit):
        rf = pl.ds(pl.multiple_of(n * c, c), c)
        rb = pl.ds(pl.multiple_of((n_chunks - 1 - n) * c, c), c)
        sc = []
        for hd in range(hg):
            sc.append(scores(rf, hd, bf_ref, 2 * hd, c // 2, c - 1))
            sc.append(scores(rb, hd, bb_ref, 2 * hd + 1, c - 1 - c // 2, 0))
        for hd in range(hg):
            cs = slice(hd * dk, (hd + 1) * dk)
            for rows, sci, mask, st_i in ((rf, sc[2 * hd], lower, 2 * hd), (rb, sc[2 * hd + 1], upper, 2 * hd + 1)):
                o = outputs(sci, mask, st_i)
                if second_visit:
                    finish(acc_ref[rows, cs] + o, rows, hd)
                else:
                    acc_ref[rows, cs] = o

    half = n_chunks // 2
    per_trip = 2 if half % 2 == 0 else 1

    def trips(first, second_visit):
        def body(n, carry):
            for u in range(per_trip):
                step(first + n * per_trip + u, second_visit)
            return carry
        lax.fori_loop(0, half // per_trip, body, 0)

    trips(0, False)
    trips(half, True)

    for hd in range(hg):
        sf_ref[0, hd] = st_ref[2 * hd].T
        sb_ref[0, hd] = st_ref[2 * hd + 1].T


def _gla(l, gla_in, bsum, gg, s0f, s0b, nb, seq):
    t = gla_in.shape[0]
    h = GLA_HEADS
    hg = GLA_HEADS_PER_STEP
    nhb = h // hg
    w = hg * GLA_DK
    has_state = s0f is not None
    n_chunks = seq // GLA_CHUNK
    assert n_chunks % 2 == 0
    col = lambda off: (lambda b, hh: (b, off + hh))
    in_bytes = seq * w * (4 * 2 + 2 * 4)
    mode = pl.Buffered(1) if 2 * in_bytes > VMEM_LIMIT // 2 else None
    big = lambda off: pl.BlockSpec((seq, w), col(off), pipeline_mode=mode)
    ins = [gla_in, gla_in, gla_in, gla_in, bsum, bsum, gg]
    in_specs = [big(0), big(nhb), big(2 * nhb), big(3 * nhb), big(0), big(nhb),
                pl.BlockSpec((1, 1, GLA_DV), lambda b, hh: (l, 0, 0))]
    if has_state:
        ins += [s0f, s0b]
        in_specs += [pl.BlockSpec((1, 1, hg, GLA_DK, GLA_DV), lambda b, hh: (b, l, hh, 0, 0))] * 2
    return pl.pallas_call(
        functools.partial(_gla_kernel, has_state, n_chunks, hg),
        grid=(nb, nhb),
        in_specs=in_specs,
        out_specs=[pl.BlockSpec((seq, w), col(0)),
                   pl.BlockSpec((1, hg, GLA_DK, GLA_DV), lambda b, hh: (b, hh, 0, 0)),
                   pl.BlockSpec((1, hg, GLA_DK, GLA_DV), lambda b, hh: (b, hh, 0, 0))],
        out_shape=[jax.ShapeDtypeStruct((t, GLA_WIDTH), BF16),
                   jax.ShapeDtypeStruct((nb, h, GLA_DK, GLA_DV), F32),
                   jax.ShapeDtypeStruct((nb, h, GLA_DK, GLA_DV), F32)],
        scratch_shapes=[pltpu.VMEM((2 * hg, GLA_DV, GLA_DK), F32), pltpu.VMEM((seq, w), F32)],
        compiler_params=_cparams(("arbitrary", "arbitrary")),
        name="gla",
    )(*ins)


def _attn_kernel(lam_init, has_ctx, n_kt, tk, *refs):
    refs = list(refs)
    qt_ref, k_ref, vt_ref = refs[:3]
    refs = refs[3:]
    if has_ctx:
        ck_ref, cv_ref = refs[:2]
        refs = refs[2:]
    lq1_ref, lk1_ref, lq2_ref, lk2_ref, gd_ref, o_ref = refs

    qt = qt_ref[0]
    tq = qt.shape[1]
    gq = ATTN_GROUP
    n_groups = tq // gq
    dim = lax.broadcasted_iota(jnp.int32, (2 * DIFF_HEAD_DIM, gq), 0)
    zero = jnp.zeros((2 * DIFF_HEAD_DIM, gq), BF16)
    qws = []
    for g in range(n_groups):
        qg = qt[:, g * gq:(g + 1) * gq]
        qws.append(jnp.concatenate([jnp.where(dim < DIFF_HEAD_DIM, qg, zero),
                                    jnp.where(dim >= DIFF_HEAD_DIM, qg, zero)], axis=1))

    def absorb(group_state, st, vt):
        m, acc = group_state
        m_new = jnp.maximum(m, jnp.max(st, axis=0, keepdims=True))
        alpha = jnp.exp2(m - m_new)
        p = jnp.exp2(st - m_new)
        return m_new, alpha * acc + _dot(vt, p.astype(BF16))

    def process(state, tiles):
        units = [(u, g) for u in range(len(tiles)) for g in range(n_groups)]
        score = lambda idx: _dot(tiles[units[idx][0]][0], qws[units[idx][1]])
        state = list(state)
        pending = [score(idx) for idx in range(min(ATTN_LOOKAHEAD, len(units)))]
        for idx, (u, g) in enumerate(units):
            if idx + ATTN_LOOKAHEAD < len(units):
                pending.append(score(idx + ATTN_LOOKAHEAD))
            state[g] = absorb(state[g], pending.pop(0), tiles[u][1])
        return tuple(state)

    def with_ones(vt):
        return jnp.concatenate([vt, jnp.ones((ONES_ROWS, vt.shape[1]), BF16)], axis=0)

    state = tuple((jnp.full((1, 2 * gq), -jnp.inf, F32), jnp.zeros((DIFF_V_DIM + ONES_ROWS, 2 * gq), F32))
                  for _ in range(n_groups))
    if has_ctx:
        past = ck_ref.shape[2]
        ctk = tk if past % tk == 0 else past
        cvt = cv_ref[0, 0].T.astype(BF16)
        tiles = [(ck_ref[0, 0, j * ctk:(j + 1) * ctk, :].astype(BF16), with_ones(cvt[:, j * ctk:(j + 1) * ctk]))
                 for j in range(past // ctk)]
        state = process(state, tiles)

    unroll = max(u for u in range(1, ATTN_UNROLL + 1) if n_kt % u == 0)

    def body(j, st):
        tiles = []
        for u in range(unroll):
            rr = pl.ds(pl.multiple_of((j * unroll + u) * tk, tk), tk)
            tiles.append((k_ref[rr, :], with_ones(vt_ref[0, :, rr])))
        return process(st, tiles)

    state = lax.fori_loop(0, n_kt // unroll, body, state)

    lam = (jnp.exp(jnp.sum(lq1_ref[0] * lk1_ref[0], axis=-1, keepdims=True))
           - jnp.exp(jnp.sum(lq2_ref[0] * lk2_ref[0], axis=-1, keepdims=True)) + lam_init)
    for g in range(n_groups):
        m, acc = state[g]
        on = acc[:DIFF_V_DIM] / acc[DIFF_V_DIM:DIFF_V_DIM + 1]
        o = on[:, :gq] - lam * on[:, gq:]
        ms = jnp.mean(o * o, axis=0, keepdims=True)
        o = o * lax.rsqrt(ms + NORM_EPS) * gd_ref[0] * (1.0 - lam_init)
        o_ref[g * gq:(g + 1) * gq, :] = o.T.astype(BF16)


def _diff_attention(l, lam_init, qt, kd, vt, ctx_k, ctx_v, lams, gd, nb, seq):
    t = kd.shape[0]
    h = DIFF_HEADS
    has_ctx = ctx_k is not None
    tq = min(ATTN_TQ, seq)
    tk = min(ATTN_TK, seq)
    nq = seq // tq
    w = 2 * DIFF_HEAD_DIM
    ins = [qt, kd, vt]
    in_specs = [pl.BlockSpec((1, w, tq), lambda b, hh, i: (b, hh, i)),
                pl.BlockSpec((seq, w), lambda b, hh, i: (b, hh)),
                pl.BlockSpec((1, DIFF_V_DIM, seq), lambda b, hh, i: (b, hh, 0))]
    if has_ctx:
        past = ctx_k.shape[2]
        ins += [ctx_k, ctx_v]
        in_specs += [pl.BlockSpec((1, 1, past, w), lambda b, hh, i: (b, l, 0, hh)),
                     pl.BlockSpec((1, 1, past, DIFF_V_DIM), lambda b, hh, i: (b, l, 0, hh))]
    lay3 = lambda b, hh, i: (l, 0, 0)
    ins += list(lams) + [gd]
    in_specs += [pl.BlockSpec((1, 1, DIFF_HEAD_DIM), lay3)] * 4 + [pl.BlockSpec((1, DIFF_V_DIM, 1), lay3)]
    return pl.pallas_call(
        functools.partial(_attn_kernel, lam_init, has_ctx, seq // tk, tk),
        grid=(nb, h, nq),
        in_specs=in_specs,
        out_specs=pl.BlockSpec((tq, DIFF_V_DIM), lambda b, hh, i: (b * nq + i, hh)),
        out_shape=jax.ShapeDtypeStruct((t, DIFF_WIDTH), BF16),
        compiler_params=_cparams(("arbitrary", "arbitrary", "arbitrary")),
        name="diff_attention",
    )(*ins)


def _top2_of4(v):
    m1 = jnp.maximum(jnp.maximum(v[0], v[1]), jnp.maximum(v[2], v[3]))
    i1 = jnp.where(v[0] == m1, 0, jnp.where(v[1] == m1, 1, jnp.where(v[2] == m1, 2, 3)))
    neg = jnp.full_like(m1, -jnp.inf)
    w = [jnp.where(i1 == j, neg, v[j]) for j in range(4)]
    m2 = jnp.maximum(jnp.maximum(w[0], w[1]), jnp.maximum(w[2], w[3]))
    i2 = jnp.where(w[0] == m2, 0, jnp.where(w[1] == m2, 1, jnp.where(w[2] == m2, 2, 3)))
    return m1, i1, m2, i2


def _post_kernel(og_ref, od_ref, x_ref, mod_ref, g2_ref, wo_ref, wr_ref, br_ref, x1_ref, hx_ref, route_ref):
    d = D_MODEL
    sub = TOKEN_TILE
    n_sub = x_ref.shape[0] // sub
    m = mod_ref[0]
    rows = [slice(j * sub, (j + 1) * sub) for j in range(n_sub)]
    outs = [_dot(og_ref[rs, :], wo_ref[0, 0:GLA_WIDTH, :]) + _dot(od_ref[rs, :], wo_ref[0, GLA_WIDTH:, :])
            for rs in rows]
    w_hi, w_lo = _split_bf16(wr_ref[...])
    w_hl = jnp.concatenate([w_hi, w_lo], axis=0)
    zs = []
    for rs, out in zip(rows, outs):
        x1 = x_ref[rs, :] + m[:, 2 * d:3 * d] * out
        x1_ref[rs, :] = x1
        ms = jnp.mean(x1 * x1, axis=-1, keepdims=True)
        h2 = x1 * lax.rsqrt(ms + NORM_EPS) * (g2_ref[0] * (1.0 + m[:, 4 * d:5 * d])) + m[:, 3 * d:4 * d]
        h_hi, h_lo = _split_bf16(h2)
        hx_ref[rs, 0:d] = h2
        za = _dot_nt(w_hl, h_hi)
        zs.append(za[0:N_EXPERTS] + za[N_EXPERTS:2 * N_EXPERTS] + _dot_nt(w_hi, h_lo))
    for rs, z in zip(rows, zs):
        _route(z, br_ref, hx_ref, route_ref, rs)


def _route(z, br_ref, hx_ref, route_ref, rs):
    d = D_MODEL
    s = _sigmoid(z)
    sel = s + br_ref[...]

    e = EXPERTS_PER_GROUP
    tops = []
    for g in range(N_GROUPS):
        tops.append(_top2_of4([sel[g * e + j:g * e + j + 1, :] for j in range(e)]))
    score = [t[0] + t[2] for t in tops]
    best = jnp.maximum(jnp.maximum(score[0], score[1]), jnp.maximum(score[2], score[3]))
    gi = jnp.where(score[0] == best, 0, jnp.where(score[1] == best, 1, jnp.where(score[2] == best, 2, 3)))

    def pick(rows):
        return jnp.where(gi == 0, rows[0], jnp.where(gi == 1, rows[1], jnp.where(gi == 2, rows[2], rows[3])))

    i1 = pick([t[1] for t in tops])
    i2 = pick([t[3] for t in tops])

    def gate_of(idx):
        per_group = []
        for g in range(N_GROUPS):
            rows = [s[g * e + j:g * e + j + 1, :] for j in range(e)]
            per_group.append(jnp.where(idx == 0, rows[0], jnp.where(idx == 1, rows[1],
                                                                     jnp.where(idx == 2, rows[2], rows[3]))))
        return pick(per_group)

    s1 = gate_of(i1)
    s2 = gate_of(i2)
    tot = s1 + s2
    w1 = s1 / tot
    w2 = s2 / tot
    lo = jnp.minimum(i1, i2)
    hi = jnp.maximum(i1, i2)
    w_lo = jnp.where(i1 < i2, w1, w2)
    w_hi = jnp.where(i1 < i2, w2, w1)
    pair = jnp.where(lo == 0, hi - 1, jnp.where(lo == 1, jnp.where(hi == 3, 3, 4), 5))
    swapped = pair == 5
    gate_a = jnp.where(swapped, w_hi, w_lo)
    gate_b = jnp.where(swapped, w_lo, w_hi)
    route_ref[:, rs] = jnp.broadcast_to(gi * N_PAIRS + pair, (route_ref.shape[0], gate_a.shape[1]))
    r = lax.broadcasted_iota(jnp.int32, (GATE_COLS, gate_a.shape[1]), 0)
    gates_t = jnp.where(r == 0, gate_a, jnp.where(r == 1, gate_b, 0.0))
    hx_ref[rs, d:d + GATE_COLS] = gates_t.T


def _post_mixer(l, og, od, x, mod, prep, seq, is_sample):
    t, d = x.shape
    tt = POST_TILE if (seq % POST_TILE == 0 or not is_sample) and t % POST_TILE == 0 else TOKEN_TILE
    tiles_per_batch = max(seq // tt, 1)
    if is_sample:
        mod_map = lambda i: (1 + i // tiles_per_batch, 0, 0)
    else:
        mod_map = lambda i: (0, 0, 0)
    row = lambda i: (i, 0)
    lay3 = lambda i: (l, 0, 0)
    return pl.pallas_call(
        _post_kernel,
        grid=(t // tt,),
        in_specs=[pl.BlockSpec((tt, GLA_WIDTH), row),
                  pl.BlockSpec((tt, DIFF_WIDTH), row),
                  pl.BlockSpec((tt, d), row),
                  pl.BlockSpec((1, 1, 6 * d), mod_map),
                  pl.BlockSpec((1, 1, d), lay3),
                  pl.BlockSpec((1, d, d), lay3),
                  pl.BlockSpec((N_EXPERTS, d), lambda i: (0, 0)),
                  pl.BlockSpec((N_EXPERTS, 1), lambda i: (0, 0))],
        out_specs=[pl.BlockSpec((tt, d), row),
                   pl.BlockSpec((tt, d + GATE_COLS), row),
                   pl.BlockSpec((8, tt), lambda i: (0, i))],
        out_shape=[jax.ShapeDtypeStruct((t, d), F32),
                   jax.ShapeDtypeStruct((t, d + GATE_COLS), F32),
                   jax.ShapeDtypeStruct((8, t), jnp.int32)],
        compiler_params=_cparams(("arbitrary",)),
        name="post_mixer",
    )(og, od, x, mod, prep["g2"], prep["w_out"], prep["w_router_t"], prep["b_router"])


def _moe_kernel(n_tiles, order_ref, first_ref, count_ref, ea_ref, eb_ref,
                hx_hbm, wga_ref, wua_ref, wda_ref, wgb_ref, wub_ref, wdb_ref,
                y_hbm, xbuf, ybuf, in_sem, out_sem):
    i = pl.program_id(0)
    slot = i % 2
    d = D_MODEL
    tm = xbuf.shape[1]
    n_tokens = hx_hbm.shape[0]

    def gather_copy(buf_slot, r, tok):
        return pltpu.make_async_copy(hx_hbm.at[pl.ds(tok, 1), :], xbuf.at[buf_slot, pl.ds(r, 1), :],
                                     in_sem.at[buf_slot])

    def scatter_copy(buf_slot, r, row):
        return pltpu.make_async_copy(ybuf.at[buf_slot, pl.ds(r, 1), :], y_hbm.at[pl.ds(row, 1), :],
                                     out_sem.at[buf_slot])

    def for_rows(fn):
        def body(q, c):
            for s in range(DMA_UNROLL):
                fn(q * DMA_UNROLL + s)
            return c
        lax.fori_loop(0, tm // DMA_UNROLL, body, 0)

    def wait_gather(buf_slot):
        for_rows(lambda r: gather_copy(buf_slot, r, 0).wait())

    def wait_scatter(buf_slot):
        for_rows(lambda r: scatter_copy(buf_slot, r, 0).wait())

    def scatter_row(buf_slot, base, n, r):
        return jnp.where(r < n, order_ref[base + r], n_tokens + buf_slot * tm + r)

    prev = jnp.maximum(i - 1, 0)

    @pl.when(i == 0)
    def _():
        ybuf[...] = jnp.zeros_like(ybuf)
        for s in range(2):
            spare_init = pltpu.make_async_copy(ybuf.at[s], y_hbm.at[pl.ds(n_tokens + s * tm, tm), :], out_sem.at[s])
            spare_init.start()
            spare_init.wait()
        base0 = first_ref[0]
        for_rows(lambda r: gather_copy(0, r, order_ref[base0 + r]).start())

    @pl.when((i == 0) | (count_ref[prev] > 0))
    def _():
        wait_gather(slot)

    @pl.when((i == 1) | ((i >= 2) & (count_ref[jnp.maximum(i - 2, 0)] > 0)))
    def _():
        wait_scatter(slot)

    @pl.when(count_ref[i] > 0)
    def _():
        xg = xbuf[slot]
        x = xg[:, 0:d].astype(BF16)
        base_next = first_ref[i + 1]
        base_prev = first_ref[prev]
        n_prev = jnp.where(i > 0, count_ref[prev], 0)
        per_dot = -(-tm // 6)

        def issue(part):
            for r in range(part * per_dot, min((part + 1) * per_dot, tm)):
                gather_copy(1 - slot, r, order_ref[base_next + r]).start()
                scatter_copy(1 - slot, r, scatter_row(1 - slot, base_prev, n_prev, r)).start()

        hg_a = _dot(x, wga_ref[0, 0])
        issue(0)
        hu_a = _dot(x, wua_ref[0, 0])
        issue(1)
        hg_b = _dot(x, wgb_ref[0, 0])
        issue(2)
        hu_b = _dot(x, wub_ref[0, 0])
        issue(3)
        out_a = _dot(((hg_a * _sigmoid(hg_a)) * hu_a).astype(BF16), wda_ref[0, 0])
        issue(4)
        out_b = _dot(((hg_b * _sigmoid(hg_b)) * hu_b).astype(BF16), wdb_ref[0, 0])
        issue(5)
        ybuf[slot] = xg[:, d:d + 1] * out_a + xg[:, d + 1:d + 2] * out_b

    @pl.when((count_ref[i] == 0) & (i > 0) & (count_ref[prev] > 0))
    def _():
        base_prev = first_ref[prev]
        n_prev = count_ref[prev]
        for_rows(lambda r: scatter_copy(1 - slot, r, scatter_row(1 - slot, base_prev, n_prev, r)).start())

    @pl.when((i == n_tiles - 1) & (count_ref[prev] > 0))
    def _():
        wait_scatter(1 - slot)


def _moe(l, hx, route, prep):
    t = hx.shape[0]
    d = D_MODEL
    tm = MOE_TILE
    n_tiles = t // tm + N_COMBOS

    combo = route[0]
    order = jnp.argsort(combo).astype(jnp.int32)
    order_padded = jnp.concatenate([order, jnp.zeros((tm,), jnp.int32)])
    counts = jnp.sum(combo[:, None] == jnp.arange(N_COMBOS, dtype=jnp.int32)[None, :], axis=0).astype(jnp.int32)
    tiles_of = (counts + tm - 1) // tm
    tile_end = jnp.cumsum(tiles_of)
    start = jnp.cumsum(counts) - counts
    tile = jnp.arange(n_tiles, dtype=jnp.int32)
    n_used = tile_end[-1]
    tile_combo = jnp.sum(jnp.minimum(tile, n_used - 1)[:, None] >= tile_end[None, :], axis=1).astype(jnp.int32)
    onehot = (tile_combo[:, None] == jnp.arange(N_COMBOS, dtype=jnp.int32)[None, :]).astype(jnp.int32)
    pick = lambda table: jnp.sum(onehot * table[None, :], axis=1)
    within = (tile - (pick(tile_end) - pick(tiles_of))) * tm
    first = jnp.clip(pick(start) + within, 0, t - 1).astype(jnp.int32)
    count = jnp.where(tile < n_used, jnp.clip(pick(counts) - within, 0, tm), 0).astype(jnp.int32)
    grp = tile_combo // N_PAIRS
    pr = tile_combo % N_PAIRS
    pair_a = jnp.sum((pr[:, None] == jnp.arange(N_PAIRS)[None, :]) * jnp.asarray(PAIR_A, jnp.int32)[None, :], axis=1)
    pair_b = jnp.sum((pr[:, None] == jnp.arange(N_PAIRS)[None, :]) * jnp.asarray(PAIR_B, jnp.int32)[None, :], axis=1)
    ea = (grp * EXPERTS_PER_GROUP + pair_a).astype(jnp.int32)
    eb = (grp * EXPERTS_PER_GROUP + pair_b).astype(jnp.int32)

    wa = lambda i, o_r, f_r, c_r, ea_r, eb_r: (l, ea_r[i], 0, 0)
    wb = lambda i, o_r, f_r, c_r, ea_r, eb_r: (l, eb_r[i], 0, 0)
    up = pl.BlockSpec((1, 1, d, D_EXPERT), wa)
    dn = pl.BlockSpec((1, 1, D_EXPERT, d), wa)
    upb = pl.BlockSpec((1, 1, d, D_EXPERT), wb)
    dnb = pl.BlockSpec((1, 1, D_EXPERT, d), wb)
    any_spec = pl.BlockSpec(memory_space=pl.ANY)
    return pl.pallas_call(
        functools.partial(_moe_kernel, n_tiles),
        grid_spec=pltpu.PrefetchScalarGridSpec(
            num_scalar_prefetch=5,
            grid=(n_tiles,),
            in_specs=[any_spec, up, up, dn, upb, upb, dnb],
            out_specs=any_spec,
            scratch_shapes=[pltpu.VMEM((2, tm, d + GATE_COLS), F32), pltpu.VMEM((2, tm, d), F32),
                            pltpu.SemaphoreType.DMA((2,)), pltpu.SemaphoreType.DMA((2,))]),
        out_shape=jax.ShapeDtypeStruct((t + 2 * tm, d), F32),
        compiler_params=_cparams(("arbitrary",)),
        name="moe",
    )(order_padded, first, count, ea, eb, hx, prep["w_eg"], prep["w_eu"], prep["w_ed"],
      prep["w_eg"], prep["w_eu"], prep["w_ed"])


def _resid_kernel(x_ref, y_ref, mod_ref, o_ref):
    d = D_MODEL
    o_ref[...] = x_ref[...] + mod_ref[0][:, 5 * d:6 * d] * y_ref[...].astype(F32)


def _residual(x, y, mod, seq, is_sample):
    t, d = x.shape
    tt = TOKEN_TILE
    tiles_per_batch = seq // tt
    if is_sample:
        mod_map = lambda i: (1 + i // tiles_per_batch, 0, 0)
    else:
        mod_map = lambda i: (0, 0, 0)
    row = lambda i: (i, 0)
    return pl.pallas_call(
        _resid_kernel,
        grid=(t // tt,),
        in_specs=[pl.BlockSpec((tt, d), row), pl.BlockSpec((tt, d), row), pl.BlockSpec((1, 1, 6 * d), mod_map)],
        out_specs=pl.BlockSpec((tt, d), row),
        out_shape=jax.ShapeDtypeStruct((t, d), F32),
        compiler_params=_cparams(("arbitrary",)),
        name="residual",
    )(x, y, mod)


def _rope_tables(length):
    rows = length // GRID_W
    row = jnp.repeat(jnp.arange(rows, dtype=F32), GRID_W)
    col = jnp.tile(jnp.arange(GRID_W, dtype=F32), rows)
    inv = ROPE_BASE ** (-jnp.arange(0, ROPE_HALF, 2, dtype=F32) / ROPE_HALF)
    ang_r = row[:, None] * inv[None, :]
    ang_c = col[:, None] * inv[None, :]
    cos = jnp.concatenate([jnp.cos(ang_r), jnp.cos(ang_r), jnp.cos(ang_c), jnp.cos(ang_c)], axis=-1)
    sin = jnp.concatenate([-jnp.sin(ang_r), jnp.sin(ang_r), -jnp.sin(ang_c), jnp.sin(ang_c)], axis=-1)
    reps = DIFF_WIDTH // DIFF_HEAD_DIM
    return jnp.tile(cos, (1, reps)), jnp.tile(sin, (1, reps))


def _prepare(g_norm1, g_norm2, w_in, w_gate_fwd, b_gate_fwd, w_gate_bwd, b_gate_bwd, g_q_norm, g_k_norm,
             w_out, w_router, b_router, w_exp_gate, w_exp_up, w_exp_down):
    depth, d, _ = w_in.shape
    r = GLA_GATE_RANK
    gate_cols = jnp.pad(w_in[..., 2048:2048 + 2 * r], ((0, 0), (0, 0), (0, GATE_COLS - 2 * r)))
    w_in_r = jnp.concatenate([w_in[..., :2048], w_in[..., 2048 + 2 * r:], gate_cols], axis=-1).astype(BF16)
    w_gate = jnp.zeros((depth, GATE_COLS, 2 * GLA_WIDTH), F32)
    w_gate = w_gate.at[:, 0:r, 0:GLA_WIDTH].set(w_gate_fwd).at[:, r:2 * r, GLA_WIDTH:].set(w_gate_bwd).astype(BF16)
    b_gate = jnp.concatenate([b_gate_fwd, b_gate_bwd], axis=-1).reshape(depth, 1, 2 * GLA_WIDTH)
    grp = jnp.arange(DIFF_WIDTH) // DIFF_HEAD_DIM
    gmat = jnp.where(grp[:, None] == grp[None, :], 1.0 / DIFF_HEAD_DIM, 0.0).astype(BF16)
    reps = DIFF_WIDTH // DIFF_HEAD_DIM
    tok = jnp.arange(TOKEN_TILE)
    same_chunk = (tok[:, None] // GLA_CHUNK) == (tok[None, :] // GLA_CHUNK)
    tri = jnp.stack([same_chunk & (tok[:, None] >= tok[None, :]),
                     same_chunk & (tok[:, None] <= tok[None, :])]).astype(BF16)
    return {
        "tri": tri,
        "g1": g_norm1.reshape(depth, 1, d),
        "g2": g_norm2.reshape(depth, 1, d),
        "w_in": w_in_r,
        "w_gate": w_gate,
        "b_gate": b_gate,
        "gmat": gmat,
        "gq": jnp.tile(g_q_norm, (1, reps)).reshape(depth, 1, DIFF_WIDTH),
        "gk": jnp.tile(g_k_norm, (1, reps)).reshape(depth, 1, DIFF_WIDTH),
        "w_out": w_out.astype(BF16),
        "w_router_t": w_router.T,
        "b_router": b_router.reshape(N_EXPERTS, 1),
        "w_eg": w_exp_gate.astype(BF16),
        "w_eu": w_exp_up.astype(BF16),
        "w_ed": w_exp_down.astype(BF16),
    }


def kernel(x_prompt, x_sample, cache_k, cache_v, state_gla_fwd, state_gla_bwd, c, c_ctx, w_mod, b_mod, g_norm1,
           g_norm2, w_in, w_gate_fwd, b_gate_fwd, w_gate_bwd, b_gate_bwd, g_gla_out, g_q_norm, g_k_norm, lambda_q1,
           lambda_k1, lambda_q2, lambda_k2, g_diff_out, w_out, w_router, b_router, w_exp_gate, w_exp_up, w_exp_down):
    nb_p, seq_p, d = x_prompt.shape
    nb_s, seq_s, _ = x_sample.shape
    depth = w_in.shape[0]
    past = cache_k.shape[2]

    prep = _prepare(g_norm1, g_norm2, w_in, w_gate_fwd, b_gate_fwd, w_gate_bwd, b_gate_bwd, g_q_norm, g_k_norm,
                    w_out, w_router, b_router, w_exp_gate, w_exp_up, w_exp_down)
    rope = _rope_tables(seq_s)

    mod_rows = 8 * ((1 + nb_s + 7) // 8)
    cs = jnp.zeros((mod_rows, d), F32).at[0].set(c_ctx).at[1:1 + nb_s].set(c)
    mod_all = _modulation(cs, w_mod, b_mod)
    ctx_k = cache_k.reshape(nb_s, depth, past, DIFF_WIDTH)
    ctx_v = cache_v.reshape(nb_s, depth, past, DIFF_WIDTH)
    gg = g_gla_out.reshape(depth, 1, GLA_DV)
    gd = g_diff_out.reshape(depth, DIFF_V_DIM, 1)
    lams = [a.reshape(depth, 1, DIFF_HEAD_DIM) for a in (lambda_q1, lambda_k1, lambda_q2, lambda_k2)]

    xp = x_prompt.reshape(nb_p * seq_p, d)
    xs = x_sample.reshape(nb_s * seq_s, d)
    yp = ys = None
    mod_prev = None
    new_k, new_v, new_sf, new_sb = [], [], [], []
    for l in range(depth):
        lam_init = 0.8 - 0.6 * math.exp(-0.3 * l)
        mod = mod_all[l].reshape(mod_rows, 1, 6 * d)
        xp, gla_in, la, qt, kd, vt, k_l, v_l = _pre_mixer(l, xp, yp, mod_prev, mod, prep, None, nb_p, seq_p, False)
        og, sf_l, sb_l = _gla(l, gla_in, la, gg, None, None, nb_p, seq_p)
        od = _diff_attention(l, lam_init, qt, kd, vt, None, None, lams, gd, nb_p, seq_p)
        xp, h2, route = _post_mixer(l, og, od, xp, mod, prep, seq_p, False)
        yp = _moe(l, h2, route, prep)
        new_k.append(k_l)
        new_v.append(v_l)
        new_sf.append(sf_l)
        new_sb.append(sb_l)
        xs, gla_in, la, qt, kd, vt = _pre_mixer(l, xs, ys, mod_prev, mod, prep, rope, nb_s, seq_s, True)
        og, _, _ = _gla(l, gla_in, la, gg, state_gla_fwd, state_gla_bwd, nb_s, seq_s)
        od = _diff_attention(l, lam_init, qt, kd, vt, ctx_k, ctx_v, lams, gd, nb_s, seq_s)
        xs, h2, route = _post_mixer(l, og, od, xs, mod, prep, seq_s, True)
        ys = _moe(l, h2, route, prep)
        mod_prev = mod
    xp = _residual(xp, yp, mod_prev, seq_p, False)
    xs = _residual(xs, ys, mod_prev, seq_s, True)

    new_cache_k = jnp.concatenate(new_k, axis=1).reshape(nb_p, depth, seq_p, DIFF_HEADS, 2, DIFF_HEAD_DIM)
    new_cache_v = jnp.concatenate(new_v, axis=1).reshape(nb_p, depth, seq_p, DIFF_HEADS, DIFF_V_DIM)
    new_sf = jnp.stack(new_sf, axis=1)
    new_sb = jnp.stack(new_sb, axis=1)
    return (xp.reshape(nb_p, seq_p, d), xs.reshape(nb_s, seq_s, d), new_cache_k, new_cache_v, new_sf, new_sb)
```

```python
import functools
import math

import jax
import jax.numpy as jnp
from jax import lax
from jax.experimental import pallas as pl
from jax.experimental.pallas import tpu as pltpu

F32 = jnp.float32
BF16 = jnp.bfloat16

D_MODEL = 1024
GLA_HEADS = 4
GLA_DK = 128
GLA_DV = 128
GLA_WIDTH = GLA_HEADS * GLA_DK
GLA_GATE_RANK = 16
GLA_GATE_NORMALIZER = 16.0
GLA_CHUNK = 64
GLA_HEADS_PER_STEP = 4
DIFF_HEADS = 4
DIFF_HEAD_DIM = 64
DIFF_V_DIM = 128
DIFF_WIDTH = DIFF_HEADS * 2 * DIFF_HEAD_DIM
ROPE_HALF = DIFF_HEAD_DIM // 2
ROPE_BASE = 10000.0
GRID_W = 64
N_EXPERTS = 16
N_GROUPS = 4
EXPERTS_PER_GROUP = 4
D_EXPERT = 512
NORM_EPS = 1e-6
LOG2E = 1.4426950408889634

PAIR_A = (0, 0, 0, 1, 1, 3)
PAIR_B = (1, 2, 3, 3, 2, 2)
N_PAIRS = len(PAIR_A)
N_COMBOS = N_GROUPS * N_PAIRS

GATE_COLS = 128
MAIN_COLS = 7 * 512
TOKEN_TILE = 256
POST_TILE = 1024
PRE_TILE = 512
MOE_TILE = 256
DMA_UNROLL = 8
ATTN_TQ = 1024
ATTN_TK = 256
ATTN_GROUP = 128
ATTN_UNROLL = 16
ATTN_LOOKAHEAD = 8
ONES_ROWS = 16
VMEM_LIMIT = 56 * 1024 * 1024


def _cparams(sem):
    return pltpu.CompilerParams(dimension_semantics=sem, vmem_limit_bytes=VMEM_LIMIT)


def _dot(a, b):
    return jnp.dot(a, b, preferred_element_type=F32)


def _dot_nt(a, b):
    return lax.dot_general(a, b, (((1,), (1,)), ((), ())), preferred_element_type=F32)


def _dot_tn(a, b):
    return lax.dot_general(a, b, (((0,), (0,)), ((), ())), preferred_element_type=F32)


def _sigmoid(x):
    return 1.0 / (1.0 + jnp.exp(-x))


def _split_bf16(x):
    hi = x.astype(BF16)
    lo = (x - hi.astype(F32)).astype(BF16)
    return hi, lo


def _mod_kernel(c_ref, w_ref, b_ref, o_ref):
    c = c_ref[...]
    s = c * _sigmoid(c)
    o_ref[0] = jnp.dot(s, w_ref[0], preferred_element_type=F32,
                       precision=lax.Precision.HIGHEST) + b_ref[0]


def _modulation(cs, w_mod, b_mod):
    depth, d, n = w_mod.shape
    rows = cs.shape[0]
    tn = 1536
    return pl.pallas_call(
        _mod_kernel,
        grid=(depth, n // tn),
        in_specs=[pl.BlockSpec((rows, d), lambda l, j: (0, 0)),
                  pl.BlockSpec((1, d, tn), lambda l, j: (l, 0, j)),
                  pl.BlockSpec((1, 1, tn), lambda l, j: (l, 0, j))],
        out_specs=pl.BlockSpec((1, rows, tn), lambda l, j: (l, 0, j)),
        out_shape=jax.ShapeDtypeStruct((depth, rows, n), F32),
        compiler_params=_cparams(("arbitrary", "arbitrary")),
        name="modulation",
    )(cs, w_mod, b_mod.reshape(depth, 1, n))


def _rope(y, cos, sin):
    w = y.shape[1]
    lane = lax.broadcasted_iota(jnp.int32, y.shape, 1)
    first = (lane % (2 * (ROPE_HALF // 2))) < (ROPE_HALF // 2)
    partner = jnp.where(first, pltpu.roll(y, w - ROPE_HALF // 2, axis=1), pltpu.roll(y, ROPE_HALF // 2, axis=1))
    return y * cos + partner * sin


def _pre_kernel(has_moe, is_sample, *refs):
    refs = list(refs)
    x_ref = refs.pop(0)
    if has_moe:
        y_ref, modp_ref = refs[:2]
        refs = refs[2:]
    mod_ref, g1_ref, win_ref, wgate_ref, bgate_ref, tri_ref, gmat_ref, gq_ref, gk_ref = refs[:9]
    refs = refs[9:]
    if is_sample:
        cos_ref, sin_ref = refs[:2]
        refs = refs[2:]
    xo_ref = refs.pop(0) if has_moe else None
    gla_ref, la_ref, qt_ref, kd_ref, vt_ref = refs[:5]
    refs = refs[5:]
    if not is_sample:
        kc_ref, vc_ref = refs

    d = D_MODEL
    sub = TOKEN_TILE
    m = mod_ref[0]
    gs = g1_ref[0] * (1.0 + m[:, d:2 * d])
    scale = DIFF_HEAD_DIM ** -0.5 * LOG2E
    w = GLA_WIDTH
    subs = [slice(j * sub, (j + 1) * sub) for j in range(x_ref.shape[0] // sub)]
    hbs = []
    for rs in subs:
        x = x_ref[rs, :]
        if has_moe:
            x = x + modp_ref[0][:, 5 * d:6 * d] * y_ref[rs, :].astype(F32)
            xo_ref[rs, :] = x
        ms = jnp.mean(x * x, axis=-1, keepdims=True)
        hbs.append((x * lax.rsqrt(ms + NORM_EPS) * gs + m[:, 0:d]).astype(BF16))

    def put_t(ref, j, rs, val):
        if is_sample:
            ref[0, :, rs] = val.astype(BF16).T
        else:
            ref[j] = val.astype(BF16).T

    hb_all = jnp.concatenate(hbs, axis=0)

    def plain(c):
        p = _dot(hb_all, win_ref[0, :, c * 512:(c + 1) * 512])
        if c == 0:
            p = p * (GLA_DK ** -0.5)
        gla_ref[:, c * 512:(c + 1) * 512] = p.astype(BF16)

    q1 = [_dot(hb, win_ref[0, :, 2048:2560]) for hb in hbs]
    k1 = [_dot(hb, win_ref[0, :, 2560:3072]) for hb in hbs]
    a1 = [_dot(hb, win_ref[0, :, MAIN_COLS:MAIN_COLS + GATE_COLS]) for hb in hbs]
    plain(0)
    msq = [_dot((p * p).astype(BF16), gmat_ref[...]) for p in q1]
    msk = [_dot((p * p).astype(BF16), gmat_ref[...]) for p in k1]
    zs = [_dot(a.astype(BF16), wgate_ref[0]) + bgate_ref[0] for a in a1]
    plain(1)
    for rs, z in zip(subs, zs):
        la = (jnp.minimum(z, 0.0) - jnp.log1p(jnp.exp(-jnp.abs(z)))) * (1.0 / GLA_GATE_NORMALIZER)
        la_hi, la_lo = _split_bf16(la)
        la_ref[rs, 0:w] = _dot(tri_ref[0], la_hi[:, 0:w]) + _dot(tri_ref[0], la_lo[:, 0:w])
        la_ref[rs, w:2 * w] = _dot(tri_ref[1], la_hi[:, w:2 * w]) + _dot(tri_ref[1], la_lo[:, w:2 * w])
    for j, rs in enumerate(subs):
        q = q1[j] * lax.rsqrt(msq[j] + NORM_EPS) * gq_ref[0]
        k = k1[j] * lax.rsqrt(msk[j] + NORM_EPS) * gk_ref[0]
        if is_sample:
            q = _rope(q, cos_ref[rs, :], sin_ref[rs, :])
            k = _rope(k, cos_ref[rs, :], sin_ref[rs, :])
        else:
            kc_ref[j, 0] = k
        put_t(qt_ref, j, rs, q * scale)
        kd_ref[rs, :] = k.astype(BF16)
    plain(2)
    for j, (rs, hb) in enumerate(zip(subs, hbs)):
        v = _dot(hb, win_ref[0, :, 3072:3584])
        if not is_sample:
            vc_ref[j, 0] = v
        put_t(vt_ref, j, rs, v)
    plain(3)


def _pre_mixer(l, x, y, mod_prev, mod, prep, rope, nb, seq, is_sample):
    t, d = x.shape
    sub = TOKEN_TILE
    tt = PRE_TILE
    has_moe = y is not None
    if is_sample:
        assert seq % tt == 0
        tiles_per_batch = seq // tt
        mod_map = lambda i: (1 + i // tiles_per_batch, 0, 0)
    else:
        assert seq == sub and t % tt == 0
        mod_map = lambda i: (0, 0, 0)
    row = lambda i: (i, 0)
    const2 = lambda i: (0, 0)
    lay3 = lambda i: (l, 0, 0)

    ins = [x]
    in_specs = [pl.BlockSpec((tt, d), row)]
    if has_moe:
        ins += [y, mod_prev]
        in_specs += [pl.BlockSpec((tt, d), row), pl.BlockSpec((1, 1, 6 * d), mod_map)]
    ins += [mod, prep["g1"], prep["w_in"], prep["w_gate"], prep["b_gate"], prep["tri"], prep["gmat"], prep["gq"],
            prep["gk"]]
    in_specs += [pl.BlockSpec((1, 1, 6 * d), mod_map),
                 pl.BlockSpec((1, 1, d), lay3),
                 pl.BlockSpec((1, d, MAIN_COLS + GATE_COLS), lay3, pipeline_mode=pl.Buffered(1)),
                 pl.BlockSpec((1, GATE_COLS, 2 * GLA_WIDTH), lay3),
                 pl.BlockSpec((1, 1, 2 * GLA_WIDTH), lay3),
                 pl.BlockSpec((2, sub, sub), lambda i: (0, 0, 0)),
                 pl.BlockSpec((DIFF_WIDTH, DIFF_WIDTH), const2),
                 pl.BlockSpec((1, 1, DIFF_WIDTH), lay3),
                 pl.BlockSpec((1, 1, DIFF_WIDTH), lay3)]
    if is_sample:
        ins += [rope[0], rope[1]]
        in_specs += [pl.BlockSpec((tt, DIFF_WIDTH), lambda i: (i % tiles_per_batch, 0))] * 2

    out_shape = []
    out_specs = []
    if has_moe:
        out_shape.append(jax.ShapeDtypeStruct((t, d), F32))
        out_specs.append(pl.BlockSpec((tt, d), row))
    if is_sample:
        tr_spec = pl.BlockSpec((1, DIFF_WIDTH, tt), lambda i: (i // tiles_per_batch, 0, i % tiles_per_batch))
    else:
        tr_spec = pl.BlockSpec((tt // sub, DIFF_WIDTH, seq), lambda i: (i, 0, 0))
    out_shape += [jax.ShapeDtypeStruct((t, 4 * GLA_WIDTH), BF16),
                  jax.ShapeDtypeStruct((t, 2 * GLA_WIDTH), F32),
                  jax.ShapeDtypeStruct((nb, DIFF_WIDTH, seq), BF16),
                  jax.ShapeDtypeStruct((t, DIFF_WIDTH), BF16),
                  jax.ShapeDtypeStruct((nb, DIFF_WIDTH, seq), BF16)]
    out_specs += [pl.BlockSpec((tt, 4 * GLA_WIDTH), row),
                  pl.BlockSpec((tt, 2 * GLA_WIDTH), row),
                  tr_spec,
                  pl.BlockSpec((tt, DIFF_WIDTH), row),
                  tr_spec]
    if not is_sample:
        out_shape += [jax.ShapeDtypeStruct((nb, 1, seq, DIFF_WIDTH), F32)] * 2
        out_specs += [pl.BlockSpec((tt // sub, 1, seq, DIFF_WIDTH), lambda i: (i, 0, 0, 0))] * 2

    outs = pl.pallas_call(
        functools.partial(_pre_kernel, has_moe, is_sample),
        grid=(t // tt,),
        in_specs=in_specs,
        out_specs=out_specs,
        out_shape=out_shape,
        compiler_params=_cparams(("arbitrary",)),
        name="pre_mixer",
    )(*ins)
    outs = list(outs)
    x_new = outs.pop(0) if has_moe else x
    return [x_new] + outs


def _gla_kernel(has_state, n_chunks, hg, *refs):
    refs = list(refs)
    q_ref, k_ref, v_ref, g_ref, bf_ref, bb_ref, gg_ref = refs[:7]
    refs = refs[7:]
    if has_state:
        s0f_ref, s0b_ref = refs[:2]
        refs = refs[2:]
    o_ref, sf_ref, sb_ref, st_ref, acc_ref = refs

    c = GLA_CHUNK
    dk = GLA_DK
    for hd in range(hg):
        if has_state:
            st_ref[2 * hd] = s0f_ref[0, 0, hd].T
            st_ref[2 * hd + 1] = s0b_ref[0, 0, hd].T
        else:
            st_ref[2 * hd] = jnp.zeros((GLA_DV, dk), F32)
            st_ref[2 * hd + 1] = jnp.zeros((GLA_DV, dk), F32)

    r = lax.broadcasted_iota(jnp.int32, (c, c), 0)
    s = lax.broadcasted_iota(jnp.int32, (c, c), 1)
    lower = r >= s
    upper = r <= s

    def scores(rows, hd, b_ref, st_i, mid_row, last_row):
        cs = slice(hd * dk, (hd + 1) * dk)
        q = q_ref[rows, cs].astype(F32)
        k = k_ref[rows, cs].astype(F32)
        v = v_ref[rows, cs]
        b = b_ref[rows, cs]
        mid = b[mid_row:mid_row + 1]
        last = b[last_row:last_row + 1]
        qe = q * jnp.exp(b - mid)
        ke = k * jnp.exp(mid - b)
        qi = (qe * jnp.exp(mid)).astype(BF16)
        ks = (ke * jnp.exp(last - mid)).astype(BF16)
        qe = qe.astype(BF16)
        ke = ke.astype(BF16)
        st = st_ref[st_i]
        return _dot_nt(qe, ke), _dot_nt(qi, st.astype(BF16)), _dot_tn(v, ks), v, st, last

    def outputs(sc, mask, st_i):
        att, o_inter, kv, v, st, last = sc
        st_ref[st_i] = st * jnp.exp(last) + kv
        return _dot(jnp.where(mask, att, 0.0).astype(BF16), v) + o_inter

    def finish(o, rows, hd):
        cs = slice(hd * dk, (hd + 1) * dk)
        ms = jnp.mean(o * o, axis=-1, keepdims=True)
        g = g_ref[rows, cs].astype(F32)
        o_ref[rows, cs] = (o * lax.rsqrt(ms + NORM_EPS) * gg_ref[0] * (g * _sigmoid(g))).astype(BF16)

    def step(n, second_visit):
        rf = pl.ds(pl.multiple_of(n * c, c), c)
        rb = pl.ds(pl.multiple_of((n_chunks - 1 - n) * c, c), c)
        sc = []
        for hd in range(hg):
            sc.append(scores(rf, hd, bf_ref, 2 * hd, c // 2, c - 1))
            sc.append(scores(rb, hd, bb_ref, 2 * hd + 1, c - 1 - c // 2, 0))
        for hd in range(hg):
            cs = slice(hd * dk, (hd + 1) * dk)
            for rows, sci, mask, st_i in ((rf, sc[2 * hd], lower, 2 * hd), (rb, sc[2 * hd + 1], upper, 2 * hd + 1)):
                o = outputs(sci, mask, st_i)
                if second_visit:
                    finish(acc_ref[rows, cs] + o, rows, hd)
                else:
                    acc_ref[rows, cs] = o

    half = n_chunks // 2
    per_trip = max(u for u in (1, 2, 4) if half % u == 0)

    def trips(first, second_visit):
        def body(n, carry):
            for u in range(per_trip):
                step(first + n * per_trip + u, second_visit)
            return carry
        lax.fori_loop(0, half // per_trip, body, 0)

    trips(0, False)
    trips(half, True)

    for hd in range(hg):
        sf_ref[0, hd] = st_ref[2 * hd].T
        sb_ref[0, hd] = st_ref[2 * hd + 1].T


def _gla(l, gla_in, bsum, gg, s0f, s0b, nb, seq):
    t = gla_in.shape[0]
    h = GLA_HEADS
    hg = GLA_HEADS_PER_STEP
    nhb = h // hg
    w = hg * GLA_DK
    has_state = s0f is not None
    n_chunks = seq // GLA_CHUNK
    assert n_chunks % 2 == 0
    col = lambda off: (lambda b, hh: (b, off + hh))
    in_bytes = seq * w * (4 * 2 + 2 * 4)
    mode = pl.Buffered(1) if 2 * in_bytes > VMEM_LIMIT // 2 else None
    big = lambda off: pl.BlockSpec((seq, w), col(off), pipeline_mode=mode)
    ins = [gla_in, gla_in, gla_in, gla_in, bsum, bsum, gg]
    in_specs = [big(0), big(nhb), big(2 * nhb), big(3 * nhb), big(0), big(nhb),
                pl.BlockSpec((1, 1, GLA_DV), lambda b, hh: (l, 0, 0))]
    if has_state:
        ins += [s0f, s0b]
        in_specs += [pl.BlockSpec((1, 1, hg, GLA_DK, GLA_DV), lambda b, hh: (b, l, hh, 0, 0))] * 2
    return pl.pallas_call(
        functools.partial(_gla_kernel, has_state, n_chunks, hg),
        grid=(nb, nhb),
        in_specs=in_specs,
        out_specs=[pl.BlockSpec((seq, w), col(0)),
                   pl.BlockSpec((1, hg, GLA_DK, GLA_DV), lambda b, hh: (b, hh, 0, 0)),
                   pl.BlockSpec((1, hg, GLA_DK, GLA_DV), lambda b, hh: (b, hh, 0, 0))],
        out_shape=[jax.ShapeDtypeStruct((t, GLA_WIDTH), BF16),
                   jax.ShapeDtypeStruct((nb, h, GLA_DK, GLA_DV), F32),
                   jax.ShapeDtypeStruct((nb, h, GLA_DK, GLA_DV), F32)],
        scratch_shapes=[pltpu.VMEM((2 * hg, GLA_DV, GLA_DK), F32), pltpu.VMEM((seq, w), F32)],
        compiler_params=_cparams(("arbitrary", "arbitrary")),
        name="gla",
    )(*ins)


def _attn_kernel(lam_init, has_ctx, n_kt, tk, *refs):
    refs = list(refs)
    qt_ref, k_ref, vt_ref = refs[:3]
    refs = refs[3:]
    if has_ctx:
        ck_ref, cv_ref = refs[:2]
        refs = refs[2:]
    lq1_ref, lk1_ref, lq2_ref, lk2_ref, gd_ref, o_ref = refs

    qt = qt_ref[0]
    tq = qt.shape[1]
    gq = ATTN_GROUP
    n_groups = tq // gq
    dim = lax.broadcasted_iota(jnp.int32, (2 * DIFF_HEAD_DIM, gq), 0)
    zero = jnp.zeros((2 * DIFF_HEAD_DIM, gq), BF16)
    qws = []
    for g in range(n_groups):
        qg = qt[:, g * gq:(g + 1) * gq]
        qws.append(jnp.concatenate([jnp.where(dim < DIFF_HEAD_DIM, qg, zero),
                                    jnp.where(dim >= DIFF_HEAD_DIM, qg, zero)], axis=1))

    def absorb(group_state, st, vt):
        m, acc = group_state
        m_new = jnp.maximum(m, jnp.max(st, axis=0, keepdims=True))
        alpha = jnp.exp2(m - m_new)
        p = jnp.exp2(st - m_new)
        return m_new, alpha * acc + _dot(vt, p.astype(BF16))

    def process(state, tiles):
        units = [(u, g) for u in range(len(tiles)) for g in range(n_groups)]
        score = lambda idx: _dot(tiles[units[idx][0]][0], qws[units[idx][1]])
        state = list(state)
        pending = [score(idx) for idx in range(min(ATTN_LOOKAHEAD, len(units)))]
        for idx, (u, g) in enumerate(units):
            if idx + ATTN_LOOKAHEAD < len(units):
                pending.append(score(idx + ATTN_LOOKAHEAD))
            state[g] = absorb(state[g], pending.pop(0), tiles[u][1])
        return tuple(state)

    def with_ones(vt):
        return jnp.concatenate([vt, jnp.ones((ONES_ROWS, vt.shape[1]), BF16)], axis=0)

    state = tuple((jnp.full((1, 2 * gq), -jnp.inf, F32), jnp.zeros((DIFF_V_DIM + ONES_ROWS, 2 * gq), F32))
                  for _ in range(n_groups))
    ctx_tiles = []
    if has_ctx:
        past = ck_ref.shape[2]
        ctk = tk if past % tk == 0 else past
        cvt = cv_ref[0, 0].T.astype(BF16)
        ctx_tiles = [(ck_ref[0, 0, j * ctk:(j + 1) * ctk, :].astype(BF16),
                      with_ones(cvt[:, j * ctk:(j + 1) * ctk])) for j in range(past // ctk)]

    unroll = max(u for u in range(1, ATTN_UNROLL + 1) if n_kt % u == 0)
    if unroll == n_kt:
        latent = [(k_ref[u * tk:(u + 1) * tk, :], with_ones(vt_ref[0, :, u * tk:(u + 1) * tk]))
                  for u in range(n_kt)]
        state = process(state, ctx_tiles + latent)
    else:
        if ctx_tiles:
            state = process(state, ctx_tiles)

        def body(j, st):
            tiles = []
            for u in range(unroll):
                rr = pl.ds(pl.multiple_of((j * unroll + u) * tk, tk), tk)
                tiles.append((k_ref[rr, :], with_ones(vt_ref[0, :, rr])))
            return process(st, tiles)

        state = lax.fori_loop(0, n_kt // unroll, body, state)

    lam = (jnp.exp(jnp.sum(lq1_ref[0] * lk1_ref[0], axis=-1, keepdims=True))
           - jnp.exp(jnp.sum(lq2_ref[0] * lk2_ref[0], axis=-1, keepdims=True)) + lam_init)
    for g in range(n_groups):
        m, acc = state[g]
        on = acc[:DIFF_V_DIM] / acc[DIFF_V_DIM:DIFF_V_DIM + 1]
        o = on[:, :gq] - lam * on[:, gq:]
        ms = jnp.mean(o * o, axis=0, keepdims=True)
        o = o * lax.rsqrt(ms + NORM_EPS) * gd_ref[0] * (1.0 - lam_init)
        o_ref[g * gq:(g + 1) * gq, :] = o.T.astype(BF16)


def _diff_attention(l, lam_init, qt, kd, vt, ctx_k, ctx_v, lams, gd, nb, seq):
    t = kd.shape[0]
    h = DIFF_HEADS
    has_ctx = ctx_k is not None
    tq = min(ATTN_TQ, seq)
    tk = min(ATTN_TK, seq)
    nq = seq // tq
    w = 2 * DIFF_HEAD_DIM
    ins = [qt, kd, vt]
    in_specs = [pl.BlockSpec((1, w, tq), lambda b, hh, i: (b, hh, i)),
                pl.BlockSpec((seq, w), lambda b, hh, i: (b, hh)),
                pl.BlockSpec((1, DIFF_V_DIM, seq), lambda b, hh, i: (b, hh, 0))]
    if has_ctx:
        past = ctx_k.shape[2]
        ins += [ctx_k, ctx_v]
        in_specs += [pl.BlockSpec((1, 1, past, w), lambda b, hh, i: (b, l, 0, hh)),
                     pl.BlockSpec((1, 1, past, DIFF_V_DIM), lambda b, hh, i: (b, l, 0, hh))]
    lay3 = lambda b, hh, i: (l, 0, 0)
    ins += list(lams) + [gd]
    in_specs += [pl.BlockSpec((1, 1, DIFF_HEAD_DIM), lay3)] * 4 + [pl.BlockSpec((1, DIFF_V_DIM, 1), lay3)]
    return pl.pallas_call(
        functools.partial(_attn_kernel, lam_init, has_ctx, seq // tk, tk),
        grid=(nb, h, nq),
        in_specs=in_specs,
        out_specs=pl.BlockSpec((tq, DIFF_V_DIM), lambda b, hh, i: (b * nq + i, hh)),
        out_shape=jax.ShapeDtypeStruct((t, DIFF_WIDTH), BF16),
        compiler_params=_cparams(("arbitrary", "arbitrary", "arbitrary")),
        name="diff_attention",
    )(*ins)


def _top2_of4(v):
    m1 = jnp.maximum(jnp.maximum(v[0], v[1]), jnp.maximum(v[2], v[3]))
    i1 = jnp.where(v[0] == m1, 0, jnp.where(v[1] == m1, 1, jnp.where(v[2] == m1, 2, 3)))
    neg = jnp.full_like(m1, -jnp.inf)
    w = [jnp.where(i1 == j, neg, v[j]) for j in range(4)]
    m2 = jnp.maximum(jnp.maximum(w[0], w[1]), jnp.maximum(w[2], w[3]))
    i2 = jnp.where(w[0] == m2, 0, jnp.where(w[1] == m2, 1, jnp.where(w[2] == m2, 2, 3)))
    return m1, i1, m2, i2


def _post_kernel(og_ref, od_ref, x_ref, mod_ref, g2_ref, wo_ref, wr_ref, br_ref, x1_ref, hx_ref, route_ref):
    d = D_MODEL
    sub = TOKEN_TILE
    n_sub = x_ref.shape[0] // sub
    m = mod_ref[0]
    rows = [slice(j * sub, (j + 1) * sub) for j in range(n_sub)]
    outs = [_dot(og_ref[rs, :], wo_ref[0, 0:GLA_WIDTH, :]) + _dot(od_ref[rs, :], wo_ref[0, GLA_WIDTH:, :])
            for rs in rows]
    w_hi, w_lo = _split_bf16(wr_ref[...])
    w_hl = jnp.concatenate([w_hi, w_lo], axis=0)
    zs = []
    for rs, out in zip(rows, outs):
        x1 = x_ref[rs, :] + m[:, 2 * d:3 * d] * out
        x1_ref[rs, :] = x1
        ms = jnp.mean(x1 * x1, axis=-1, keepdims=True)
        h2 = x1 * lax.rsqrt(ms + NORM_EPS) * (g2_ref[0] * (1.0 + m[:, 4 * d:5 * d])) + m[:, 3 * d:4 * d]
        h_hi, h_lo = _split_bf16(h2)
        hx_ref[rs, 0:d] = h2
        za = _dot_nt(w_hl, h_hi)
        zs.append(za[0:N_EXPERTS] + za[N_EXPERTS:2 * N_EXPERTS] + _dot_nt(w_hi, h_lo))
    for rs, z in zip(rows, zs):
        _route(z, br_ref, hx_ref, route_ref, rs)


def _route(z, br_ref, hx_ref, route_ref, rs):
    d = D_MODEL
    s = _sigmoid(z)
    sel = s + br_ref[...]

    e = EXPERTS_PER_GROUP
    tops = []
    for g in range(N_GROUPS):
        tops.append(_top2_of4([sel[g * e + j:g * e + j + 1, :] for j in range(e)]))
    score = [t[0] + t[2] for t in tops]
    best = jnp.maximum(jnp.maximum(score[0], score[1]), jnp.maximum(score[2], score[3]))
    gi = jnp.where(score[0] == best, 0, jnp.where(score[1] == best, 1, jnp.where(score[2] == best, 2, 3)))

    def pick(rows):
        return jnp.where(gi == 0, rows[0], jnp.where(gi == 1, rows[1], jnp.where(gi == 2, rows[2], rows[3])))

    i1 = pick([t[1] for t in tops])
    i2 = pick([t[3] for t in tops])

    def gate_of(idx):
        per_group = []
        for g in range(N_GROUPS):
            rows = [s[g * e + j:g * e + j + 1, :] for j in range(e)]
            per_group.append(jnp.where(idx == 0, rows[0], jnp.where(idx == 1, rows[1],
                                                                     jnp.where(idx == 2, rows[2], rows[3]))))
        return pick(per_group)

    s1 = gate_of(i1)
    s2 = gate_of(i2)
    tot = s1 + s2
    w1 = s1 / tot
    w2 = s2 / tot
    lo = jnp.minimum(i1, i2)
    hi = jnp.maximum(i1, i2)
    w_lo = jnp.where(i1 < i2, w1, w2)
    w_hi = jnp.where(i1 < i2, w2, w1)
    pair = jnp.where(lo == 0, hi - 1, jnp.where(lo == 1, jnp.where(hi == 3, 3, 4), 5))
    swapped = pair == 5
    gate_a = jnp.where(swapped, w_hi, w_lo)
    gate_b = jnp.where(swapped, w_lo, w_hi)
    route_ref[:, rs] = jnp.broadcast_to(gi * N_PAIRS + pair, (route_ref.shape[0], gate_a.shape[1]))
    r = lax.broadcasted_iota(jnp.int32, (GATE_COLS, gate_a.shape[1]), 0)
    gates_t = jnp.where(r == 0, gate_a, jnp.where(r == 1, gate_b, 0.0))
    hx_ref[rs, d:d + GATE_COLS] = gates_t.T


def _post_mixer(l, og, od, x, mod, prep, seq, is_sample):
    t, d = x.shape
    tt = POST_TILE if (seq % POST_TILE == 0 or not is_sample) and t % POST_TILE == 0 else TOKEN_TILE
    tiles_per_batch = max(seq // tt, 1)
    if is_sample:
        mod_map = lambda i: (1 + i // tiles_per_batch, 0, 0)
    else:
        mod_map = lambda i: (0, 0, 0)
    row = lambda i: (i, 0)
    lay3 = lambda i: (l, 0, 0)
    return pl.pallas_call(
        _post_kernel,
        grid=(t // tt,),
        in_specs=[pl.BlockSpec((tt, GLA_WIDTH), row),
                  pl.BlockSpec((tt, DIFF_WIDTH), row),
                  pl.BlockSpec((tt, d), row),
                  pl.BlockSpec((1, 1, 6 * d), mod_map),
                  pl.BlockSpec((1, 1, d), lay3),
                  pl.BlockSpec((1, d, d), lay3),
                  pl.BlockSpec((N_EXPERTS, d), lambda i: (0, 0)),
                  pl.BlockSpec((N_EXPERTS, 1), lambda i: (0, 0))],
        out_specs=[pl.BlockSpec((tt, d), row),
                   pl.BlockSpec((tt, d + GATE_COLS), row),
                   pl.BlockSpec((8, tt), lambda i: (0, i))],
        out_shape=[jax.ShapeDtypeStruct((t, d), F32),
                   jax.ShapeDtypeStruct((t, d + GATE_COLS), F32),
                   jax.ShapeDtypeStruct((8, t), jnp.int32)],
        compiler_params=_cparams(("arbitrary",)),
        name="post_mixer",
    )(og, od, x, mod, prep["g2"], prep["w_out"], prep["w_router_t"], prep["b_router"])


def _moe_kernel(n_tiles, order_ref, first_ref, count_ref, ea_ref, eb_ref,
                hx_hbm, wga_ref, wua_ref, wda_ref, wgb_ref, wub_ref, wdb_ref,
                y_hbm, xbuf, ybuf, in_sem, out_sem):
    i = pl.program_id(0)
    slot = i % 2
    d = D_MODEL
    tm = xbuf.shape[1]
    n_tokens = hx_hbm.shape[0]

    def gather_copy(buf_slot, r, tok):
        return pltpu.make_async_copy(hx_hbm.at[pl.ds(tok, 1), :], xbuf.at[buf_slot, pl.ds(r, 1), :],
                                     in_sem.at[buf_slot])

    def scatter_copy(buf_slot, r, row):
        return pltpu.make_async_copy(ybuf.at[buf_slot, pl.ds(r, 1), :], y_hbm.at[pl.ds(row, 1), :],
                                     out_sem.at[buf_slot])

    def for_rows(fn):
        def body(q, c):
            for s in range(DMA_UNROLL):
                fn(q * DMA_UNROLL + s)
            return c
        lax.fori_loop(0, tm // DMA_UNROLL, body, 0)

    def wait_gather(buf_slot):
        for_rows(lambda r: gather_copy(buf_slot, r, 0).wait())

    def wait_scatter(buf_slot):
        for_rows(lambda r: scatter_copy(buf_slot, r, 0).wait())

    def scatter_row(buf_slot, base, n, r):
        return jnp.where(r < n, order_ref[base + r], n_tokens + buf_slot * tm + r)

    prev = jnp.maximum(i - 1, 0)

    @pl.when(i == 0)
    def _():
        ybuf[...] = jnp.zeros_like(ybuf)
        for s in range(2):
            spare_init = pltpu.make_async_copy(ybuf.at[s], y_hbm.at[pl.ds(n_tokens + s * tm, tm), :], out_sem.at[s])
            spare_init.start()
            spare_init.wait()
        base0 = first_ref[0]
        for_rows(lambda r: gather_copy(0, r, order_ref[base0 + r]).start())

    @pl.when((i == 0) | (count_ref[prev] > 0))
    def _():
        wait_gather(slot)

    @pl.when((i == 1) | ((i >= 2) & (count_ref[jnp.maximum(i - 2, 0)] > 0)))
    def _():
        wait_scatter(slot)

    @pl.when(count_ref[i] > 0)
    def _():
        xg = xbuf[slot]
        x = xg[:, 0:d].astype(BF16)
        base_next = first_ref[i + 1]
        base_prev = first_ref[prev]
        n_prev = jnp.where(i > 0, count_ref[prev], 0)
        per_dot = -(-tm // 6)

        def issue(part):
            for r in range(part * per_dot, min((part + 1) * per_dot, tm)):
                gather_copy(1 - slot, r, order_ref[base_next + r]).start()
                scatter_copy(1 - slot, r, scatter_row(1 - slot, base_prev, n_prev, r)).start()

        hg_a = _dot(x, wga_ref[0, 0])
        issue(0)
        hu_a = _dot(x, wua_ref[0, 0])
        issue(1)
        hg_b = _dot(x, wgb_ref[0, 0])
        issue(2)
        hu_b = _dot(x, wub_ref[0, 0])
        issue(3)
        out_a = _dot(((hg_a * _sigmoid(hg_a)) * hu_a).astype(BF16), wda_ref[0, 0])
        issue(4)
        out_b = _dot(((hg_b * _sigmoid(hg_b)) * hu_b).astype(BF16), wdb_ref[0, 0])
        issue(5)
        ybuf[slot] = xg[:, d:d + 1] * out_a + xg[:, d + 1:d + 2] * out_b

    @pl.when((count_ref[i] == 0) & (i > 0) & (count_ref[prev] > 0))
    def _():
        base_prev = first_ref[prev]
        n_prev = count_ref[prev]
        for_rows(lambda r: scatter_copy(1 - slot, r, scatter_row(1 - slot, base_prev, n_prev, r)).start())

    @pl.when((i == n_tiles - 1) & (count_ref[prev] > 0))
    def _():
        wait_scatter(1 - slot)


def _moe(l, hx, route, prep):
    t = hx.shape[0]
    d = D_MODEL
    tm = MOE_TILE
    n_tiles = t // tm + N_COMBOS

    combo = route[0]
    order = jnp.argsort(combo).astype(jnp.int32)
    order_padded = jnp.concatenate([order, jnp.zeros((tm,), jnp.int32)])
    counts = jnp.sum(combo[:, None] == jnp.arange(N_COMBOS, dtype=jnp.int32)[None, :], axis=0).astype(jnp.int32)
    tiles_of = (counts + tm - 1) // tm
    tile_end = jnp.cumsum(tiles_of)
    start = jnp.cumsum(counts) - counts
    tile = jnp.arange(n_tiles, dtype=jnp.int32)
    n_used = tile_end[-1]
    tile_combo = jnp.sum(jnp.minimum(tile, n_used - 1)[:, None] >= tile_end[None, :], axis=1).astype(jnp.int32)
    onehot = (tile_combo[:, None] == jnp.arange(N_COMBOS, dtype=jnp.int32)[None, :]).astype(jnp.int32)
    pick = lambda table: jnp.sum(onehot * table[None, :], axis=1)
    within = (tile - (pick(tile_end) - pick(tiles_of))) * tm
    first = jnp.clip(pick(start) + within, 0, t - 1).astype(jnp.int32)
    count = jnp.where(tile < n_used, jnp.clip(pick(counts) - within, 0, tm), 0).astype(jnp.int32)
    grp = tile_combo // N_PAIRS
    pr = tile_combo % N_PAIRS
    pair_a = jnp.sum((pr[:, None] == jnp.arange(N_PAIRS)[None, :]) * jnp.asarray(PAIR_A, jnp.int32)[None, :], axis=1)
    pair_b = jnp.sum((pr[:, None] == jnp.arange(N_PAIRS)[None, :]) * jnp.asarray(PAIR_B, jnp.int32)[None, :], axis=1)
    ea = (grp * EXPERTS_PER_GROUP + pair_a).astype(jnp.int32)
    eb = (grp * EXPERTS_PER_GROUP + pair_b).astype(jnp.int32)

    wa = lambda i, o_r, f_r, c_r, ea_r, eb_r: (l, ea_r[i], 0, 0)
    wb = lambda i, o_r, f_r, c_r, ea_r, eb_r: (l, eb_r[i], 0, 0)
    up = pl.BlockSpec((1, 1, d, D_EXPERT), wa)
    dn = pl.BlockSpec((1, 1, D_EXPERT, d), wa)
    upb = pl.BlockSpec((1, 1, d, D_EXPERT), wb)
    dnb = pl.BlockSpec((1, 1, D_EXPERT, d), wb)
    any_spec = pl.BlockSpec(memory_space=pl.ANY)
    return pl.pallas_call(
        functools.partial(_moe_kernel, n_tiles),
        grid_spec=pltpu.PrefetchScalarGridSpec(
            num_scalar_prefetch=5,
            grid=(n_tiles,),
            in_specs=[any_spec, up, up, dn, upb, upb, dnb],
            out_specs=any_spec,
            scratch_shapes=[pltpu.VMEM((2, tm, d + GATE_COLS), F32), pltpu.VMEM((2, tm, d), F32),
                            pltpu.SemaphoreType.DMA((2,)), pltpu.SemaphoreType.DMA((2,))]),
        out_shape=jax.ShapeDtypeStruct((t + 2 * tm, d), F32),
        compiler_params=_cparams(("arbitrary",)),
        name="moe",
    )(order_padded, first, count, ea, eb, hx, prep["w_eg"], prep["w_eu"], prep["w_ed"],
      prep["w_eg"], prep["w_eu"], prep["w_ed"])


def _resid_kernel(x_ref, y_ref, mod_ref, o_ref):
    d = D_MODEL
    o_ref[...] = x_ref[...] + mod_ref[0][:, 5 * d:6 * d] * y_ref[...].astype(F32)


def _residual(x, y, mod, seq, is_sample):
    t, d = x.shape
    tt = TOKEN_TILE
    tiles_per_batch = seq // tt
    if is_sample:
        mod_map = lambda i: (1 + i // tiles_per_batch, 0, 0)
    else:
        mod_map = lambda i: (0, 0, 0)
    row = lambda i: (i, 0)
    return pl.pallas_call(
        _resid_kernel,
        grid=(t // tt,),
        in_specs=[pl.BlockSpec((tt, d), row), pl.BlockSpec((tt, d), row), pl.BlockSpec((1, 1, 6 * d), mod_map)],
        out_specs=pl.BlockSpec((tt, d), row),
        out_shape=jax.ShapeDtypeStruct((t, d), F32),
        compiler_params=_cparams(("arbitrary",)),
        name="residual",
    )(x, y, mod)


def _rope_tables(length):
    rows = length // GRID_W
    row = jnp.repeat(jnp.arange(rows, dtype=F32), GRID_W)
    col = jnp.tile(jnp.arange(GRID_W, dtype=F32), rows)
    inv = ROPE_BASE ** (-jnp.arange(0, ROPE_HALF, 2, dtype=F32) / ROPE_HALF)
    ang_r = row[:, None] * inv[None, :]
    ang_c = col[:, None] * inv[None, :]
    cos = jnp.concatenate([jnp.cos(ang_r), jnp.cos(ang_r), jnp.cos(ang_c), jnp.cos(ang_c)], axis=-1)
    sin = jnp.concatenate([-jnp.sin(ang_r), jnp.sin(ang_r), -jnp.sin(ang_c), jnp.sin(ang_c)], axis=-1)
    reps = DIFF_WIDTH // DIFF_HEAD_DIM
    return jnp.tile(cos, (1, reps)), jnp.tile(sin, (1, reps))


def _prepare(g_norm1, g_norm2, w_in, w_gate_fwd, b_gate_fwd, w_gate_bwd, b_gate_bwd, g_q_norm, g_k_norm,
             w_out, w_router, b_router, w_exp_gate, w_exp_up, w_exp_down):
    depth, d, _ = w_in.shape
    r = GLA_GATE_RANK
    gate_cols = jnp.pad(w_in[..., 2048:2048 + 2 * r], ((0, 0), (0, 0), (0, GATE_COLS - 2 * r)))
    w_in_r = jnp.concatenate([w_in[..., :2048], w_in[..., 2048 + 2 * r:], gate_cols], axis=-1).astype(BF16)
    w_gate = jnp.zeros((depth, GATE_COLS, 2 * GLA_WIDTH), F32)
    w_gate = w_gate.at[:, 0:r, 0:GLA_WIDTH].set(w_gate_fwd).at[:, r:2 * r, GLA_WIDTH:].set(w_gate_bwd).astype(BF16)
    b_gate = jnp.concatenate([b_gate_fwd, b_gate_bwd], axis=-1).reshape(depth, 1, 2 * GLA_WIDTH)
    grp = jnp.arange(DIFF_WIDTH) // DIFF_HEAD_DIM
    gmat = jnp.where(grp[:, None] == grp[None, :], 1.0 / DIFF_HEAD_DIM, 0.0).astype(BF16)
    reps = DIFF_WIDTH // DIFF_HEAD_DIM
    tok = jnp.arange(TOKEN_TILE)
    same_chunk = (tok[:, None] // GLA_CHUNK) == (tok[None, :] // GLA_CHUNK)
    tri = jnp.stack([same_chunk & (tok[:, None] >= tok[None, :]),
                     same_chunk & (tok[:, None] <= tok[None, :])]).astype(BF16)
    return {
        "tri": tri,
        "g1": g_norm1.reshape(depth, 1, d),
        "g2": g_norm2.reshape(depth, 1, d),
        "w_in": w_in_r,
        "w_gate": w_gate,
        "b_gate": b_gate,
        "gmat": gmat,
        "gq": jnp.tile(g_q_norm, (1, reps)).reshape(depth, 1, DIFF_WIDTH),
        "gk": jnp.tile(g_k_norm, (1, reps)).reshape(depth, 1, DIFF_WIDTH),
        "w_out": w_out.astype(BF16),
        "w_router_t": w_router.T,
        "b_router": b_router.reshape(N_EXPERTS, 1),
        "w_eg": w_exp_gate.astype(BF16),
        "w_eu": w_exp_up.astype(BF16),
        "w_ed": w_exp_down.astype(BF16),
    }


def kernel(x_prompt, x_sample, cache_k, cache_v, state_gla_fwd, state_gla_bwd, c, c_ctx, w_mod, b_mod, g_norm1,
           g_norm2, w_in, w_gate_fwd, b_gate_fwd, w_gate_bwd, b_gate_bwd, g_gla_out, g_q_norm, g_k_norm, lambda_q1,
           lambda_k1, lambda_q2, lambda_k2, g_diff_out, w_out, w_router, b_router, w_exp_gate, w_exp_up, w_exp_down):
    nb_p, seq_p, d = x_prompt.shape
    nb_s, seq_s, _ = x_sample.shape
    depth = w_in.shape[0]
    past = cache_k.shape[2]

    prep = _prepare(g_norm1, g_norm2, w_in, w_gate_fwd, b_gate_fwd, w_gate_bwd, b_gate_bwd, g_q_norm, g_k_norm,
                    w_out, w_router, b_router, w_exp_gate, w_exp_up, w_exp_down)
    rope = _rope_tables(seq_s)

    mod_rows = 8 * ((1 + nb_s + 7) // 8)
    cs = jnp.zeros((mod_rows, d), F32).at[0].set(c_ctx).at[1:1 + nb_s].set(c)
    mod_all = _modulation(cs, w_mod, b_mod)
    ctx_k = cache_k.reshape(nb_s, depth, past, DIFF_WIDTH)
    ctx_v = cache_v.reshape(nb_s, depth, past, DIFF_WIDTH)
    gg = g_gla_out.reshape(depth, 1, GLA_DV)
    gd = g_diff_out.reshape(depth, DIFF_V_DIM, 1)
    lams = [a.reshape(depth, 1, DIFF_HEAD_DIM) for a in (lambda_q1, lambda_k1, lambda_q2, lambda_k2)]

    xp = x_prompt.reshape(nb_p * seq_p, d)
    xs = x_sample.reshape(nb_s * seq_s, d)
    yp = ys = None
    mod_prev = None
    new_k, new_v, new_sf, new_sb = [], [], [], []
    for l in range(depth):
        lam_init = 0.8 - 0.6 * math.exp(-0.3 * l)
        mod = mod_all[l].reshape(mod_rows, 1, 6 * d)
        xp, gla_in, la, qt, kd, vt, k_l, v_l = _pre_mixer(l, xp, yp, mod_prev, mod, prep, None, nb_p, seq_p, False)
        og, sf_l, sb_l = _gla(l, gla_in, la, gg, None, None, nb_p, seq_p)
        od = _diff_attention(l, lam_init, qt, kd, vt, None, None, lams, gd, nb_p, seq_p)
        xp, h2, route = _post_mixer(l, og, od, xp, mod, prep, seq_p, False)
        yp = _moe(l, h2, route, prep)
        new_k.append(k_l)
        new_v.append(v_l)
        new_sf.append(sf_l)
        new_sb.append(sb_l)
        xs, gla_in, la, qt, kd, vt = _pre_mixer(l, xs, ys, mod_prev, mod, prep, rope, nb_s, seq_s, True)
        og, _, _ = _gla(l, gla_in, la, gg, state_gla_fwd, state_gla_bwd, nb_s, seq_s)
        od = _diff_attention(l, lam_init, qt, kd, vt, ctx_k, ctx_v, lams, gd, nb_s, seq_s)
        xs, h2, route = _post_mixer(l, og, od, xs, mod, prep, seq_s, True)
        ys = _moe(l, h2, route, prep)
        mod_prev = mod
    xp = _residual(xp, yp, mod_prev, seq_p, False)
    xs = _residual(xs, ys, mod_prev, seq_s, True)

    new_cache_k = jnp.concatenate(new_k, axis=1).reshape(nb_p, depth, seq_p, DIFF_HEADS, 2, DIFF_HEAD_DIM)
    new_cache_v = jnp.concatenate(new_v, axis=1).reshape(nb_p, depth, seq_p, DIFF_HEADS, DIFF_V_DIM)
    new_sf = jnp.stack(new_sf, axis=1)
    new_sb = jnp.stack(new_sb, axis=1)
    return (xp.reshape(nb_p, seq_p, d), xs.reshape(nb_s, seq_s, d), new_cache_k, new_cache_v, new_sf, new_sb)
```

```python
import functools
import math

import jax
import jax.numpy as jnp
from jax import lax
from jax.experimental import pallas as pl
from jax.experimental.pallas import tpu as pltpu

F32 = jnp.float32
BF16 = jnp.bfloat16

D_MODEL = 1024
GLA_HEADS = 4
GLA_DK = 128
GLA_DV = 128
GLA_WIDTH = GLA_HEADS * GLA_DK
GLA_GATE_RANK = 16
GLA_GATE_NORMALIZER = 16.0
GLA_CHUNK = 64
GLA_HEADS_PER_STEP = 4
DIFF_HEADS = 4
DIFF_HEAD_DIM = 64
DIFF_V_DIM = 128
DIFF_WIDTH = DIFF_HEADS * 2 * DIFF_HEAD_DIM
ROPE_HALF = DIFF_HEAD_DIM // 2
ROPE_BASE = 10000.0
GRID_W = 64
N_EXPERTS = 16
N_GROUPS = 4
EXPERTS_PER_GROUP = 4
D_EXPERT = 512
NORM_EPS = 1e-6
LOG2E = 1.4426950408889634

PAIR_A = (0, 0, 0, 1, 1, 3)
PAIR_B = (1, 2, 3, 3, 2, 2)
N_PAIRS = len(PAIR_A)
N_COMBOS = N_GROUPS * N_PAIRS

GATE_COLS = 128
MAIN_COLS = 7 * 512
TOKEN_TILE = 256
POST_TILE = 1024
PRE_TILE = 512
MOE_TILE = 256
DMA_UNROLL = 8
ATTN_TQ = 1024
ATTN_TK = 256
ATTN_GROUP = 128
ATTN_UNROLL = 16
ATTN_LOOKAHEAD = 8
ONES_ROWS = 16
VMEM_LIMIT = 56 * 1024 * 1024


def _cparams(sem):
    return pltpu.CompilerParams(dimension_semantics=sem, vmem_limit_bytes=VMEM_LIMIT)


def _dot(a, b):
    return jnp.dot(a, b, preferred_element_type=F32)


def _dot_nt(a, b):
    return lax.dot_general(a, b, (((1,), (1,)), ((), ())), preferred_element_type=F32)


def _dot_tn(a, b):
    return lax.dot_general(a, b, (((0,), (0,)), ((), ())), preferred_element_type=F32)


def _sigmoid(x):
    return 1.0 / (1.0 + jnp.exp(-x))


def _split_bf16(x):
    hi = x.astype(BF16)
    lo = (x - hi.astype(F32)).astype(BF16)
    return hi, lo


def _mod_kernel(c_ref, w_ref, b_ref, o_ref):
    c = c_ref[...]
    s = c * _sigmoid(c)
    o_ref[0] = jnp.dot(s, w_ref[0], preferred_element_type=F32,
                       precision=lax.Precision.HIGHEST) + b_ref[0]


def _modulation(cs, w_mod, b_mod):
    depth, d, n = w_mod.shape
    rows = cs.shape[0]
    tn = 1536
    return pl.pallas_call(
        _mod_kernel,
        grid=(depth, n // tn),
        in_specs=[pl.BlockSpec((rows, d), lambda l, j: (0, 0)),
                  pl.BlockSpec((1, d, tn), lambda l, j: (l, 0, j)),
                  pl.BlockSpec((1, 1, tn), lambda l, j: (l, 0, j))],
        out_specs=pl.BlockSpec((1, rows, tn), lambda l, j: (l, 0, j)),
        out_shape=jax.ShapeDtypeStruct((depth, rows, n), F32),
        compiler_params=_cparams(("arbitrary", "arbitrary")),
        name="modulation",
    )(cs, w_mod, b_mod.reshape(depth, 1, n))


def _rope(y, cos, sin):
    w = y.shape[1]
    lane = lax.broadcasted_iota(jnp.int32, y.shape, 1)
    first = (lane % (2 * (ROPE_HALF // 2))) < (ROPE_HALF // 2)
    partner = jnp.where(first, pltpu.roll(y, w - ROPE_HALF // 2, axis=1), pltpu.roll(y, ROPE_HALF // 2, axis=1))
    return y * cos + partner * sin


def _pre_kernel(has_moe, is_sample, *refs):
    refs = list(refs)
    x_ref = refs.pop(0)
    if has_moe:
        y_ref, modp_ref = refs[:2]
        refs = refs[2:]
    mod_ref, g1_ref, win_ref, wgate_ref, bgate_ref, tri_ref, gmat_ref, gq_ref, gk_ref = refs[:9]
    refs = refs[9:]
    if is_sample:
        cos_ref, sin_ref = refs[:2]
        refs = refs[2:]
    xo_ref = refs.pop(0) if has_moe else None
    gla_ref, la_ref, qt_ref, kd_ref, vt_ref = refs[:5]
    refs = refs[5:]
    if not is_sample:
        kc_ref, vc_ref = refs

    d = D_MODEL
    sub = TOKEN_TILE
    m = mod_ref[0]
    gs = g1_ref[0] * (1.0 + m[:, d:2 * d])
    scale = DIFF_HEAD_DIM ** -0.5 * LOG2E
    w = GLA_WIDTH
    subs = [slice(j * sub, (j + 1) * sub) for j in range(x_ref.shape[0] // sub)]
    hbs = []
    for rs in subs:
        x = x_ref[rs, :]
        if has_moe:
            x = x + modp_ref[0][:, 5 * d:6 * d] * y_ref[rs, :].astype(F32)
            xo_ref[rs, :] = x
        ms = jnp.mean(x * x, axis=-1, keepdims=True)
        hbs.append((x * lax.rsqrt(ms + NORM_EPS) * gs + m[:, 0:d]).astype(BF16))

    def put_t(ref, j, rs, val):
        if is_sample:
            ref[0, :, rs] = val.astype(BF16).T
        else:
            ref[j] = val.astype(BF16).T

    hb_all = jnp.concatenate(hbs, axis=0)

    def plain(c):
        p = _dot(hb_all, win_ref[0, :, c * 512:(c + 1) * 512])
        if c == 0:
            p = p * (GLA_DK ** -0.5)
        gla_ref[:, c * 512:(c + 1) * 512] = p.astype(BF16)

    q1 = [_dot(hb, win_ref[0, :, 2048:2560]) for hb in hbs]
    k1 = [_dot(hb, win_ref[0, :, 2560:3072]) for hb in hbs]
    a1 = [_dot(hb, win_ref[0, :, MAIN_COLS:MAIN_COLS + GATE_COLS]) for hb in hbs]
    plain(0)
    msq = [_dot((p * p).astype(BF16), gmat_ref[...]) for p in q1]
    msk = [_dot((p * p).astype(BF16), gmat_ref[...]) for p in k1]
    zs = [_dot(a.astype(BF16), wgate_ref[0]) + bgate_ref[0] for a in a1]
    plain(1)
    for rs, z in zip(subs, zs):
        la = (jnp.minimum(z, 0.0) - jnp.log1p(jnp.exp(-jnp.abs(z)))) * (1.0 / GLA_GATE_NORMALIZER)
        la_hi, la_lo = _split_bf16(la)
        la_ref[rs, 0:w] = _dot(tri_ref[0], la_hi[:, 0:w]) + _dot(tri_ref[0], la_lo[:, 0:w])
        la_ref[rs, w:2 * w] = _dot(tri_ref[1], la_hi[:, w:2 * w]) + _dot(tri_ref[1], la_lo[:, w:2 * w])
    for j, rs in enumerate(subs):
        q = q1[j] * lax.rsqrt(msq[j] + NORM_EPS) * gq_ref[0]
        k = k1[j] * lax.rsqrt(msk[j] + NORM_EPS) * gk_ref[0]
        if is_sample:
            q = _rope(q, cos_ref[rs, :], sin_ref[rs, :])
            k = _rope(k, cos_ref[rs, :], sin_ref[rs, :])
        else:
            kc_ref[j, 0] = k
        put_t(qt_ref, j, rs, q * scale)
        kd_ref[rs, :] = k.astype(BF16)
    plain(2)
    for j, (rs, hb) in enumerate(zip(subs, hbs)):
        v = _dot(hb, win_ref[0, :, 3072:3584])
        if not is_sample:
            vc_ref[j, 0] = v
        put_t(vt_ref, j, rs, v)
    plain(3)


def _pre_mixer(l, x, y, mod_prev, mod, prep, rope, nb, seq, is_sample):
    t, d = x.shape
    sub = TOKEN_TILE
    tt = PRE_TILE
    has_moe = y is not None
    if is_sample:
        assert seq % tt == 0
        tiles_per_batch = seq // tt
        mod_map = lambda i: (1 + i // tiles_per_batch, 0, 0)
    else:
        assert seq == sub and t % tt == 0
        mod_map = lambda i: (0, 0, 0)
    row = lambda i: (i, 0)
    const2 = lambda i: (0, 0)
    lay3 = lambda i: (l, 0, 0)

    ins = [x]
    in_specs = [pl.BlockSpec((tt, d), row)]
    if has_moe:
        ins += [y, mod_prev]
        in_specs += [pl.BlockSpec((tt, d), row), pl.BlockSpec((1, 1, 6 * d), mod_map)]
    ins += [mod, prep["g1"], prep["w_in"], prep["w_gate"], prep["b_gate"], prep["tri"], prep["gmat"], prep["gq"],
            prep["gk"]]
    in_specs += [pl.BlockSpec((1, 1, 6 * d), mod_map),
                 pl.BlockSpec((1, 1, d), lay3),
                 pl.BlockSpec((1, d, MAIN_COLS + GATE_COLS), lay3, pipeline_mode=pl.Buffered(1)),
                 pl.BlockSpec((1, GATE_COLS, 2 * GLA_WIDTH), lay3),
                 pl.BlockSpec((1, 1, 2 * GLA_WIDTH), lay3),
                 pl.BlockSpec((2, sub, sub), lambda i: (0, 0, 0)),
                 pl.BlockSpec((DIFF_WIDTH, DIFF_WIDTH), const2),
                 pl.BlockSpec((1, 1, DIFF_WIDTH), lay3),
                 pl.BlockSpec((1, 1, DIFF_WIDTH), lay3)]
    if is_sample:
        ins += [rope[0], rope[1]]
        in_specs += [pl.BlockSpec((tt, DIFF_WIDTH), lambda i: (i % tiles_per_batch, 0))] * 2

    out_shape = []
    out_specs = []
    if has_moe:
        out_shape.append(jax.ShapeDtypeStruct((t, d), F32))
        out_specs.append(pl.BlockSpec((tt, d), row))
    if is_sample:
        tr_spec = pl.BlockSpec((1, DIFF_WIDTH, tt), lambda i: (i // tiles_per_batch, 0, i % tiles_per_batch))
    else:
        tr_spec = pl.BlockSpec((tt // sub, DIFF_WIDTH, seq), lambda i: (i, 0, 0))
    out_shape += [jax.ShapeDtypeStruct((t, 4 * GLA_WIDTH), BF16),
                  jax.ShapeDtypeStruct((t, 2 * GLA_WIDTH), F32),
                  jax.ShapeDtypeStruct((nb, DIFF_WIDTH, seq), BF16),
                  jax.ShapeDtypeStruct((t, DIFF_WIDTH), BF16),
                  jax.ShapeDtypeStruct((nb, DIFF_WIDTH, seq), BF16)]
    out_specs += [pl.BlockSpec((tt, 4 * GLA_WIDTH), row),
                  pl.BlockSpec((tt, 2 * GLA_WIDTH), row),
                  tr_spec,
                  pl.BlockSpec((tt, DIFF_WIDTH), row),
                  tr_spec]
    if not is_sample:
        out_shape += [jax.ShapeDtypeStruct((nb, 1, seq, DIFF_WIDTH), F32)] * 2
        out_specs += [pl.BlockSpec((tt // sub, 1, seq, DIFF_WIDTH), lambda i: (i, 0, 0, 0))] * 2

    outs = pl.pallas_call(
        functools.partial(_pre_kernel, has_moe, is_sample),
        grid=(t // tt,),
        in_specs=in_specs,
        out_specs=out_specs,
        out_shape=out_shape,
        compiler_params=_cparams(("arbitrary",)),
        name="pre_mixer",
    )(*ins)
    outs = list(outs)
    x_new = outs.pop(0) if has_moe else x
    return [x_new] + outs


def _gla_kernel(has_state, n_chunks, hg, *refs):
    refs = list(refs)
    q_ref, k_ref, v_ref, g_ref, bf_ref, bb_ref, gg_ref = refs[:7]
    refs = refs[7:]
    if has_state:
        s0f_ref, s0b_ref = refs[:2]
        refs = refs[2:]
    o_ref, sf_ref, sb_ref, st_ref, acc_ref = refs

    c = GLA_CHUNK
    dk = GLA_DK
    for hd in range(hg):
        if has_state:
            st_ref[2 * hd] = s0f_ref[0, 0, hd].T
            st_ref[2 * hd + 1] = s0b_ref[0, 0, hd].T
        else:
            st_ref[2 * hd] = jnp.zeros((GLA_DV, dk), F32)
            st_ref[2 * hd + 1] = jnp.zeros((GLA_DV, dk), F32)

    r = lax.broadcasted_iota(jnp.int32, (c, c), 0)
    s = lax.broadcasted_iota(jnp.int32, (c, c), 1)
    lower = r >= s
    upper = r <= s

    def scores(rows, hd, b_ref, st_i, mid_row, last_row):
        cs = slice(hd * dk, (hd + 1) * dk)
        q = q_ref[rows, cs].astype(F32)
        k = k_ref[rows, cs].astype(F32)
        v = v_ref[rows, cs]
        b = b_ref[rows, cs]
        mid = b[mid_row:mid_row + 1]
        last = b[last_row:last_row + 1]
        qe = q * jnp.exp(b - mid)
        ke = k * jnp.exp(mid - b)
        qi = (qe * jnp.exp(mid)).astype(BF16)
        ks = (ke * jnp.exp(last - mid)).astype(BF16)
        qe = qe.astype(BF16)
        ke = ke.astype(BF16)
        st = st_ref[st_i]
        return _dot_nt(qe, ke), _dot_nt(qi, st.astype(BF16)), _dot_tn(v, ks), v, st, last

    def outputs(sc, mask, st_i):
        att, o_inter, kv, v, st, last = sc
        st_ref[st_i] = st * jnp.exp(last) + kv
        return _dot(jnp.where(mask, att, 0.0).astype(BF16), v) + o_inter

    def finish(o, rows, hd):
        cs = slice(hd * dk, (hd + 1) * dk)
        ms = jnp.mean(o * o, axis=-1, keepdims=True)
        g = g_ref[rows, cs].astype(F32)
        o_ref[rows, cs] = (o * lax.rsqrt(ms + NORM_EPS) * gg_ref[0] * (g * _sigmoid(g))).astype(BF16)

    def step(n, second_visit):
        rf = pl.ds(pl.multiple_of(n * c, c), c)
        rb = pl.ds(pl.multiple_of((n_chunks - 1 - n) * c, c), c)
        sc = []
        for hd in range(hg):
            sc.append(scores(rf, hd, bf_ref, 2 * hd, c // 2, c - 1))
            sc.append(scores(rb, hd, bb_ref, 2 * hd + 1, c - 1 - c // 2, 0))
        for hd in range(hg):
            cs = slice(hd * dk, (hd + 1) * dk)
            for rows, sci, mask, st_i in ((rf, sc[2 * hd], lower, 2 * hd), (rb, sc[2 * hd + 1], upper, 2 * hd + 1)):
                o = outputs(sci, mask, st_i)
                if second_visit:
                    finish(acc_ref[rows, cs] + o, rows, hd)
                else:
                    acc_ref[rows, cs] = o

    half = n_chunks // 2
    per_trip = max(u for u in (1, 2, 4) if half % u == 0)

    def trips(first, second_visit):
        def body(n, carry):
            for u in range(per_trip):
                step(first + n * per_trip + u, second_visit)
            return carry
        lax.fori_loop(0, half // per_trip, body, 0)

    trips(0, False)
    trips(half, True)

    for hd in range(hg):
        sf_ref[0, hd] = st_ref[2 * hd].T
        sb_ref[0, hd] = st_ref[2 * hd + 1].T


def _gla(l, gla_in, bsum, gg, s0f, s0b, nb, seq):
    t = gla_in.shape[0]
    h = GLA_HEADS
    hg = GLA_HEADS_PER_STEP
    nhb = h // hg
    w = hg * GLA_DK
    has_state = s0f is not None
    n_chunks = seq // GLA_CHUNK
    assert n_chunks % 2 == 0
    col = lambda off: (lambda b, hh: (b, off + hh))
    in_bytes = seq * w * (4 * 2 + 2 * 4)
    mode = pl.Buffered(1) if 2 * in_bytes > VMEM_LIMIT // 2 else None
    big = lambda off: pl.BlockSpec((seq, w), col(off), pipeline_mode=mode)
    ins = [gla_in, gla_in, gla_in, gla_in, bsum, bsum, gg]
    in_specs = [big(0), big(nhb), big(2 * nhb), big(3 * nhb), big(0), big(nhb),
                pl.BlockSpec((1, 1, GLA_DV), lambda b, hh: (l, 0, 0))]
    if has_state:
        ins += [s0f, s0b]
        in_specs += [pl.BlockSpec((1, 1, hg, GLA_DK, GLA_DV), lambda b, hh: (b, l, hh, 0, 0))] * 2
    return pl.pallas_call(
        functools.partial(_gla_kernel, has_state, n_chunks, hg),
        grid=(nb, nhb),
        in_specs=in_specs,
        out_specs=[pl.BlockSpec((seq, w), col(0)),
                   pl.BlockSpec((1, hg, GLA_DK, GLA_DV), lambda b, hh: (b, hh, 0, 0)),
                   pl.BlockSpec((1, hg, GLA_DK, GLA_DV), lambda b, hh: (b, hh, 0, 0))],
        out_shape=[jax.ShapeDtypeStruct((t, GLA_WIDTH), BF16),
                   jax.ShapeDtypeStruct((nb, h, GLA_DK, GLA_DV), F32),
                   jax.ShapeDtypeStruct((nb, h, GLA_DK, GLA_DV), F32)],
        scratch_shapes=[pltpu.VMEM((2 * hg, GLA_DV, GLA_DK), F32), pltpu.VMEM((seq, w), F32)],
        compiler_params=_cparams(("arbitrary", "arbitrary")),
        name="gla",
    )(*ins)


def _attn_kernel(lam_init, has_ctx, n_kt, tk, *refs):
    refs = list(refs)
    qt_ref, k_ref, vt_ref = refs[:3]
    refs = refs[3:]
    if has_ctx:
        ck_ref, cv_ref = refs[:2]
        refs = refs[2:]
    lq1_ref, lk1_ref, lq2_ref, lk2_ref, gd_ref, o_ref = refs

    qt = qt_ref[0]
    tq = qt.shape[1]
    hw = 2 * DIFF_HEAD_DIM
    heads = qt.shape[0] // hw
    gq = ATTN_GROUP
    n_groups = heads * (tq // gq)
    head_of = lambda e: e // (tq // gq)
    cols_of = lambda e: slice((e % (tq // gq)) * gq, (e % (tq // gq) + 1) * gq)
    dim = lax.broadcasted_iota(jnp.int32, (hw, gq), 0)
    zero = jnp.zeros((hw, gq), BF16)
    qws = []
    for e in range(n_groups):
        qg = qt[head_of(e) * hw:(head_of(e) + 1) * hw, cols_of(e)]
        qws.append(jnp.concatenate([jnp.where(dim < DIFF_HEAD_DIM, qg, zero),
                                    jnp.where(dim >= DIFF_HEAD_DIM, qg, zero)], axis=1))

    def absorb(group_state, st, vt):
        m, acc = group_state
        m_new = jnp.maximum(m, jnp.max(st, axis=0, keepdims=True))
        alpha = jnp.exp2(m - m_new)
        p = jnp.exp2(st - m_new)
        return m_new, alpha * acc + _dot(vt, p.astype(BF16))

    def process(state, tiles):
        units = [(u, e) for u in range(len(tiles)) for e in range(n_groups)]

        def score(idx):
            u, e = units[idx]
            return _dot(tiles[u][0][head_of(e)], qws[e])

        state = list(state)
        pending = [score(idx) for idx in range(min(ATTN_LOOKAHEAD, len(units)))]
        for idx, (u, e) in enumerate(units):
            if idx + ATTN_LOOKAHEAD < len(units):
                pending.append(score(idx + ATTN_LOOKAHEAD))
            state[e] = absorb(state[e], pending.pop(0), tiles[u][1][head_of(e)])
        return tuple(state)

    def with_ones(vt):
        return jnp.concatenate([vt, jnp.ones((ONES_ROWS, vt.shape[1]), BF16)], axis=0)

    state = tuple((jnp.full((1, 2 * gq), -jnp.inf, F32), jnp.zeros((DIFF_V_DIM + ONES_ROWS, 2 * gq), F32))
                  for _ in range(n_groups))
    ctx_tiles = []
    if has_ctx:
        assert heads == 1
        past = ck_ref.shape[2]
        ctk = tk if past % tk == 0 else past
        cvt = cv_ref[0, 0].T.astype(BF16)
        ctx_tiles = [([ck_ref[0, 0, j * ctk:(j + 1) * ctk, :].astype(BF16)],
                      [with_ones(cvt[:, j * ctk:(j + 1) * ctk])]) for j in range(past // ctk)]

    def latent_tile(rows):
        return ([k_ref[rows, hd * hw:(hd + 1) * hw] for hd in range(heads)],
                [with_ones(vt_ref[0, hd * DIFF_V_DIM:(hd + 1) * DIFF_V_DIM, rows]) for hd in range(heads)])

    unroll = max(u for u in range(1, ATTN_UNROLL + 1) if n_kt % u == 0)
    if unroll == n_kt:
        latent = [latent_tile(slice(u * tk, (u + 1) * tk)) for u in range(n_kt)]
        state = process(state, ctx_tiles + latent)
    else:
        if ctx_tiles:
            state = process(state, ctx_tiles)

        def body(j, st):
            return process(st, [latent_tile(pl.ds(pl.multiple_of((j * unroll + u) * tk, tk), tk))
                                for u in range(unroll)])

        state = lax.fori_loop(0, n_kt // unroll, body, state)

    lam = (jnp.exp(jnp.sum(lq1_ref[0] * lk1_ref[0], axis=-1, keepdims=True))
           - jnp.exp(jnp.sum(lq2_ref[0] * lk2_ref[0], axis=-1, keepdims=True)) + lam_init)
    for e in range(n_groups):
        m, acc = state[e]
        on = acc[:DIFF_V_DIM] / acc[DIFF_V_DIM:DIFF_V_DIM + 1]
        o = on[:, :gq] - lam * on[:, gq:]
        ms = jnp.mean(o * o, axis=0, keepdims=True)
        o = o * lax.rsqrt(ms + NORM_EPS) * gd_ref[0] * (1.0 - lam_init)
        o_ref[cols_of(e), head_of(e) * DIFF_V_DIM:(head_of(e) + 1) * DIFF_V_DIM] = o.T.astype(BF16)


def _diff_attention(l, lam_init, qt, kd, vt, ctx_k, ctx_v, lams, gd, nb, seq):
    t = kd.shape[0]
    h = DIFF_HEADS
    has_ctx = ctx_k is not None
    tq = min(ATTN_TQ, seq)
    tk = min(ATTN_TK, seq)
    nq = seq // tq
    hps = h if (not has_ctx and seq * h <= ATTN_TQ) else 1
    w = hps * 2 * DIFF_HEAD_DIM
    wv = hps * DIFF_V_DIM
    ins = [qt, kd, vt]
    in_specs = [pl.BlockSpec((1, w, tq), lambda b, hh, i: (b, hh, i)),
                pl.BlockSpec((seq, w), lambda b, hh, i: (b, hh)),
                pl.BlockSpec((1, wv, seq), lambda b, hh, i: (b, hh, 0))]
    if has_ctx:
        past = ctx_k.shape[2]
        ins += [ctx_k, ctx_v]
        in_specs += [pl.BlockSpec((1, 1, past, w), lambda b, hh, i: (b, l, 0, hh)),
                     pl.BlockSpec((1, 1, past, wv), lambda b, hh, i: (b, l, 0, hh))]
    lay3 = lambda b, hh, i: (l, 0, 0)
    ins += list(lams) + [gd]
    in_specs += [pl.BlockSpec((1, 1, DIFF_HEAD_DIM), lay3)] * 4 + [pl.BlockSpec((1, DIFF_V_DIM, 1), lay3)]
    return pl.pallas_call(
        functools.partial(_attn_kernel, lam_init, has_ctx, seq // tk, tk),
        grid=(nb, h // hps, nq),
        in_specs=in_specs,
        out_specs=pl.BlockSpec((tq, wv), lambda b, hh, i: (b * nq + i, hh)),
        out_shape=jax.ShapeDtypeStruct((t, DIFF_WIDTH), BF16),
        compiler_params=_cparams(("arbitrary", "arbitrary", "arbitrary")),
        name="diff_attention",
    )(*ins)


def _top2_of4(v):
    m1 = jnp.maximum(jnp.maximum(v[0], v[1]), jnp.maximum(v[2], v[3]))
    i1 = jnp.where(v[0] == m1, 0, jnp.where(v[1] == m1, 1, jnp.where(v[2] == m1, 2, 3)))
    neg = jnp.full_like(m1, -jnp.inf)
    w = [jnp.where(i1 == j, neg, v[j]) for j in range(4)]
    m2 = jnp.maximum(jnp.maximum(w[0], w[1]), jnp.maximum(w[2], w[3]))
    i2 = jnp.where(w[0] == m2, 0, jnp.where(w[1] == m2, 1, jnp.where(w[2] == m2, 2, 3)))
    return m1, i1, m2, i2


def _post_kernel(og_ref, od_ref, x_ref, mod_ref, g2_ref, wo_ref, wr_ref, br_ref, x1_ref, hx_ref, route_ref):
    d = D_MODEL
    sub = TOKEN_TILE
    n_sub = x_ref.shape[0] // sub
    m = mod_ref[0]
    rows = [slice(j * sub, (j + 1) * sub) for j in range(n_sub)]
    outs = [_dot(og_ref[rs, :], wo_ref[0, 0:GLA_WIDTH, :]) + _dot(od_ref[rs, :], wo_ref[0, GLA_WIDTH:, :])
            for rs in rows]
    w_hi, w_lo = _split_bf16(wr_ref[...])
    w_hl = jnp.concatenate([w_hi, w_lo], axis=0)
    zs = []
    for rs, out in zip(rows, outs):
        x1 = x_ref[rs, :] + m[:, 2 * d:3 * d] * out
        x1_ref[rs, :] = x1
        ms = jnp.mean(x1 * x1, axis=-1, keepdims=True)
        h2 = x1 * lax.rsqrt(ms + NORM_EPS) * (g2_ref[0] * (1.0 + m[:, 4 * d:5 * d])) + m[:, 3 * d:4 * d]
        h_hi, h_lo = _split_bf16(h2)
        hx_ref[rs, 0:d] = h2
        za = _dot_nt(w_hl, h_hi)
        zs.append(za[0:N_EXPERTS] + za[N_EXPERTS:2 * N_EXPERTS] + _dot_nt(w_hi, h_lo))
    for rs, z in zip(rows, zs):
        _route(z, br_ref, hx_ref, route_ref, rs)


def _route(z, br_ref, hx_ref, route_ref, rs):
    d = D_MODEL
    s = _sigmoid(z)
    sel = s + br_ref[...]

    e = EXPERTS_PER_GROUP
    tops = []
    for g in range(N_GROUPS):
        tops.append(_top2_of4([sel[g * e + j:g * e + j + 1, :] for j in range(e)]))
    score = [t[0] + t[2] for t in tops]
    best = jnp.maximum(jnp.maximum(score[0], score[1]), jnp.maximum(score[2], score[3]))
    gi = jnp.where(score[0] == best, 0, jnp.where(score[1] == best, 1, jnp.where(score[2] == best, 2, 3)))

    def pick(rows):
        return jnp.where(gi == 0, rows[0], jnp.where(gi == 1, rows[1], jnp.where(gi == 2, rows[2], rows[3])))

    i1 = pick([t[1] for t in tops])
    i2 = pick([t[3] for t in tops])

    def gate_of(idx):
        per_group = []
        for g in range(N_GROUPS):
            rows = [s[g * e + j:g * e + j + 1, :] for j in range(e)]
            per_group.append(jnp.where(idx == 0, rows[0], jnp.where(idx == 1, rows[1],
                                                                     jnp.where(idx == 2, rows[2], rows[3]))))
        return pick(per_group)

    s1 = gate_of(i1)
    s2 = gate_of(i2)
    tot = s1 + s2
    w1 = s1 / tot
    w2 = s2 / tot
    lo = jnp.minimum(i1, i2)
    hi = jnp.maximum(i1, i2)
    w_lo = jnp.where(i1 < i2, w1, w2)
    w_hi = jnp.where(i1 < i2, w2, w1)
    pair = jnp.where(lo == 0, hi - 1, jnp.where(lo == 1, jnp.where(hi == 3, 3, 4), 5))
    swapped = pair == 5
    gate_a = jnp.where(swapped, w_hi, w_lo)
    gate_b = jnp.where(swapped, w_lo, w_hi)
    route_ref[:, rs] = jnp.broadcast_to(gi * N_PAIRS + pair, (route_ref.shape[0], gate_a.shape[1]))
    r = lax.broadcasted_iota(jnp.int32, (GATE_COLS, gate_a.shape[1]), 0)
    gates_t = jnp.where(r == 0, gate_a, jnp.where(r == 1, gate_b, 0.0))
    hx_ref[rs, d:d + GATE_COLS] = gates_t.T


def _post_mixer(l, og, od, x, mod, prep, seq, is_sample):
    t, d = x.shape
    tt = POST_TILE if (seq % POST_TILE == 0 or not is_sample) and t % POST_TILE == 0 else TOKEN_TILE
    tiles_per_batch = max(seq // tt, 1)
    if is_sample:
        mod_map = lambda i: (1 + i // tiles_per_batch, 0, 0)
    else:
        mod_map = lambda i: (0, 0, 0)
    row = lambda i: (i, 0)
    lay3 = lambda i: (l, 0, 0)
    return pl.pallas_call(
        _post_kernel,
        grid=(t // tt,),
        in_specs=[pl.BlockSpec((tt, GLA_WIDTH), row),
                  pl.BlockSpec((tt, DIFF_WIDTH), row),
                  pl.BlockSpec((tt, d), row),
                  pl.BlockSpec((1, 1, 6 * d), mod_map),
                  pl.BlockSpec((1, 1, d), lay3),
                  pl.BlockSpec((1, d, d), lay3),
                  pl.BlockSpec((N_EXPERTS, d), lambda i: (0, 0)),
                  pl.BlockSpec((N_EXPERTS, 1), lambda i: (0, 0))],
        out_specs=[pl.BlockSpec((tt, d), row),
                   pl.BlockSpec((tt, d + GATE_COLS), row),
                   pl.BlockSpec((8, tt), lambda i: (0, i))],
        out_shape=[jax.ShapeDtypeStruct((t, d), F32),
                   jax.ShapeDtypeStruct((t, d + GATE_COLS), F32),
                   jax.ShapeDtypeStruct((8, t), jnp.int32)],
        compiler_params=_cparams(("arbitrary",)),
        name="post_mixer",
    )(og, od, x, mod, prep["g2"], prep["w_out"], prep["w_router_t"], prep["b_router"])


def _moe_kernel(n_tiles, order_ref, first_ref, count_ref, ea_ref, eb_ref,
                hx_hbm, wga_ref, wua_ref, wda_ref, wgb_ref, wub_ref, wdb_ref,
                y_hbm, xbuf, ybuf, in_sem, out_sem):
    i = pl.program_id(0)
    slot = i % 2
    d = D_MODEL
    tm = xbuf.shape[1]
    n_tokens = hx_hbm.shape[0]

    def gather_copy(buf_slot, r, tok):
        return pltpu.make_async_copy(hx_hbm.at[pl.ds(tok, 1), :], xbuf.at[buf_slot, pl.ds(r, 1), :],
                                     in_sem.at[buf_slot])

    def scatter_copy(buf_slot, r, row):
        return pltpu.make_async_copy(ybuf.at[buf_slot, pl.ds(r, 1), :], y_hbm.at[pl.ds(row, 1), :],
                                     out_sem.at[buf_slot])

    def for_rows(fn):
        def body(q, c):
            for s in range(DMA_UNROLL):
                fn(q * DMA_UNROLL + s)
            return c
        lax.fori_loop(0, tm // DMA_UNROLL, body, 0)

    def wait_gather(buf_slot):
        for_rows(lambda r: gather_copy(buf_slot, r, 0).wait())

    def wait_scatter(buf_slot):
        for_rows(lambda r: scatter_copy(buf_slot, r, 0).wait())

    def scatter_row(buf_slot, base, n, r):
        return jnp.where(r < n, order_ref[base + r], n_tokens + buf_slot * tm + r)

    prev = jnp.maximum(i - 1, 0)

    @pl.when(i == 0)
    def _():
        ybuf[...] = jnp.zeros_like(ybuf)
        for s in range(2):
            spare_init = pltpu.make_async_copy(ybuf.at[s], y_hbm.at[pl.ds(n_tokens + s * tm, tm), :], out_sem.at[s])
            spare_init.start()
            spare_init.wait()
        base0 = first_ref[0]
        for_rows(lambda r: gather_copy(0, r, order_ref[base0 + r]).start())

    @pl.when((i == 0) | (count_ref[prev] > 0))
    def _():
        wait_gather(slot)

    @pl.when((i == 1) | ((i >= 2) & (count_ref[jnp.maximum(i - 2, 0)] > 0)))
    def _():
        wait_scatter(slot)

    @pl.when(count_ref[i] > 0)
    def _():
        xg = xbuf[slot]
        x = xg[:, 0:d].astype(BF16)
        base_next = first_ref[i + 1]
        base_prev = first_ref[prev]
        n_prev = jnp.where(i > 0, count_ref[prev], 0)
        per_dot = -(-tm // 6)

        def issue(part):
            for r in range(part * per_dot, min((part + 1) * per_dot, tm)):
                gather_copy(1 - slot, r, order_ref[base_next + r]).start()
                scatter_copy(1 - slot, r, scatter_row(1 - slot, base_prev, n_prev, r)).start()

        hg_a = _dot(x, wga_ref[0, 0])
        issue(0)
        hu_a = _dot(x, wua_ref[0, 0])
        issue(1)
        hg_b = _dot(x, wgb_ref[0, 0])
        issue(2)
        hu_b = _dot(x, wub_ref[0, 0])
        issue(3)
        out_a = _dot(((hg_a * _sigmoid(hg_a)) * hu_a).astype(BF16), wda_ref[0, 0])
        issue(4)
        out_b = _dot(((hg_b * _sigmoid(hg_b)) * hu_b).astype(BF16), wdb_ref[0, 0])
        issue(5)
        ybuf[slot] = xg[:, d:d + 1] * out_a + xg[:, d + 1:d + 2] * out_b

    @pl.when((count_ref[i] == 0) & (i > 0) & (count_ref[prev] > 0))
    def _():
        base_prev = first_ref[prev]
        n_prev = count_ref[prev]
        for_rows(lambda r: scatter_copy(1 - slot, r, scatter_row(1 - slot, base_prev, n_prev, r)).start())

    @pl.when((i == n_tiles - 1) & (count_ref[prev] > 0))
    def _():
        wait_scatter(1 - slot)


def _moe(l, hx, route, prep):
    t = hx.shape[0]
    d = D_MODEL
    tm = MOE_TILE
    n_tiles = t // tm + N_COMBOS

    combo = route[0]
    order = jnp.argsort(combo).astype(jnp.int32)
    order_padded = jnp.concatenate([order, jnp.zeros((tm,), jnp.int32)])
    counts = jnp.sum(combo[:, None] == jnp.arange(N_COMBOS, dtype=jnp.int32)[None, :], axis=0).astype(jnp.int32)
    tiles_of = (counts + tm - 1) // tm
    tile_end = jnp.cumsum(tiles_of)
    start = jnp.cumsum(counts) - counts
    tile = jnp.arange(n_tiles, dtype=jnp.int32)
    n_used = tile_end[-1]
    tile_combo = jnp.sum(jnp.minimum(tile, n_used - 1)[:, None] >= tile_end[None, :], axis=1).astype(jnp.int32)
    onehot = (tile_combo[:, None] == jnp.arange(N_COMBOS, dtype=jnp.int32)[None, :]).astype(jnp.int32)
    pick = lambda table: jnp.sum(onehot * table[None, :], axis=1)
    within = (tile - (pick(tile_end) - pick(tiles_of))) * tm
    first = jnp.clip(pick(start) + within, 0, t - 1).astype(jnp.int32)
    count = jnp.where(tile < n_used, jnp.clip(pick(counts) - within, 0, tm), 0).astype(jnp.int32)
    grp = tile_combo // N_PAIRS
    pr = tile_combo % N_PAIRS
    pair_a = jnp.sum((pr[:, None] == jnp.arange(N_PAIRS)[None, :]) * jnp.asarray(PAIR_A, jnp.int32)[None, :], axis=1)
    pair_b = jnp.sum((pr[:, None] == jnp.arange(N_PAIRS)[None, :]) * jnp.asarray(PAIR_B, jnp.int32)[None, :], axis=1)
    ea = (grp * EXPERTS_PER_GROUP + pair_a).astype(jnp.int32)
    eb = (grp * EXPERTS_PER_GROUP + pair_b).astype(jnp.int32)

    wa = lambda i, o_r, f_r, c_r, ea_r, eb_r: (l, ea_r[i], 0, 0)
    wb = lambda i, o_r, f_r, c_r, ea_r, eb_r: (l, eb_r[i], 0, 0)
    up = pl.BlockSpec((1, 1, d, D_EXPERT), wa)
    dn = pl.BlockSpec((1, 1, D_EXPERT, d), wa)
    upb = pl.BlockSpec((1, 1, d, D_EXPERT), wb)
    dnb = pl.BlockSpec((1, 1, D_EXPERT, d), wb)
    any_spec = pl.BlockSpec(memory_space=pl.ANY)
    return pl.pallas_call(
        functools.partial(_moe_kernel, n_tiles),
        grid_spec=pltpu.PrefetchScalarGridSpec(
            num_scalar_prefetch=5,
            grid=(n_tiles,),
            in_specs=[any_spec, up, up, dn, upb, upb, dnb],
            out_specs=any_spec,
            scratch_shapes=[pltpu.VMEM((2, tm, d + GATE_COLS), F32), pltpu.VMEM((2, tm, d), F32),
                            pltpu.SemaphoreType.DMA((2,)), pltpu.SemaphoreType.DMA((2,))]),
        out_shape=jax.ShapeDtypeStruct((t + 2 * tm, d), F32),
        compiler_params=_cparams(("arbitrary",)),
        name="moe",
    )(order_padded, first, count, ea, eb, hx, prep["w_eg"], prep["w_eu"], prep["w_ed"],
      prep["w_eg"], prep["w_eu"], prep["w_ed"])


def _resid_kernel(x_ref, y_ref, mod_ref, o_ref):
    d = D_MODEL
    o_ref[...] = x_ref[...] + mod_ref[0][:, 5 * d:6 * d] * y_ref[...].astype(F32)


def _residual(x, y, mod, seq, is_sample):
    t, d = x.shape
    tt = TOKEN_TILE
    tiles_per_batch = seq // tt
    if is_sample:
        mod_map = lambda i: (1 + i // tiles_per_batch, 0, 0)
    else:
        mod_map = lambda i: (0, 0, 0)
    row = lambda i: (i, 0)
    return pl.pallas_call(
        _resid_kernel,
        grid=(t // tt,),
        in_specs=[pl.BlockSpec((tt, d), row), pl.BlockSpec((tt, d), row), pl.BlockSpec((1, 1, 6 * d), mod_map)],
        out_specs=pl.BlockSpec((tt, d), row),
        out_shape=jax.ShapeDtypeStruct((t, d), F32),
        compiler_params=_cparams(("arbitrary",)),
        name="residual",
    )(x, y, mod)


def _rope_tables(length):
    rows = length // GRID_W
    row = jnp.repeat(jnp.arange(rows, dtype=F32), GRID_W)
    col = jnp.tile(jnp.arange(GRID_W, dtype=F32), rows)
    inv = ROPE_BASE ** (-jnp.arange(0, ROPE_HALF, 2, dtype=F32) / ROPE_HALF)
    ang_r = row[:, None] * inv[None, :]
    ang_c = col[:, None] * inv[None, :]
    cos = jnp.concatenate([jnp.cos(ang_r), jnp.cos(ang_r), jnp.cos(ang_c), jnp.cos(ang_c)], axis=-1)
    sin = jnp.concatenate([-jnp.sin(ang_r), jnp.sin(ang_r), -jnp.sin(ang_c), jnp.sin(ang_c)], axis=-1)
    reps = DIFF_WIDTH // DIFF_HEAD_DIM
    return jnp.tile(cos, (1, reps)), jnp.tile(sin, (1, reps))


def _prepare(g_norm1, g_norm2, w_in, w_gate_fwd, b_gate_fwd, w_gate_bwd, b_gate_bwd, g_q_norm, g_k_norm,
             w_out, w_router, b_router, w_exp_gate, w_exp_up, w_exp_down):
    depth, d, _ = w_in.shape
    r = GLA_GATE_RANK
    gate_cols = jnp.pad(w_in[..., 2048:2048 + 2 * r], ((0, 0), (0, 0), (0, GATE_COLS - 2 * r)))
    w_in_r = jnp.concatenate([w_in[..., :2048], w_in[..., 2048 + 2 * r:], gate_cols], axis=-1).astype(BF16)
    w_gate = jnp.zeros((depth, GATE_COLS, 2 * GLA_WIDTH), F32)
    w_gate = w_gate.at[:, 0:r, 0:GLA_WIDTH].set(w_gate_fwd).at[:, r:2 * r, GLA_WIDTH:].set(w_gate_bwd).astype(BF16)
    b_gate = jnp.concatenate([b_gate_fwd, b_gate_bwd], axis=-1).reshape(depth, 1, 2 * GLA_WIDTH)
    grp = jnp.arange(DIFF_WIDTH) // DIFF_HEAD_DIM
    gmat = jnp.where(grp[:, None] == grp[None, :], 1.0 / DIFF_HEAD_DIM, 0.0).astype(BF16)
    reps = DIFF_WIDTH // DIFF_HEAD_DIM
    tok = jnp.arange(TOKEN_TILE)
    same_chunk = (tok[:, None] // GLA_CHUNK) == (tok[None, :] // GLA_CHUNK)
    tri = jnp.stack([same_chunk & (tok[:, None] >= tok[None, :]),
                     same_chunk & (tok[:, None] <= tok[None, :])]).astype(BF16)
    return {
        "tri": tri,
        "g1": g_norm1.reshape(depth, 1, d),
        "g2": g_norm2.reshape(depth, 1, d),
        "w_in": w_in_r,
        "w_gate": w_gate,
        "b_gate": b_gate,
        "gmat": gmat,
        "gq": jnp.tile(g_q_norm, (1, reps)).reshape(depth, 1, DIFF_WIDTH),
        "gk": jnp.tile(g_k_norm, (1, reps)).reshape(depth, 1, DIFF_WIDTH),
        "w_out": w_out.astype(BF16),
        "w_router_t": w_router.T,
        "b_router": b_router.reshape(N_EXPERTS, 1),
        "w_eg": w_exp_gate.astype(BF16),
        "w_eu": w_exp_up.astype(BF16),
        "w_ed": w_exp_down.astype(BF16),
    }


def kernel(x_prompt, x_sample, cache_k, cache_v, state_gla_fwd, state_gla_bwd, c, c_ctx, w_mod, b_mod, g_norm1,
           g_norm2, w_in, w_gate_fwd, b_gate_fwd, w_gate_bwd, b_gate_bwd, g_gla_out, g_q_norm, g_k_norm, lambda_q1,
           lambda_k1, lambda_q2, lambda_k2, g_diff_out, w_out, w_router, b_router, w_exp_gate, w_exp_up, w_exp_down):
    nb_p, seq_p, d = x_prompt.shape
    nb_s, seq_s, _ = x_sample.shape
    depth = w_in.shape[0]
    past = cache_k.shape[2]

    prep = _prepare(g_norm1, g_norm2, w_in, w_gate_fwd, b_gate_fwd, w_gate_bwd, b_gate_bwd, g_q_norm, g_k_norm,
                    w_out, w_router, b_router, w_exp_gate, w_exp_up, w_exp_down)
    rope = _rope_tables(seq_s)

    mod_rows = 8 * ((1 + nb_s + 7) // 8)
    cs = jnp.zeros((mod_rows, d), F32).at[0].set(c_ctx).at[1:1 + nb_s].set(c)
    mod_all = _modulation(cs, w_mod, b_mod)
    ctx_k = cache_k.reshape(nb_s, depth, past, DIFF_WIDTH)
    ctx_v = cache_v.reshape(nb_s, depth, past, DIFF_WIDTH)
    gg = g_gla_out.reshape(depth, 1, GLA_DV)
    gd = g_diff_out.reshape(depth, DIFF_V_DIM, 1)
    lams = [a.reshape(depth, 1, DIFF_HEAD_DIM) for a in (lambda_q1, lambda_k1, lambda_q2, lambda_k2)]

    xp = x_prompt.reshape(nb_p * seq_p, d)
    xs = x_sample.reshape(nb_s * seq_s, d)
    yp = ys = None
    mod_prev = None
    new_k, new_v, new_sf, new_sb = [], [], [], []
    for l in range(depth):
        lam_init = 0.8 - 0.6 * math.exp(-0.3 * l)
        mod = mod_all[l].reshape(mod_rows, 1, 6 * d)
        xp, gla_in, la, qt, kd, vt, k_l, v_l = _pre_mixer(l, xp, yp, mod_prev, mod, prep, None, nb_p, seq_p, False)
        og, sf_l, sb_l = _gla(l, gla_in, la, gg, None, None, nb_p, seq_p)
        od = _diff_attention(l, lam_init, qt, kd, vt, None, None, lams, gd, nb_p, seq_p)
        xp, h2, route = _post_mixer(l, og, od, xp, mod, prep, seq_p, False)
        yp = _moe(l, h2, route, prep)
        new_k.append(k_l)
        new_v.append(v_l)
        new_sf.append(sf_l)
        new_sb.append(sb_l)
        xs, gla_in, la, qt, kd, vt = _pre_mixer(l, xs, ys, mod_prev, mod, prep, rope, nb_s, seq_s, True)
        og, _, _ = _gla(l, gla_in, la, gg, state_gla_fwd, state_gla_bwd, nb_s, seq_s)
        od = _diff_attention(l, lam_init, qt, kd, vt, ctx_k, ctx_v, lams, gd, nb_s, seq_s)
        xs, h2, route = _post_mixer(l, og, od, xs, mod, prep, seq_s, True)
        ys = _moe(l, h2, route, prep)
        mod_prev = mod
    xp = _residual(xp, yp, mod_prev, seq_p, False)
    xs = _residual(xs, ys, mod_prev, seq_s, True)

    new_cache_k = jnp.concatenate(new_k, axis=1).reshape(nb_p, depth, seq_p, DIFF_HEADS, 2, DIFF_HEAD_DIM)
    new_cache_v = jnp.concatenate(new_v, axis=1).reshape(nb_p, depth, seq_p, DIFF_HEADS, DIFF_V_DIM)
    new_sf = jnp.stack(new_sf, axis=1)
    new_sb = jnp.stack(new_sb, axis=1)
    return (xp.reshape(nb_p, seq_p, d), xs.reshape(nb_s, seq_s, d), new_cache_k, new_cache_v, new_sf, new_sb)
```

```python
import functools
import math

import jax
import jax.numpy as jnp
from jax import lax
from jax.experimental import pallas as pl
from jax.experimental.pallas import tpu as pltpu

F32 = jnp.float32
BF16 = jnp.bfloat16

D_MODEL = 1024
GLA_HEADS = 4
GLA_DK = 128
GLA_DV = 128
GLA_WIDTH = GLA_HEADS * GLA_DK
GLA_GATE_RANK = 16
GLA_GATE_NORMALIZER = 16.0
GLA_CHUNK = 64
GLA_HEADS_PER_STEP = 4
DIFF_HEADS = 4
DIFF_HEAD_DIM = 64
DIFF_V_DIM = 128
DIFF_WIDTH = DIFF_HEADS * 2 * DIFF_HEAD_DIM
ROPE_HALF = DIFF_HEAD_DIM // 2
ROPE_BASE = 10000.0
GRID_W = 64
N_EXPERTS = 16
N_GROUPS = 4
EXPERTS_PER_GROUP = 4
D_EXPERT = 512
NORM_EPS = 1e-6
LOG2E = 1.4426950408889634

PAIR_A = (0, 0, 0, 1, 1, 3)
PAIR_B = (1, 2, 3, 3, 2, 2)
N_PAIRS = len(PAIR_A)
N_COMBOS = N_GROUPS * N_PAIRS

GATE_COLS = 128
MAIN_COLS = 7 * 512
TOKEN_TILE = 256
POST_TILE = 1024
PRE_TILE = 512
MOE_TILE = 256
DMA_UNROLL = 8
ATTN_TQ = 1024
ATTN_TK = 256
ATTN_GROUP = 128
ATTN_UNROLL = 16
ATTN_LOOKAHEAD = 8
ONES_ROWS = 16
VMEM_LIMIT = 56 * 1024 * 1024


def _cparams(sem):
    return pltpu.CompilerParams(dimension_semantics=sem, vmem_limit_bytes=VMEM_LIMIT)


def _dot(a, b):
    return jnp.dot(a, b, preferred_element_type=F32)


def _dot_nt(a, b):
    return lax.dot_general(a, b, (((1,), (1,)), ((), ())), preferred_element_type=F32)


def _dot_tn(a, b):
    return lax.dot_general(a, b, (((0,), (0,)), ((), ())), preferred_element_type=F32)


def _sigmoid(x):
    return 1.0 / (1.0 + jnp.exp(-x))


def _split_bf16(x):
    hi = x.astype(BF16)
    lo = (x - hi.astype(F32)).astype(BF16)
    return hi, lo


def _mod_kernel(c_ref, w_ref, b_ref, o_ref):
    c = c_ref[...]
    s = c * _sigmoid(c)
    o_ref[0] = jnp.dot(s, w_ref[0], preferred_element_type=F32,
                       precision=lax.Precision.HIGHEST) + b_ref[0]


def _modulation(cs, w_mod, b_mod):
    depth, d, n = w_mod.shape
    rows = cs.shape[0]
    tn = 1536
    return pl.pallas_call(
        _mod_kernel,
        grid=(depth, n // tn),
        in_specs=[pl.BlockSpec((rows, d), lambda l, j: (0, 0)),
                  pl.BlockSpec((1, d, tn), lambda l, j: (l, 0, j)),
                  pl.BlockSpec((1, 1, tn), lambda l, j: (l, 0, j))],
        out_specs=pl.BlockSpec((1, rows, tn), lambda l, j: (l, 0, j)),
        out_shape=jax.ShapeDtypeStruct((depth, rows, n), F32),
        compiler_params=_cparams(("arbitrary", "arbitrary")),
        name="modulation",
    )(cs, w_mod, b_mod.reshape(depth, 1, n))


def _rope(y, cos, sin):
    w = y.shape[1]
    lane = lax.broadcasted_iota(jnp.int32, y.shape, 1)
    first = (lane % (2 * (ROPE_HALF // 2))) < (ROPE_HALF // 2)
    partner = jnp.where(first, pltpu.roll(y, w - ROPE_HALF // 2, axis=1), pltpu.roll(y, ROPE_HALF // 2, axis=1))
    return y * cos + partner * sin


def _pre_kernel(has_moe, is_sample, *refs):
    refs = list(refs)
    x_ref = refs.pop(0)
    if has_moe:
        y_ref, modp_ref = refs[:2]
        refs = refs[2:]
    mod_ref, g1_ref, win_ref, wgate_ref, bgate_ref, tri_ref, gmat_ref, gq_ref, gk_ref = refs[:9]
    refs = refs[9:]
    if is_sample:
        cos_ref, sin_ref = refs[:2]
        refs = refs[2:]
    xo_ref = refs.pop(0) if has_moe else None
    gla_ref, la_ref, qt_ref, kd_ref, vt_ref = refs[:5]
    refs = refs[5:]
    if not is_sample:
        kc_ref, vc_ref = refs

    d = D_MODEL
    sub = TOKEN_TILE
    m = mod_ref[0]
    gs = g1_ref[0] * (1.0 + m[:, d:2 * d])
    scale = DIFF_HEAD_DIM ** -0.5 * LOG2E
    w = GLA_WIDTH
    subs = [slice(j * sub, (j + 1) * sub) for j in range(x_ref.shape[0] // sub)]
    hbs = []
    for rs in subs:
        x = x_ref[rs, :]
        if has_moe:
            x = x + modp_ref[0][:, 5 * d:6 * d] * y_ref[rs, :].astype(F32)
            xo_ref[rs, :] = x
        ms = jnp.mean(x * x, axis=-1, keepdims=True)
        hbs.append((x * lax.rsqrt(ms + NORM_EPS) * gs + m[:, 0:d]).astype(BF16))

    def put_t(ref, j, rs, val):
        if is_sample:
            ref[0, :, rs] = val.astype(BF16).T
        else:
            ref[j] = val.astype(BF16).T

    hb_all = jnp.concatenate(hbs, axis=0)

    def plain(c):
        p = _dot(hb_all, win_ref[0, :, c * 512:(c + 1) * 512])
        if c == 0:
            p = p * (GLA_DK ** -0.5)
        gla_ref[:, c * 512:(c + 1) * 512] = p.astype(BF16)

    q1 = [_dot(hb, win_ref[0, :, 2048:2560]) for hb in hbs]
    k1 = [_dot(hb, win_ref[0, :, 2560:3072]) for hb in hbs]
    a1 = [_dot(hb, win_ref[0, :, MAIN_COLS:MAIN_COLS + GATE_COLS]) for hb in hbs]
    plain(0)
    msq = [_dot((p * p).astype(BF16), gmat_ref[...]) for p in q1]
    msk = [_dot((p * p).astype(BF16), gmat_ref[...]) for p in k1]
    zs = [_dot(a.astype(BF16), wgate_ref[0]) + bgate_ref[0] for a in a1]
    plain(1)
    for rs, z in zip(subs, zs):
        la = (jnp.minimum(z, 0.0) - jnp.log1p(jnp.exp(-jnp.abs(z)))) * (1.0 / GLA_GATE_NORMALIZER)
        la_hi, la_lo = _split_bf16(la)
        la_ref[rs, 0:w] = _dot(tri_ref[0], la_hi[:, 0:w]) + _dot(tri_ref[0], la_lo[:, 0:w])
        la_ref[rs, w:2 * w] = _dot(tri_ref[1], la_hi[:, w:2 * w]) + _dot(tri_ref[1], la_lo[:, w:2 * w])
    for j, rs in enumerate(subs):
        q = q1[j] * lax.rsqrt(msq[j] + NORM_EPS) * gq_ref[0]
        k = k1[j] * lax.rsqrt(msk[j] + NORM_EPS) * gk_ref[0]
        if is_sample:
            q = _rope(q, cos_ref[rs, :], sin_ref[rs, :])
            k = _rope(k, cos_ref[rs, :], sin_ref[rs, :])
        else:
            kc_ref[j, 0] = k
        put_t(qt_ref, j, rs, q * scale)
        kd_ref[rs, :] = k.astype(BF16)
    plain(2)
    for j, (rs, hb) in enumerate(zip(subs, hbs)):
        v = _dot(hb, win_ref[0, :, 3072:3584])
        if not is_sample:
            vc_ref[j, 0] = v
        put_t(vt_ref, j, rs, v)
    plain(3)


def _pre_mixer(l, x, y, mod_prev, mod, prep, rope, nb, seq, is_sample):
    t, d = x.shape
    sub = TOKEN_TILE
    tt = PRE_TILE
    has_moe = y is not None
    if is_sample:
        assert seq % tt == 0
        tiles_per_batch = seq // tt
        mod_map = lambda i: (1 + i // tiles_per_batch, 0, 0)
    else:
        assert seq == sub and t % tt == 0
        mod_map = lambda i: (0, 0, 0)
    row = lambda i: (i, 0)
    const2 = lambda i: (0, 0)
    lay3 = lambda i: (l, 0, 0)

    ins = [x]
    in_specs = [pl.BlockSpec((tt, d), row)]
    if has_moe:
        ins += [y, mod_prev]
        in_specs += [pl.BlockSpec((tt, d), row), pl.BlockSpec((1, 1, 6 * d), mod_map)]
    ins += [mod, prep["g1"], prep["w_in"], prep["w_gate"], prep["b_gate"], prep["tri"], prep["gmat"], prep["gq"],
            prep["gk"]]
    in_specs += [pl.BlockSpec((1, 1, 6 * d), mod_map),
                 pl.BlockSpec((1, 1, d), lay3),
                 pl.BlockSpec((1, d, MAIN_COLS + GATE_COLS), lay3, pipeline_mode=pl.Buffered(1)),
                 pl.BlockSpec((1, GATE_COLS, 2 * GLA_WIDTH), lay3),
                 pl.BlockSpec((1, 1, 2 * GLA_WIDTH), lay3),
                 pl.BlockSpec((2, sub, sub), lambda i: (0, 0, 0)),
                 pl.BlockSpec((DIFF_WIDTH, DIFF_WIDTH), const2),
                 pl.BlockSpec((1, 1, DIFF_WIDTH), lay3),
                 pl.BlockSpec((1, 1, DIFF_WIDTH), lay3)]
    if is_sample:
        ins += [rope[0], rope[1]]
        in_specs += [pl.BlockSpec((tt, DIFF_WIDTH), lambda i: (i % tiles_per_batch, 0))] * 2

    out_shape = []
    out_specs = []
    if has_moe:
        out_shape.append(jax.ShapeDtypeStruct((t, d), F32))
        out_specs.append(pl.BlockSpec((tt, d), row))
    if is_sample:
        tr_spec = pl.BlockSpec((1, DIFF_WIDTH, tt), lambda i: (i // tiles_per_batch, 0, i % tiles_per_batch))
    else:
        tr_spec = pl.BlockSpec((tt // sub, DIFF_WIDTH, seq), lambda i: (i, 0, 0))
    out_shape += [jax.ShapeDtypeStruct((t, 4 * GLA_WIDTH), BF16),
                  jax.ShapeDtypeStruct((t, 2 * GLA_WIDTH), F32),
                  jax.ShapeDtypeStruct((nb, DIFF_WIDTH, seq), BF16),
                  jax.ShapeDtypeStruct((t, DIFF_WIDTH), BF16),
                  jax.ShapeDtypeStruct((nb, DIFF_WIDTH, seq), BF16)]
    out_specs += [pl.BlockSpec((tt, 4 * GLA_WIDTH), row),
                  pl.BlockSpec((tt, 2 * GLA_WIDTH), row),
                  tr_spec,
                  pl.BlockSpec((tt, DIFF_WIDTH), row),
                  tr_spec]
    if not is_sample:
        out_shape += [jax.ShapeDtypeStruct((nb, 1, seq, DIFF_WIDTH), F32)] * 2
        out_specs += [pl.BlockSpec((tt // sub, 1, seq, DIFF_WIDTH), lambda i: (i, 0, 0, 0))] * 2

    outs = pl.pallas_call(
        functools.partial(_pre_kernel, has_moe, is_sample),
        grid=(t // tt,),
        in_specs=in_specs,
        out_specs=out_specs,
        out_shape=out_shape,
        compiler_params=_cparams(("arbitrary",)),
        name="pre_mixer",
    )(*ins)
    outs = list(outs)
    x_new = outs.pop(0) if has_moe else x
    return [x_new] + outs


def _gla_kernel(has_state, n_chunks, hg, *refs):
    refs = list(refs)
    q_ref, k_ref, v_ref, g_ref, bf_ref, bb_ref, gg_ref = refs[:7]
    refs = refs[7:]
    if has_state:
        s0f_ref, s0b_ref = refs[:2]
        refs = refs[2:]
    o_ref, sf_ref, sb_ref, st_ref, acc_ref = refs

    c = GLA_CHUNK
    dk = GLA_DK
    for hd in range(hg):
        if has_state:
            st_ref[2 * hd] = s0f_ref[0, 0, hd].T
            st_ref[2 * hd + 1] = s0b_ref[0, 0, hd].T
        else:
            st_ref[2 * hd] = jnp.zeros((GLA_DV, dk), F32)
            st_ref[2 * hd + 1] = jnp.zeros((GLA_DV, dk), F32)

    r = lax.broadcasted_iota(jnp.int32, (c, c), 0)
    s = lax.broadcasted_iota(jnp.int32, (c, c), 1)
    lower = r >= s
    upper = r <= s

    def scores(rows, hd, b_ref, st_i, mid_row, last_row):
        cs = slice(hd * dk, (hd + 1) * dk)
        q = q_ref[rows, cs].astype(F32)
        k = k_ref[rows, cs].astype(F32)
        v = v_ref[rows, cs]
        b = b_ref[rows, cs]
        mid = b[mid_row:mid_row + 1]
        last = b[last_row:last_row + 1]
        qe = q * jnp.exp(b - mid)
        ke = k * jnp.exp(mid - b)
        qi = (qe * jnp.exp(mid)).astype(BF16)
        ks = (ke * jnp.exp(last - mid)).astype(BF16)
        qe = qe.astype(BF16)
        ke = ke.astype(BF16)
        st = st_ref[st_i]
        return _dot_nt(qe, ke), _dot_nt(qi, st.astype(BF16)), _dot_tn(v, ks), v, st, last

    def outputs(sc, mask, st_i):
        att, o_inter, kv, v, st, last = sc
        st_ref[st_i] = st * jnp.exp(last) + kv
        return _dot(jnp.where(mask, att, 0.0).astype(BF16), v) + o_inter

    def finish(o, rows, hd):
        cs = slice(hd * dk, (hd + 1) * dk)
        ms = jnp.mean(o * o, axis=-1, keepdims=True)
        g = g_ref[rows, cs].astype(F32)
        o_ref[rows, cs] = (o * lax.rsqrt(ms + NORM_EPS) * gg_ref[0] * (g * _sigmoid(g))).astype(BF16)

    def step(n, second_visit):
        rf = pl.ds(pl.multiple_of(n * c, c), c)
        rb = pl.ds(pl.multiple_of((n_chunks - 1 - n) * c, c), c)
        sc = []
        for hd in range(hg):
            sc.append(scores(rf, hd, bf_ref, 2 * hd, c // 2, c - 1))
            sc.append(scores(rb, hd, bb_ref, 2 * hd + 1, c - 1 - c // 2, 0))
        for hd in range(hg):
            cs = slice(hd * dk, (hd + 1) * dk)
            for rows, sci, mask, st_i in ((rf, sc[2 * hd], lower, 2 * hd), (rb, sc[2 * hd + 1], upper, 2 * hd + 1)):
                o = outputs(sci, mask, st_i)
                if second_visit:
                    finish(acc_ref[rows, cs] + o, rows, hd)
                else:
                    acc_ref[rows, cs] = o

    half = n_chunks // 2
    per_trip = max(u for u in (1, 2, 4) if half % u == 0)

    def trips(first, second_visit):
        def body(n, carry):
            for u in range(per_trip):
                step(first + n * per_trip + u, second_visit)
            return carry
        lax.fori_loop(0, half // per_trip, body, 0)

    trips(0, False)
    trips(half, True)

    for hd in range(hg):
        sf_ref[0, hd] = st_ref[2 * hd].T
        sb_ref[0, hd] = st_ref[2 * hd + 1].T


def _gla(l, gla_in, bsum, gg, s0f, s0b, nb, seq):
    t = gla_in.shape[0]
    h = GLA_HEADS
    hg = GLA_HEADS_PER_STEP
    nhb = h // hg
    w = hg * GLA_DK
    has_state = s0f is not None
    n_chunks = seq // GLA_CHUNK
    assert n_chunks % 2 == 0
    col = lambda off: (lambda b, hh: (b, off + hh))
    in_bytes = seq * w * (4 * 2 + 2 * 4)
    mode = pl.Buffered(1) if 2 * in_bytes > VMEM_LIMIT // 2 else None
    big = lambda off: pl.BlockSpec((seq, w), col(off), pipeline_mode=mode)
    ins = [gla_in, gla_in, gla_in, gla_in, bsum, bsum, gg]
    in_specs = [big(0), big(nhb), big(2 * nhb), big(3 * nhb), big(0), big(nhb),
                pl.BlockSpec((1, 1, GLA_DV), lambda b, hh: (l, 0, 0))]
    if has_state:
        ins += [s0f, s0b]
        in_specs += [pl.BlockSpec((1, 1, hg, GLA_DK, GLA_DV), lambda b, hh: (b, l, hh, 0, 0))] * 2
    return pl.pallas_call(
        functools.partial(_gla_kernel, has_state, n_chunks, hg),
        grid=(nb, nhb),
        in_specs=in_specs,
        out_specs=[pl.BlockSpec((seq, w), col(0)),
                   pl.BlockSpec((1, hg, GLA_DK, GLA_DV), lambda b, hh: (b, hh, 0, 0)),
                   pl.BlockSpec((1, hg, GLA_DK, GLA_DV), lambda b, hh: (b, hh, 0, 0))],
        out_shape=[jax.ShapeDtypeStruct((t, GLA_WIDTH), BF16),
                   jax.ShapeDtypeStruct((nb, h, GLA_DK, GLA_DV), F32),
                   jax.ShapeDtypeStruct((nb, h, GLA_DK, GLA_DV), F32)],
        scratch_shapes=[pltpu.VMEM((2 * hg, GLA_DV, GLA_DK), F32), pltpu.VMEM((seq, w), F32)],
        compiler_params=_cparams(("arbitrary", "arbitrary")),
        name="gla",
    )(*ins)


def _attn_kernel(lam_init, has_ctx, n_kt, tk, *refs):
    refs = list(refs)
    qt_ref, k_ref, vt_ref = refs[:3]
    refs = refs[3:]
    if has_ctx:
        ck_ref, cv_ref = refs[:2]
        refs = refs[2:]
    lq1_ref, lk1_ref, lq2_ref, lk2_ref, gd_ref, o_ref = refs

    qt = qt_ref[0]
    tq = qt.shape[1]
    hw = 2 * DIFF_HEAD_DIM
    heads = qt.shape[0] // hw
    gq = ATTN_GROUP
    n_groups = heads * (tq // gq)
    head_of = lambda e: e // (tq // gq)
    cols_of = lambda e: slice((e % (tq // gq)) * gq, (e % (tq // gq) + 1) * gq)
    dim = lax.broadcasted_iota(jnp.int32, (hw, gq), 0)
    zero = jnp.zeros((hw, gq), BF16)
    qws = []
    for e in range(n_groups):
        qg = qt[head_of(e) * hw:(head_of(e) + 1) * hw, cols_of(e)]
        qws.append(jnp.concatenate([jnp.where(dim < DIFF_HEAD_DIM, qg, zero),
                                    jnp.where(dim >= DIFF_HEAD_DIM, qg, zero)], axis=1))

    def absorb(group_state, st, vt):
        m, acc = group_state
        m_new = jnp.maximum(m, jnp.max(st, axis=0, keepdims=True))
        alpha = jnp.exp2(m - m_new)
        p = jnp.exp2(st - m_new)
        return m_new, alpha * acc + _dot(vt, p.astype(BF16))

    def process(state, tiles):
        units = [(u, e) for u in range(len(tiles)) for e in range(n_groups)]

        def score(idx):
            u, e = units[idx]
            return _dot(tiles[u][0][head_of(e)], qws[e])

        state = list(state)
        pending = [score(idx) for idx in range(min(ATTN_LOOKAHEAD, len(units)))]
        for idx, (u, e) in enumerate(units):
            if idx + ATTN_LOOKAHEAD < len(units):
                pending.append(score(idx + ATTN_LOOKAHEAD))
            state[e] = absorb(state[e], pending.pop(0), tiles[u][1][head_of(e)])
        return tuple(state)

    def with_ones(vt):
        return jnp.concatenate([vt, jnp.ones((ONES_ROWS, vt.shape[1]), BF16)], axis=0)

    state = tuple((jnp.full((1, 2 * gq), -jnp.inf, F32), jnp.zeros((DIFF_V_DIM + ONES_ROWS, 2 * gq), F32))
                  for _ in range(n_groups))
    ctx_tiles = []
    if has_ctx:
        assert heads == 1
        past = ck_ref.shape[2]
        ctk = tk if past % tk == 0 else past
        cvt = cv_ref[0, 0].T.astype(BF16)
        ctx_tiles = [([ck_ref[0, 0, j * ctk:(j + 1) * ctk, :].astype(BF16)],
                      [with_ones(cvt[:, j * ctk:(j + 1) * ctk])]) for j in range(past // ctk)]

    def latent_tile(rows):
        return ([k_ref[rows, hd * hw:(hd + 1) * hw] for hd in range(heads)],
                [with_ones(vt_ref[0, hd * DIFF_V_DIM:(hd + 1) * DIFF_V_DIM, rows]) for hd in range(heads)])

    unroll = max(u for u in range(1, ATTN_UNROLL + 1) if n_kt % u == 0)
    if unroll == n_kt:
        latent = [latent_tile(slice(u * tk, (u + 1) * tk)) for u in range(n_kt)]
        state = process(state, ctx_tiles + latent)
    else:
        if ctx_tiles:
            state = process(state, ctx_tiles)

        def body(j, st):
            return process(st, [latent_tile(pl.ds(pl.multiple_of((j * unroll + u) * tk, tk), tk))
                                for u in range(unroll)])

        state = lax.fori_loop(0, n_kt // unroll, body, state)

    lam = (jnp.exp(jnp.sum(lq1_ref[0] * lk1_ref[0], axis=-1, keepdims=True))
           - jnp.exp(jnp.sum(lq2_ref[0] * lk2_ref[0], axis=-1, keepdims=True)) + lam_init)
    for e in range(n_groups):
        m, acc = state[e]
        on = acc[:DIFF_V_DIM] / acc[DIFF_V_DIM:DIFF_V_DIM + 1]
        o = on[:, :gq] - lam * on[:, gq:]
        ms = jnp.mean(o * o, axis=0, keepdims=True)
        o = o * lax.rsqrt(ms + NORM_EPS) * gd_ref[0] * (1.0 - lam_init)
        o_ref[cols_of(e), head_of(e) * DIFF_V_DIM:(head_of(e) + 1) * DIFF_V_DIM] = o.T.astype(BF16)


def _diff_attention(l, lam_init, qt, kd, vt, ctx_k, ctx_v, lams, gd, nb, seq):
    t = kd.shape[0]
    h = DIFF_HEADS
    has_ctx = ctx_k is not None
    tq = min(ATTN_TQ, seq)
    tk = min(ATTN_TK, seq)
    nq = seq // tq
    hps = h if (not has_ctx and seq * h <= ATTN_TQ) else 1
    w = hps * 2 * DIFF_HEAD_DIM
    wv = hps * DIFF_V_DIM
    ins = [qt, kd, vt]
    in_specs = [pl.BlockSpec((1, w, tq), lambda b, hh, i: (b, hh, i)),
                pl.BlockSpec((seq, w), lambda b, hh, i: (b, hh)),
                pl.BlockSpec((1, wv, seq), lambda b, hh, i: (b, hh, 0))]
    if has_ctx:
        past = ctx_k.shape[2]
        ins += [ctx_k, ctx_v]
        in_specs += [pl.BlockSpec((1, 1, past, w), lambda b, hh, i: (b, l, 0, hh)),
                     pl.BlockSpec((1, 1, past, wv), lambda b, hh, i: (b, l, 0, hh))]
    lay3 = lambda b, hh, i: (l, 0, 0)
    ins += list(lams) + [gd]
    in_specs += [pl.BlockSpec((1, 1, DIFF_HEAD_DIM), lay3)] * 4 + [pl.BlockSpec((1, DIFF_V_DIM, 1), lay3)]
    return pl.pallas_call(
        functools.partial(_attn_kernel, lam_init, has_ctx, seq // tk, tk),
        grid=(nb, h // hps, nq),
        in_specs=in_specs,
        out_specs=pl.BlockSpec((tq, wv), lambda b, hh, i: (b * nq + i, hh)),
        out_shape=jax.ShapeDtypeStruct((t, DIFF_WIDTH), BF16),
        compiler_params=_cparams(("arbitrary", "arbitrary", "arbitrary")),
        name="diff_attention",
    )(*ins)


def _top2_of4(v):
    m1 = jnp.maximum(jnp.maximum(v[0], v[1]), jnp.maximum(v[2], v[3]))
    i1 = jnp.where(v[0] == m1, 0, jnp.where(v[1] == m1, 1, jnp.where(v[2] == m1, 2, 3)))
    neg = jnp.full_like(m1, -jnp.inf)
    w = [jnp.where(i1 == j, neg, v[j]) for j in range(4)]
    m2 = jnp.maximum(jnp.maximum(w[0], w[1]), jnp.maximum(w[2], w[3]))
    i2 = jnp.where(w[0] == m2, 0, jnp.where(w[1] == m2, 1, jnp.where(w[2] == m2, 2, 3)))
    return m1, i1, m2, i2


def _post_kernel(og_ref, od_ref, x_ref, mod_ref, g2_ref, wo_ref, wr_ref, br_ref, x1_ref, hx_ref, route_ref):
    d = D_MODEL
    sub = TOKEN_TILE
    n_sub = x_ref.shape[0] // sub
    m = mod_ref[0]
    rows = [slice(j * sub, (j + 1) * sub) for j in range(n_sub)]
    outs = [_dot(og_ref[rs, :], wo_ref[0, 0:GLA_WIDTH, :]) + _dot(od_ref[rs, :], wo_ref[0, GLA_WIDTH:, :])
            for rs in rows]
    w_hi, w_lo = _split_bf16(wr_ref[...])
    w_hl = jnp.concatenate([w_hi, w_lo], axis=0)
    zs = []
    for rs, out in zip(rows, outs):
        x1 = x_ref[rs, :] + m[:, 2 * d:3 * d] * out
        x1_ref[rs, :] = x1
        ms = jnp.mean(x1 * x1, axis=-1, keepdims=True)
        h2 = x1 * lax.rsqrt(ms + NORM_EPS) * (g2_ref[0] * (1.0 + m[:, 4 * d:5 * d])) + m[:, 3 * d:4 * d]
        h_hi, h_lo = _split_bf16(h2)
        hx_ref[rs, 0:d] = h2
        za = _dot_nt(w_hl, h_hi)
        zs.append(za[0:N_EXPERTS] + za[N_EXPERTS:2 * N_EXPERTS] + _dot_nt(w_hi, h_lo))
    for rs, z in zip(rows, zs):
        _route(z, br_ref, hx_ref, route_ref, rs)


def _route(z, br_ref, hx_ref, route_ref, rs):
    d = D_MODEL
    s = _sigmoid(z)
    sel = s + br_ref[...]

    e = EXPERTS_PER_GROUP
    tops = []
    for g in range(N_GROUPS):
        tops.append(_top2_of4([sel[g * e + j:g * e + j + 1, :] for j in range(e)]))
    score = [t[0] + t[2] for t in tops]
    best = jnp.maximum(jnp.maximum(score[0], score[1]), jnp.maximum(score[2], score[3]))
    gi = jnp.where(score[0] == best, 0, jnp.where(score[1] == best, 1, jnp.where(score[2] == best, 2, 3)))

    def pick(rows):
        return jnp.where(gi == 0, rows[0], jnp.where(gi == 1, rows[1], jnp.where(gi == 2, rows[2], rows[3])))

    i1 = pick([t[1] for t in tops])
    i2 = pick([t[3] for t in tops])

    def gate_of(idx):
        per_group = []
        for g in range(N_GROUPS):
            rows = [s[g * e + j:g * e + j + 1, :] for j in range(e)]
            per_group.append(jnp.where(idx == 0, rows[0], jnp.where(idx == 1, rows[1],
                                                                     jnp.where(idx == 2, rows[2], rows[3]))))
        return pick(per_group)

    s1 = gate_of(i1)
    s2 = gate_of(i2)
    tot = s1 + s2
    w1 = s1 / tot
    w2 = s2 / tot
    lo = jnp.minimum(i1, i2)
    hi = jnp.maximum(i1, i2)
    w_lo = jnp.where(i1 < i2, w1, w2)
    w_hi = jnp.where(i1 < i2, w2, w1)
    pair = jnp.where(lo == 0, hi - 1, jnp.where(lo == 1, jnp.where(hi == 3, 3, 4), 5))
    swapped = pair == 5
    gate_a = jnp.where(swapped, w_hi, w_lo)
    gate_b = jnp.where(swapped, w_lo, w_hi)
    route_ref[:, rs] = jnp.broadcast_to(gi * N_PAIRS + pair, (route_ref.shape[0], gate_a.shape[1]))
    r = lax.broadcasted_iota(jnp.int32, (GATE_COLS, gate_a.shape[1]), 0)
    gates_t = jnp.where(r == 0, gate_a, jnp.where(r == 1, gate_b, 0.0))
    hx_ref[rs, d:d + GATE_COLS] = gates_t.T


def _post_mixer(l, og, od, x, mod, prep, seq, is_sample):
    t, d = x.shape
    tt = POST_TILE if (seq % POST_TILE == 0 or not is_sample) and t % POST_TILE == 0 else TOKEN_TILE
    tiles_per_batch = max(seq // tt, 1)
    if is_sample:
        mod_map = lambda i: (1 + i // tiles_per_batch, 0, 0)
    else:
        mod_map = lambda i: (0, 0, 0)
    row = lambda i: (i, 0)
    lay3 = lambda i: (l, 0, 0)
    return pl.pallas_call(
        _post_kernel,
        grid=(t // tt,),
        in_specs=[pl.BlockSpec((tt, GLA_WIDTH), row),
                  pl.BlockSpec((tt, DIFF_WIDTH), row),
                  pl.BlockSpec((tt, d), row),
                  pl.BlockSpec((1, 1, 6 * d), mod_map),
                  pl.BlockSpec((1, 1, d), lay3),
                  pl.BlockSpec((1, d, d), lay3),
                  pl.BlockSpec((N_EXPERTS, d), lambda i: (0, 0)),
                  pl.BlockSpec((N_EXPERTS, 1), lambda i: (0, 0))],
        out_specs=[pl.BlockSpec((tt, d), row),
                   pl.BlockSpec((tt, d + GATE_COLS), row),
                   pl.BlockSpec((8, tt), lambda i: (0, i))],
        out_shape=[jax.ShapeDtypeStruct((t, d), F32),
                   jax.ShapeDtypeStruct((t, d + GATE_COLS), F32),
                   jax.ShapeDtypeStruct((8, t), jnp.int32)],
        compiler_params=_cparams(("arbitrary",)),
        name="post_mixer",
    )(og, od, x, mod, prep["g2"], prep["w_out"], prep["w_router_t"], prep["b_router"])


def _moe_kernel(n_tiles, order_ref, first_ref, count_ref, ea_ref, eb_ref,
                hx_hbm, wga_ref, wua_ref, wda_ref, wgb_ref, wub_ref, wdb_ref,
                y_hbm, xbuf, ybuf, in_sem, out_sem):
    i = pl.program_id(0)
    slot = i % 2
    d = D_MODEL
    tm = xbuf.shape[1]
    n_tokens = hx_hbm.shape[0]

    def gather_copy(buf_slot, r, tok):
        return pltpu.make_async_copy(hx_hbm.at[pl.ds(tok, 1), :], xbuf.at[buf_slot, pl.ds(r, 1), :],
                                     in_sem.at[buf_slot])

    def scatter_copy(buf_slot, r, row):
        return pltpu.make_async_copy(ybuf.at[buf_slot, pl.ds(r, 1), :], y_hbm.at[pl.ds(row, 1), :],
                                     out_sem.at[buf_slot])

    def for_rows(fn):
        def body(q, c):
            for s in range(DMA_UNROLL):
                fn(q * DMA_UNROLL + s)
            return c
        lax.fori_loop(0, tm // DMA_UNROLL, body, 0)

    def wait_gather(buf_slot):
        for_rows(lambda r: gather_copy(buf_slot, r, 0).wait())

    def wait_scatter(buf_slot):
        for_rows(lambda r: scatter_copy(buf_slot, r, 0).wait())

    def scatter_row(buf_slot, base, n, r):
        return jnp.where(r < n, order_ref[base + r], n_tokens + buf_slot * tm + r)

    prev = jnp.maximum(i - 1, 0)

    @pl.when(i == 0)
    def _():
        ybuf[...] = jnp.zeros_like(ybuf)
        for s in range(2):
            spare_init = pltpu.make_async_copy(ybuf.at[s], y_hbm.at[pl.ds(n_tokens + s * tm, tm), :], out_sem.at[s])
            spare_init.start()
            spare_init.wait()
        base0 = first_ref[0]
        for_rows(lambda r: gather_copy(0, r, order_ref[base0 + r]).start())

    @pl.when((i == 0) | (count_ref[prev] > 0))
    def _():
        wait_gather(slot)

    @pl.when((i == 1) | ((i >= 2) & (count_ref[jnp.maximum(i - 2, 0)] > 0)))
    def _():
        wait_scatter(slot)

    @pl.when(count_ref[i] > 0)
    def _():
        xg = xbuf[slot]
        x = xg[:, 0:d].astype(BF16)
        base_next = first_ref[i + 1]
        base_prev = first_ref[prev]
        n_prev = jnp.where(i > 0, count_ref[prev], 0)
        per_dot = -(-tm // 6)

        def issue(part):
            for r in range(part * per_dot, min((part + 1) * per_dot, tm)):
                gather_copy(1 - slot, r, order_ref[base_next + r]).start(priority=r % 2)
                scatter_copy(1 - slot, r, scatter_row(1 - slot, base_prev, n_prev, r)).start(priority=r % 2)

        hg_a = _dot(x, wga_ref[0, 0])
        issue(0)
        hu_a = _dot(x, wua_ref[0, 0])
        issue(1)
        hg_b = _dot(x, wgb_ref[0, 0])
        issue(2)
        hu_b = _dot(x, wub_ref[0, 0])
        issue(3)
        out_a = _dot(((hg_a * _sigmoid(hg_a)) * hu_a).astype(BF16), wda_ref[0, 0])
        issue(4)
        out_b = _dot(((hg_b * _sigmoid(hg_b)) * hu_b).astype(BF16), wdb_ref[0, 0])
        issue(5)
        ybuf[slot] = xg[:, d:d + 1] * out_a + xg[:, d + 1:d + 2] * out_b

    @pl.when((count_ref[i] == 0) & (i > 0) & (count_ref[prev] > 0))
    def _():
        base_prev = first_ref[prev]
        n_prev = count_ref[prev]
        for_rows(lambda r: scatter_copy(1 - slot, r, scatter_row(1 - slot, base_prev, n_prev, r)).start())

    @pl.when((i == n_tiles - 1) & (count_ref[prev] > 0))
    def _():
        wait_scatter(1 - slot)


def _moe(l, hx, route, prep):
    t = hx.shape[0]
    d = D_MODEL
    tm = MOE_TILE
    n_tiles = t // tm + N_COMBOS

    combo = route[0]
    order = jnp.argsort(combo).astype(jnp.int32)
    order_padded = jnp.concatenate([order, jnp.zeros((tm,), jnp.int32)])
    counts = jnp.sum(combo[:, None] == jnp.arange(N_COMBOS, dtype=jnp.int32)[None, :], axis=0).astype(jnp.int32)
    tiles_of = (counts + tm - 1) // tm
    tile_end = jnp.cumsum(tiles_of)
    start = jnp.cumsum(counts) - counts
    tile = jnp.arange(n_tiles, dtype=jnp.int32)
    n_used = tile_end[-1]
    tile_combo = jnp.sum(jnp.minimum(tile, n_used - 1)[:, None] >= tile_end[None, :], axis=1).astype(jnp.int32)
    onehot = (tile_combo[:, None] == jnp.arange(N_COMBOS, dtype=jnp.int32)[None, :]).astype(jnp.int32)
    pick = lambda table: jnp.sum(onehot * table[None, :], axis=1)
    within = (tile - (pick(tile_end) - pick(tiles_of))) * tm
    first = jnp.clip(pick(start) + within, 0, t - 1).astype(jnp.int32)
    count = jnp.where(tile < n_used, jnp.clip(pick(counts) - within, 0, tm), 0).astype(jnp.int32)
    grp = tile_combo // N_PAIRS
    pr = tile_combo % N_PAIRS
    pair_a = jnp.sum((pr[:, None] == jnp.arange(N_PAIRS)[None, :]) * jnp.asarray(PAIR_A, jnp.int32)[None, :], axis=1)
    pair_b = jnp.sum((pr[:, None] == jnp.arange(N_PAIRS)[None, :]) * jnp.asarray(PAIR_B, jnp.int32)[None, :], axis=1)
    ea = (grp * EXPERTS_PER_GROUP + pair_a).astype(jnp.int32)
    eb = (grp * EXPERTS_PER_GROUP + pair_b).astype(jnp.int32)

    wa = lambda i, o_r, f_r, c_r, ea_r, eb_r: (l, ea_r[i], 0, 0)
    wb = lambda i, o_r, f_r, c_r, ea_r, eb_r: (l, eb_r[i], 0, 0)
    up = pl.BlockSpec((1, 1, d, D_EXPERT), wa)
    dn = pl.BlockSpec((1, 1, D_EXPERT, d), wa)
    upb = pl.BlockSpec((1, 1, d, D_EXPERT), wb)
    dnb = pl.BlockSpec((1, 1, D_EXPERT, d), wb)
    any_spec = pl.BlockSpec(memory_space=pl.ANY)
    return pl.pallas_call(
        functools.partial(_moe_kernel, n_tiles),
        grid_spec=pltpu.PrefetchScalarGridSpec(
            num_scalar_prefetch=5,
            grid=(n_tiles,),
            in_specs=[any_spec, up, up, dn, upb, upb, dnb],
            out_specs=any_spec,
            scratch_shapes=[pltpu.VMEM((2, tm, d + GATE_COLS), F32), pltpu.VMEM((2, tm, d), F32),
                            pltpu.SemaphoreType.DMA((2,)), pltpu.SemaphoreType.DMA((2,))]),
        out_shape=jax.ShapeDtypeStruct((t + 2 * tm, d), F32),
        compiler_params=_cparams(("arbitrary",)),
        name="moe",
    )(order_padded, first, count, ea, eb, hx, prep["w_eg"], prep["w_eu"], prep["w_ed"],
      prep["w_eg"], prep["w_eu"], prep["w_ed"])


def _resid_kernel(x_ref, y_ref, mod_ref, o_ref):
    d = D_MODEL
    o_ref[...] = x_ref[...] + mod_ref[0][:, 5 * d:6 * d] * y_ref[...].astype(F32)


def _residual(x, y, mod, seq, is_sample):
    t, d = x.shape
    tt = TOKEN_TILE
    tiles_per_batch = seq // tt
    if is_sample:
        mod_map = lambda i: (1 + i // tiles_per_batch, 0, 0)
    else:
        mod_map = lambda i: (0, 0, 0)
    row = lambda i: (i, 0)
    return pl.pallas_call(
        _resid_kernel,
        grid=(t // tt,),
        in_specs=[pl.BlockSpec((tt, d), row), pl.BlockSpec((tt, d), row), pl.BlockSpec((1, 1, 6 * d), mod_map)],
        out_specs=pl.BlockSpec((tt, d), row),
        out_shape=jax.ShapeDtypeStruct((t, d), F32),
        compiler_params=_cparams(("arbitrary",)),
        name="residual",
    )(x, y, mod)


def _rope_tables(length):
    rows = length // GRID_W
    row = jnp.repeat(jnp.arange(rows, dtype=F32), GRID_W)
    col = jnp.tile(jnp.arange(GRID_W, dtype=F32), rows)
    inv = ROPE_BASE ** (-jnp.arange(0, ROPE_HALF, 2, dtype=F32) / ROPE_HALF)
    ang_r = row[:, None] * inv[None, :]
    ang_c = col[:, None] * inv[None, :]
    cos = jnp.concatenate([jnp.cos(ang_r), jnp.cos(ang_r), jnp.cos(ang_c), jnp.cos(ang_c)], axis=-1)
    sin = jnp.concatenate([-jnp.sin(ang_r), jnp.sin(ang_r), -jnp.sin(ang_c), jnp.sin(ang_c)], axis=-1)
    reps = DIFF_WIDTH // DIFF_HEAD_DIM
    return jnp.tile(cos, (1, reps)), jnp.tile(sin, (1, reps))


def _prepare(g_norm1, g_norm2, w_in, w_gate_fwd, b_gate_fwd, w_gate_bwd, b_gate_bwd, g_q_norm, g_k_norm,
             w_out, w_router, b_router, w_exp_gate, w_exp_up, w_exp_down):
    depth, d, _ = w_in.shape
    r = GLA_GATE_RANK
    gate_cols = jnp.pad(w_in[..., 2048:2048 + 2 * r], ((0, 0), (0, 0), (0, GATE_COLS - 2 * r)))
    w_in_r = jnp.concatenate([w_in[..., :2048], w_in[..., 2048 + 2 * r:], gate_cols], axis=-1).astype(BF16)
    w_gate = jnp.zeros((depth, GATE_COLS, 2 * GLA_WIDTH), F32)
    w_gate = w_gate.at[:, 0:r, 0:GLA_WIDTH].set(w_gate_fwd).at[:, r:2 * r, GLA_WIDTH:].set(w_gate_bwd).astype(BF16)
    b_gate = jnp.concatenate([b_gate_fwd, b_gate_bwd], axis=-1).reshape(depth, 1, 2 * GLA_WIDTH)
    grp = jnp.arange(DIFF_WIDTH) // DIFF_HEAD_DIM
    gmat = jnp.where(grp[:, None] == grp[None, :], 1.0 / DIFF_HEAD_DIM, 0.0).astype(BF16)
    reps = DIFF_WIDTH // DIFF_HEAD_DIM
    tok = jnp.arange(TOKEN_TILE)
    same_chunk = (tok[:, None] // GLA_CHUNK) == (tok[None, :] // GLA_CHUNK)
    tri = jnp.stack([same_chunk & (tok[:, None] >= tok[None, :]),
                     same_chunk & (tok[:, None] <= tok[None, :])]).astype(BF16)
    return {
        "tri": tri,
        "g1": g_norm1.reshape(depth, 1, d),
        "g2": g_norm2.reshape(depth, 1, d),
        "w_in": w_in_r,
        "w_gate": w_gate,
        "b_gate": b_gate,
        "gmat": gmat,
        "gq": jnp.tile(g_q_norm, (1, reps)).reshape(depth, 1, DIFF_WIDTH),
        "gk": jnp.tile(g_k_norm, (1, reps)).reshape(depth, 1, DIFF_WIDTH),
        "w_out": w_out.astype(BF16),
        "w_router_t": w_router.T,
        "b_router": b_router.reshape(N_EXPERTS, 1),
        "w_eg": w_exp_gate.astype(BF16),
        "w_eu": w_exp_up.astype(BF16),
        "w_ed": w_exp_down.astype(BF16),
    }


def kernel(x_prompt, x_sample, cache_k, cache_v, state_gla_fwd, state_gla_bwd, c, c_ctx, w_mod, b_mod, g_norm1,
           g_norm2, w_in, w_gate_fwd, b_gate_fwd, w_gate_bwd, b_gate_bwd, g_gla_out, g_q_norm, g_k_norm, lambda_q1,
           lambda_k1, lambda_q2, lambda_k2, g_diff_out, w_out, w_router, b_router, w_exp_gate, w_exp_up, w_exp_down):
    nb_p, seq_p, d = x_prompt.shape
    nb_s, seq_s, _ = x_sample.shape
    depth = w_in.shape[0]
    past = cache_k.shape[2]

    prep = _prepare(g_norm1, g_norm2, w_in, w_gate_fwd, b_gate_fwd, w_gate_bwd, b_gate_bwd, g_q_norm, g_k_norm,
                    w_out, w_router, b_router, w_exp_gate, w_exp_up, w_exp_down)
    rope = _rope_tables(seq_s)

    mod_rows = 8 * ((1 + nb_s + 7) // 8)
    cs = jnp.zeros((mod_rows, d), F32).at[0].set(c_ctx).at[1:1 + nb_s].set(c)
    mod_all = _modulation(cs, w_mod, b_mod)
    ctx_k = cache_k.reshape(nb_s, depth, past, DIFF_WIDTH)
    ctx_v = cache_v.reshape(nb_s, depth, past, DIFF_WIDTH)
    gg = g_gla_out.reshape(depth, 1, GLA_DV)
    gd = g_diff_out.reshape(depth, DIFF_V_DIM, 1)
    lams = [a.reshape(depth, 1, DIFF_HEAD_DIM) for a in (lambda_q1, lambda_k1, lambda_q2, lambda_k2)]

    xp = x_prompt.reshape(nb_p * seq_p, d)
    xs = x_sample.reshape(nb_s * seq_s, d)
    yp = ys = None
    mod_prev = None
    new_k, new_v, new_sf, new_sb = [], [], [], []
    for l in range(depth):
        lam_init = 0.8 - 0.6 * math.exp(-0.3 * l)
        mod = mod_all[l].reshape(mod_rows, 1, 6 * d)
        xp, gla_in, la, qt, kd, vt, k_l, v_l = _pre_mixer(l, xp, yp, mod_prev, mod, prep, None, nb_p, seq_p, False)
        og, sf_l, sb_l = _gla(l, gla_in, la, gg, None, None, nb_p, seq_p)
        od = _diff_attention(l, lam_init, qt, kd, vt, None, None, lams, gd, nb_p, seq_p)
        xp, h2, route = _post_mixer(l, og, od, xp, mod, prep, seq_p, False)
        yp = _moe(l, h2, route, prep)
        new_k.append(k_l)
        new_v.append(v_l)
        new_sf.append(sf_l)
        new_sb.append(sb_l)
        xs, gla_in, la, qt, kd, vt = _pre_mixer(l, xs, ys, mod_prev, mod, prep, rope, nb_s, seq_s, True)
        og, _, _ = _gla(l, gla_in, la, gg, state_gla_fwd, state_gla_bwd, nb_s, seq_s)
        od = _diff_attention(l, lam_init, qt, kd, vt, ctx_k, ctx_v, lams, gd, nb_s, seq_s)
        xs, h2, route = _post_mixer(l, og, od, xs, mod, prep, seq_s, True)
        ys = _moe(l, h2, route, prep)
        mod_prev = mod
    xp = _residual(xp, yp, mod_prev, seq_p, False)
    xs = _residual(xs, ys, mod_prev, seq_s, True)

    new_cache_k = jnp.concatenate(new_k, axis=1).reshape(nb_p, depth, seq_p, DIFF_HEADS, 2, DIFF_HEAD_DIM)
    new_cache_v = jnp.concatenate(new_v, axis=1).reshape(nb_p, depth, seq_p, DIFF_HEADS, DIFF_V_DIM)
    new_sf = jnp.stack(new_sf, axis=1)
    new_sb = jnp.stack(new_sb, axis=1)
    return (xp.reshape(nb_p, seq_p, d), xs.reshape(nb_s, seq_s, d), new_cache_k, new_cache_v, new_sf, new_sb)
```
